```python
import jax, jax.numpy as jnp
from jax import lax
import numpy as np

D_MODEL = 1024
BATCH = 4
SEQ = 8192
DEPTH = 2

HEAD_DIM = 64
SB_HEADS = (3 * D_MODEL) // (8 * HEAD_DIM)
FOX_HEADS = (3 * D_MODEL) // (8 * HEAD_DIM)
SB_DIM = SB_HEADS * HEAD_DIM
FOX_DIM = FOX_HEADS * HEAD_DIM
POOL_WINDOWS = (2, 4, 8, 16)
POOL_DIM = D_MODEL // 4
POOL_GROUP = POOL_DIM // len(POOL_WINDOWS)
MIX_DIM = SB_DIM + FOX_DIM + POOL_DIM
SPLIT_POINTS = [SB_DIM, 2 * SB_DIM, 3 * SB_DIM,
                3 * SB_DIM + FOX_DIM, 3 * SB_DIM + 2 * FOX_DIM, 3 * SB_DIM + 3 * FOX_DIM,
                3 * SB_DIM + 3 * FOX_DIM + FOX_HEADS]
C_IN = 3 * SB_DIM + 3 * FOX_DIM + FOX_HEADS + POOL_DIM
Q_BLOCK = 128
MEM_LEN = 256
XA_HEADS = 4
XA_HEAD_DIM = D_MODEL // XA_HEADS
D_FF = ((8 * D_MODEL // 3 + 255) // 256) * 256
N_EXPERTS = 8
TOP_K = 2
MOE_BLOCK = 256
N_DENSE = (DEPTH + 1) // 2
N_MOE = DEPTH // 2
EPS = 1e-6

kernel_name = 'hymba_sb_fox_pool_moe_trunk'


def rmsnorm(x, g):
    xf = x.astype(jnp.float32)
    y = xf * lax.rsqrt(jnp.mean(xf * xf, axis=-1, keepdims=True) + EPS)
    return (y * g.astype(jnp.float32)).astype(x.dtype)


def _heads(t, n):
    b, s, _ = t.shape
    return t.reshape(b, s, n, HEAD_DIM).transpose(0, 2, 1, 3)


def _merge_blocks(o):
    nb, b, h, q, dh = o.shape
    return o.transpose(1, 0, 3, 2, 4).reshape(b, nb * q, h * dh)


def stick_breaking_attention(q, k, v):
    _, _, s_len, dh = q.shape
    scale = dh ** -0.5
    s_idx = jnp.arange(s_len)

    def block(i):
        t0 = i * Q_BLOCK
        qb = lax.dynamic_slice_in_dim(q, t0, Q_BLOCK, axis=2)
        z = jnp.einsum('bhqd,bhsd->bhqs', qb, k).astype(jnp.float32) * scale
        t_idx = t0 + jnp.arange(Q_BLOCK)
        mask = s_idx[None, :] < t_idx[:, None]
        log_beta = jax.nn.log_sigmoid(z)
        log_keep = jnp.where(mask, log_beta - z, 0.0)
        later = lax.cumsum(log_keep, axis=3, reverse=True) - log_keep
        w = jnp.where(mask, jnp.exp(log_beta + later), 0.0)
        return jnp.einsum('bhqs,bhsd->bhqd', w.astype(v.dtype), v)

    return _merge_blocks(lax.map(block, jnp.arange(s_len // Q_BLOCK)))


def forgetting_attention(q, k, v, log_f_cum):
    _, _, s_len, dh = q.shape
    scale = dh ** -0.5
    s_idx = jnp.arange(s_len)

    def block(i):
        t0 = i * Q_BLOCK
        qb = lax.dynamic_slice_in_dim(q, t0, Q_BLOCK, axis=2)
        cq = lax.dynamic_slice_in_dim(log_f_cum, t0, Q_BLOCK, axis=2)
        logits = (jnp.einsum('bhqd,bhsd->bhqs', qb, k).astype(jnp.float32) * scale
                  + cq[..., :, None] - log_f_cum[..., None, :])
        t_idx = t0 + jnp.arange(Q_BLOCK)
        mask = s_idx[None, :] <= t_idx[:, None]
        p = jax.nn.softmax(jnp.where(mask, logits, -jnp.inf), axis=-1)
        return jnp.einsum('bhqs,bhsd->bhqd', p.astype(v.dtype), v)

    return _merge_blocks(lax.map(block, jnp.arange(s_len // Q_BLOCK)))


def multiscale_pool(u, pool_w, pool_scale):
    _, s_len, _ = u.shape
    pos = jnp.arange(s_len)
    outs = []
    for g, win in enumerate(POOL_WINDOWS):
        ug = u[..., g * POOL_GROUP:(g + 1) * POOL_GROUP].astype(jnp.float32)
        cs = jnp.cumsum(ug, axis=1)
        cs_prev = jnp.pad(cs, ((0, 0), (win, 0), (0, 0)))[:, :s_len]
        count = jnp.minimum(pos + 1, win).astype(jnp.float32)[None, :, None]
        r = ((cs - cs_prev) / count - ug).astype(u.dtype)
        outs.append(r @ pool_w[g])
    return jnp.concatenate(outs, axis=-1) * pool_scale


def hybrid_mixer(hn, w_in, b_forget, pool_w, pool_scale, w_out):
    proj = hn @ w_in
    sq, sk, sv, fq, fk, fv, f_logit, u = jnp.split(proj, SPLIT_POINTS, axis=-1)
    sb = stick_breaking_attention(_heads(sq, SB_HEADS), _heads(sk, SB_HEADS), _heads(sv, SB_HEADS))
    log_f = jax.nn.log_sigmoid((f_logit + b_forget).astype(jnp.float32))
    log_f_cum = jnp.cumsum(log_f, axis=1).transpose(0, 2, 1)
    fox = forgetting_attention(_heads(fq, FOX_HEADS), _heads(fk, FOX_HEADS), _heads(fv, FOX_HEADS), log_f_cum)
    pool = multiscale_pool(u, pool_w, pool_scale)
    return jnp.concatenate([sb, fox, pool], axis=-1) @ w_out


def memory_cross_attention(hn, mn, wq, wkv, wo):
    b, s_len, _ = hn.shape
    m_len = mn.shape[1]
    q = (hn @ wq).reshape(b, s_len, XA_HEADS, XA_HEAD_DIM)
    k, v = jnp.split(mn @ wkv, 2, axis=-1)
    k = k.reshape(b, m_len, XA_HEADS, XA_HEAD_DIM)
    v = v.reshape(b, m_len, XA_HEADS, XA_HEAD_DIM)
    logits = jnp.einsum('bqhd,bmhd->bhqm', q, k).astype(jnp.float32) * (XA_HEAD_DIM ** -0.5)
    p = jax.nn.softmax(logits, axis=-1)
    o = jnp.einsum('bhqm,bmhd->bqhd', p.astype(v.dtype), v).reshape(b, s_len, D_MODEL)
    return o @ wo


def swiglu(h, w_gate, w_up, w_down):
    return (jax.nn.silu(h @ w_gate) * (h @ w_up)) @ w_down


def moe_swiglu(h, router_w, w_gate, w_up, w_down):
    b, s_len, d = h.shape
    xf = h.reshape(-1, d)
    n_tok = xf.shape[0]
    n_pairs = n_tok * TOP_K
    logits = (xf @ router_w).astype(jnp.float32)
    top_val, top_idx = lax.top_k(logits, TOP_K)
    gates = jax.nn.softmax(top_val, axis=-1)
    e_flat = top_idx.reshape(-1)
    tok_flat = jnp.repeat(jnp.arange(n_tok), TOP_K)
    g_flat = gates.reshape(-1)
    order = jnp.argsort(e_flat)
    e_s, tok_s, g_s = e_flat[order], tok_flat[order], g_flat[order]
    counts = jnp.bincount(e_flat, length=N_EXPERTS)
    padded = ((counts + MOE_BLOCK - 1) // MOE_BLOCK) * MOE_BLOCK
    start = jnp.cumsum(counts) - counts
    pend = jnp.cumsum(padded)
    pstart = pend - padded
    dest = pstart[e_s] + (jnp.arange(n_pairs) - start[e_s])
    n_blocks = -(-n_pairs // MOE_BLOCK) + N_EXPERTS
    n_rows = n_blocks * MOE_BLOCK
    buf_tok = jnp.zeros((n_rows,), jnp.int32).at[dest].set(tok_s.astype(jnp.int32))
    buf_g = jnp.zeros((n_rows,), jnp.float32).at[dest].set(g_s)
    blk_e = jnp.clip(jnp.searchsorted(pend, jnp.arange(n_blocks) * MOE_BLOCK, side='right'), 0, N_EXPERTS - 1)
    xs = xf[buf_tok].reshape(n_blocks, MOE_BLOCK, d)

    def expert_block(args):
        xb, e = args
        return swiglu(xb, w_gate[e], w_up[e], w_down[e])

    ys = lax.map(expert_block, (xs, blk_e)).reshape(n_rows, d)
    ys = ys * buf_g[:, None].astype(ys.dtype)
    out = jnp.zeros((n_tok, d), h.dtype).at[buf_tok].add(ys)
    return out.reshape(b, s_len, d)


def setup_inputs(seed: int = 0) -> dict:
    key = jax.random.key(seed)
    ks = jax.random.split(key, 26)
    f32 = jnp.float32

    def nrm(k, shape, fan_in):
        return jax.random.normal(k, shape, f32) * (fan_in ** -0.5)

    def gain(k, shape):
        return 1.0 + 0.05 * jax.random.normal(k, shape, f32)

    return {
        'x': jax.random.normal(ks[0], (BATCH, SEQ, D_MODEL), f32),
        'mem': jax.random.normal(ks[1], (BATCH, MEM_LEN, D_MODEL), f32),
        'mix_norm_pre': gain(ks[2], (DEPTH, D_MODEL)),
        'mix_norm_post': gain(ks[3], (DEPTH, D_MODEL)),
        'w_in': nrm(ks[4], (DEPTH, D_MODEL, C_IN), D_MODEL),
        'b_forget': 2.0 + 0.5 * jax.random.normal(ks[5], (DEPTH, FOX_HEADS), f32),
        'pool_w': nrm(ks[6], (DEPTH, len(POOL_WINDOWS), POOL_GROUP, POOL_GROUP), POOL_GROUP),
        'pool_scale': 1.0 + 0.1 * jax.random.normal(ks[7], (DEPTH, POOL_DIM), f32),
        'w_out': nrm(ks[8], (DEPTH, MIX_DIM, D_MODEL), MIX_DIM),
        'xa_norm_pre': gain(ks[9], (DEPTH, D_MODEL)),
        'xa_norm_post': gain(ks[10], (DEPTH, D_MODEL)),
        'mem_norm': gain(ks[11], (DEPTH, D_MODEL)),
        'xa_wq': nrm(ks[12], (DEPTH, D_MODEL, D_MODEL), D_MODEL),
        'xa_wkv': nrm(ks[13], (DEPTH, D_MODEL, 2 * D_MODEL), D_MODEL),
        'xa_wo': nrm(ks[14], (DEPTH, D_MODEL, D_MODEL), D_MODEL),
        'ffn_norm_pre': gain(ks[15], (DEPTH, D_MODEL)),
        'ffn_norm_post': gain(ks[16], (DEPTH, D_MODEL)),
        'dense_w_gate': nrm(ks[17], (N_DENSE, D_MODEL, D_FF), D_MODEL),
        'dense_w_up': nrm(ks[18], (N_DENSE, D_MODEL, D_FF), D_MODEL),
        'dense_w_down': nrm(ks[19], (N_DENSE, D_FF, D_MODEL), D_FF),
        'router_w': nrm(ks[20], (N_MOE, D_MODEL, N_EXPERTS), D_MODEL),
        'moe_w_gate': nrm(ks[21], (N_MOE, N_EXPERTS, D_MODEL, D_FF), D_MODEL),
        'moe_w_up': nrm(ks[22], (N_MOE, N_EXPERTS, D_MODEL, D_FF), D_MODEL),
        'moe_w_down': nrm(ks[23], (N_MOE, N_EXPERTS, D_FF, D_MODEL), D_FF),
    }


def reference(x, mem, mix_norm_pre, mix_norm_post, w_in, b_forget, pool_w, pool_scale, w_out,
              xa_norm_pre, xa_norm_post, mem_norm, xa_wq, xa_wkv, xa_wo,
              ffn_norm_pre, ffn_norm_post, dense_w_gate, dense_w_up, dense_w_down,
              router_w, moe_w_gate, moe_w_up, moe_w_down):
    h = x
    for i in range(DEPTH):
        a = hybrid_mixer(rmsnorm(h, mix_norm_pre[i]), w_in[i], b_forget[i], pool_w[i], pool_scale[i], w_out[i])
        h = h + rmsnorm(a, mix_norm_post[i])
        c = memory_cross_attention(rmsnorm(h, xa_norm_pre[i]), rmsnorm(mem, mem_norm[i]),
                                   xa_wq[i], xa_wkv[i], xa_wo[i])
        h = h + rmsnorm(c, xa_norm_post[i])
        hn = rmsnorm(h, ffn_norm_pre[i])
        j = i // 2
        if i % 2 == 0:
            f = swiglu(hn, dense_w_gate[j], dense_w_up[j], dense_w_down[j])
        else:
            f = moe_swiglu(hn, router_w[j], moe_w_gate[j], moe_w_up[j], moe_w_down[j])
        h = h + rmsnorm(f, ffn_norm_post[i])
    return h
```

```python
import functools

import jax
import jax.numpy as jnp
from jax import lax
from jax.experimental import pallas as pl
from jax.experimental.pallas import tpu as pltpu

F32 = jnp.float32
BF16 = jnp.bfloat16
EPS = 1e-6

HEAD_DIM = 64
LANES = 128
N_PAIRS = 3
GROUP_DIM = N_PAIRS * LANES
QKV_DIM = 6 * GROUP_DIM
POOL_DIM = 256
POOL_WINDOWS = (2, 4, 8, 16)
POOL_HALO = 16
N_FOX = 6
XA_HEADS = 4
N_EXPERTS = 8
TOP_K = 2
MOE_BLOCK = 256
ATT_BLOCK = 256
VMEM_LIMIT = 56 * 1024 * 1024

_NT = (((1,), (1,)), ((), ()))


def _params(n_axes):
    return pltpu.CompilerParams(dimension_semantics=("arbitrary",) * n_axes,
                                vmem_limit_bytes=VMEM_LIMIT)


def _rms(x, g):
    return x * lax.rsqrt(jnp.mean(x * x, axis=-1, keepdims=True) + EPS) * g


def _dot(a, b):
    return jnp.dot(a, b, preferred_element_type=F32)


def _split3(x):
    hi = x.astype(BF16)
    r1 = x - hi.astype(F32)
    mid = r1.astype(BF16)
    lo = (r1 - mid.astype(F32)).astype(BF16)
    return hi, mid, lo


def _norm_matmul_kernel(x_ref, g_ref, w_ref, *out_refs, splits):
    yb = _rms(x_ref[...], g_ref[...]).astype(BF16)
    for o_ref, (c0, width) in zip(out_refs, splits):
        for c in range(0, width, 256):
            cw = min(256, width - c)
            o_ref[:, c:c + cw] = _dot(yb, w_ref[:, c0 + c:c0 + c + cw]).astype(o_ref.dtype)


def _norm_matmul(x, g, w, splits, dtypes, tm):
    n, d = x.shape
    kern = functools.partial(_norm_matmul_kernel, splits=tuple(splits))
    return pl.pallas_call(
        kern,
        grid=(n // tm,),
        in_specs=[pl.BlockSpec((tm, d), lambda i: (i, 0)),
                  pl.BlockSpec((1, d), lambda i: (0, 0)),
                  pl.BlockSpec(w.shape, lambda i: (0, 0))],
        out_specs=[pl.BlockSpec((tm, wd), lambda i: (i, 0)) for (_, wd) in splits],
        out_shape=[jax.ShapeDtypeStruct((n, wd), dt) for (_, wd), dt in zip(splits, dtypes)],
        compiler_params=_params(1),
        name="norm_matmul",
    )(x, g, w)


def _logf_cumsum_kernel(fl_ref, b_ref, c_ref, carry_ref, *, tc):
    @pl.when(pl.program_id(1) == 0)
    def _():
        carry_ref[...] = jnp.zeros_like(carry_ref)

    x = fl_ref[...] + b_ref[...]
    ls = jnp.minimum(x, 0.0) - jnp.log(1.0 + jnp.exp(-jnp.abs(x)))
    row = lax.broadcasted_iota(jnp.int32, (tc, tc), 0)
    col = lax.broadcasted_iota(jnp.int32, (tc, tc), 1)
    tri = jnp.where(row >= col, 1.0, 0.0).astype(BF16)
    hi, mid, lo = _split3(ls)
    c = _dot(tri, hi) + _dot(tri, mid) + _dot(tri, lo) + carry_ref[...]
    c_ref[...] = c
    carry_ref[...] = c[tc - 1:tc, :]


def _logf_cumsum(fl, b_pad, tc=256):
    bsz, s, _ = fl.shape
    return pl.pallas_call(
        functools.partial(_logf_cumsum_kernel, tc=tc),
        grid=(bsz, s // tc),
        in_specs=[pl.BlockSpec((None, tc, LANES), lambda b, j: (b, j, 0)),
                  pl.BlockSpec((1, LANES), lambda b, j: (0, 0))],
        out_specs=pl.BlockSpec((None, tc, LANES), lambda b, j: (b, j, 0)),
        out_shape=jax.ShapeDtypeStruct(fl.shape, F32),
        scratch_shapes=[pltpu.VMEM((1, LANES), F32)],
        compiler_params=_params(2),
        name="logf_cumsum",
    )(fl, b_pad)


def _head_masks(x):
    lane = lax.broadcasted_iota(jnp.int32, (1, LANES), 1)
    zero = jnp.zeros_like(x)
    return jnp.where(lane < HEAD_DIM, x, zero), jnp.where(lane >= HEAD_DIM, x, zero)


def _sb_kernel(q_ref, k_ref, v_ref, tri_ref, o_ref, acc_ref, *, blk):
    i = pl.program_id(2)
    q_heads = _head_masks(q_ref[...] * jnp.asarray(HEAD_DIM ** -0.5, BF16))
    row = lax.broadcasted_iota(jnp.int32, (blk, blk), 0)
    col = lax.broadcasted_iota(jnp.int32, (blk, blk), 1)
    acc_ref[...] = jnp.zeros_like(acc_ref)

    def tile(j, run, diag):
        start = pl.multiple_of(j * blk, blk)
        k = k_ref[pl.ds(start, blk), :]
        v_heads = _head_masks(v_ref[pl.ds(start, blk), :])
        new_run = []
        upd = None
        for hd in range(2):
            z = lax.dot_general(q_heads[hd], k, _NT, preferred_element_type=F32)
            sp = jnp.maximum(z, 0.0) + jnp.log(1.0 + jnp.exp(-jnp.abs(z)))
            if diag:
                sp = jnp.where(col < row, sp, 0.0)
            suffix = _dot(sp.astype(BF16), tri_ref[...])
            w = jnp.exp(z - suffix - run[hd])
            if diag:
                w = jnp.where(col < row, w, 0.0)
            part = _dot(w.astype(BF16), v_heads[hd])
            upd = part if upd is None else upd + part
            new_run.append(run[hd] + suffix[:, 0:1])
        acc_ref[...] += upd
        return tuple(new_run)

    zero = jnp.zeros((blk, 1), F32)
    run = tile(i, (zero, zero), True)
    lax.fori_loop(0, i, lambda jj, r: tile(i - 1 - jj, r, False), run)
    o_ref[...] = acc_ref[...].astype(o_ref.dtype)


def _sb_attention(qkv, tri, blk=ATT_BLOCK):
    bsz, s, _ = qkv.shape
    return pl.pallas_call(
        functools.partial(_sb_kernel, blk=blk),
        grid=(bsz, N_PAIRS, s // blk),
        in_specs=[pl.BlockSpec((None, blk, LANES), lambda b, p, i: (b, i, p)),
                  pl.BlockSpec((None, s, LANES), lambda b, p, i: (b, 0, N_PAIRS + p)),
                  pl.BlockSpec((None, s, LANES), lambda b, p, i: (b, 0, 2 * N_PAIRS + p)),
                  pl.BlockSpec((blk, blk), lambda b, p, i: (0, 0))],
        out_specs=pl.BlockSpec((None, blk, LANES), lambda b, p, i: (b, i, p)),
        out_shape=jax.ShapeDtypeStruct((bsz, s, GROUP_DIM), BF16),
        scratch_shapes=[pltpu.VMEM((blk, LANES), F32)],
        compiler_params=_params(3),
        name="sb_attention",
    )(qkv, qkv, qkv, tri)


def _fox_kernel(q_ref, k_ref, v_ref, c_ref, o_ref, acc_ref, *, blk):
    i = pl.program_id(2)
    q_heads = _head_masks(q_ref[...] * jnp.asarray(HEAD_DIM ** -0.5, BF16))
    row = lax.broadcasted_iota(jnp.int32, (blk, blk), 0)
    col = lax.broadcasted_iota(jnp.int32, (blk, blk), 1)
    lane = lax.broadcasted_iota(jnp.int32, (1, LANES), 1)
    q0 = pl.multiple_of(i * blk, blk)
    c_base = [c_ref[hd:hd + 1, pl.ds(q0, blk)][:, 0:1] for hd in range(2)]
    acc_ref[...] = jnp.zeros_like(acc_ref)

    def tile(j, carry, diag):
        start = pl.multiple_of(j * blk, blk)
        k = k_ref[pl.ds(start, blk), :]
        v_heads = _head_masks(v_ref[pl.ds(start, blk), :])
        new_carry = []
        alphas = []
        upd = None
        for hd in range(2):
            m_old, l_old = carry[hd]
            z = lax.dot_general(q_heads[hd], k, _NT, preferred_element_type=F32)
            s = z - (c_ref[hd:hd + 1, pl.ds(start, blk)] - c_base[hd])
            if diag:
                s = jnp.where(col <= row, s, -1e30)
            m_new = jnp.maximum(m_old, jnp.max(s, axis=1, keepdims=True))
            alpha = jnp.exp(m_old - m_new)
            p = jnp.exp(s - m_new)
            l_new = alpha * l_old + jnp.sum(p, axis=1, keepdims=True)
            part = _dot(p.astype(BF16), v_heads[hd])
            upd = part if upd is None else upd + part
            alphas.append(alpha)
            new_carry.append((m_new, l_new))
        acc_ref[...] = acc_ref[...] * jnp.where(lane < HEAD_DIM, alphas[0], alphas[1]) + upd
        return tuple(new_carry)

    init = (jnp.full((blk, 1), -1e30, F32), jnp.zeros((blk, 1), F32))
    carry = tile(i, (init, init), True)
    carry = lax.fori_loop(0, i, lambda jj, c: tile(i - 1 - jj, c, False), carry)
    inv = jnp.where(lane < HEAD_DIM, 1.0 / carry[0][1], 1.0 / carry[1][1])
    o_ref[...] = (acc_ref[...] * inv).astype(o_ref.dtype)


def _fox_attention(qkv, c_rows, blk=ATT_BLOCK):
    bsz, s, _ = qkv.shape
    return pl.pallas_call(
        functools.partial(_fox_kernel, blk=blk),
        grid=(bsz, N_PAIRS, s // blk),
        in_specs=[pl.BlockSpec((None, blk, LANES), lambda b, p, i: (b, i, 3 * N_PAIRS + p)),
                  pl.BlockSpec((None, s, LANES), lambda b, p, i: (b, 0, 4 * N_PAIRS + p)),
                  pl.BlockSpec((None, s, LANES), lambda b, p, i: (b, 0, 5 * N_PAIRS + p)),
                  pl.BlockSpec((None, None, 2, s), lambda b, p, i: (b, p, 0, 0))],
        out_specs=pl.BlockSpec((None, blk, LANES), lambda b, p, i: (b, i, p)),
        out_shape=jax.ShapeDtypeStruct((bsz, s, GROUP_DIM), BF16),
        scratch_shapes=[pltpu.VMEM((blk, LANES), F32)],
        compiler_params=_params(3),
        name="fox_attention",
    )(qkv, qkv, qkv, c_rows)


def _mix_out_kernel(sb_ref, fox_ref, u_ref, halo_ref, pw_ref, ps_ref, wo_ref, g_ref, h_ref, o_ref,
                    *, tm, seq):
    pos0 = (pl.program_id(0) * tm) % seq
    u = u_ref[...]
    halo = jnp.where(pos0 == 0, 0.0, halo_ref[...])
    x = jnp.concatenate([halo, u], axis=0)
    s2 = x + pltpu.roll(x, 1, 0)
    s4 = s2 + pltpu.roll(s2, 2, 0)
    s8 = s4 + pltpu.roll(s4, 4, 0)
    s16 = s8 + pltpu.roll(s8, 8, 0)
    lane = lax.broadcasted_iota(jnp.int32, (1, POOL_DIM), 1)
    grp = POOL_DIM // len(POOL_WINDOWS)
    wsum = jnp.where(lane < grp, s2, jnp.where(lane < 2 * grp, s4, jnp.where(lane < 3 * grp, s8, s16)))
    win = jnp.where(lane < grp, 2, jnp.where(lane < 2 * grp, 4, jnp.where(lane < 3 * grp, 8, 16)))
    pos = pos0 + lax.broadcasted_iota(jnp.int32, (tm, 1), 0)
    count = jnp.minimum(pos + 1, win).astype(F32)
    r = wsum[POOL_HALO:, :] / count - u
    pool = _dot(r.astype(BF16), pw_ref[...]) * ps_ref[...]
    gd = GROUP_DIM
    a = (_dot(sb_ref[...], wo_ref[0:gd, :]) + _dot(fox_ref[...], wo_ref[gd:2 * gd, :])
         + _dot(pool.astype(BF16), wo_ref[2 * gd:, :]))
    o_ref[...] = h_ref[...] + _rms(a, g_ref[...])


def _mix_out(sb, fox, u, pool_w_bd, pool_scale, w_out, g, h, seq, tm=512):
    n, d = h.shape
    hb = tm // POOL_HALO
    return pl.pallas_call(
        functools.partial(_mix_out_kernel, tm=tm, seq=seq),
        grid=(n // tm,),
        in_specs=[pl.BlockSpec((tm, GROUP_DIM), lambda i: (i, 0)),
                  pl.BlockSpec((tm, GROUP_DIM), lambda i: (i, 0)),
                  pl.BlockSpec((tm, POOL_DIM), lambda i: (i, 0)),
                  pl.BlockSpec((POOL_HALO, POOL_DIM), lambda i: (jnp.maximum(i * hb - 1, 0), 0)),
                  pl.BlockSpec((POOL_DIM, POOL_DIM), lambda i: (0, 0)),
                  pl.BlockSpec((1, POOL_DIM), lambda i: (0, 0)),
                  pl.BlockSpec(w_out.shape, lambda i: (0, 0)),
                  pl.BlockSpec((1, d), lambda i: (0, 0)),
                  pl.BlockSpec((tm, d), lambda i: (i, 0))],
        out_specs=pl.BlockSpec((tm, d), lambda i: (i, 0)),
        out_shape=jax.ShapeDtypeStruct((n, d), F32),
        compiler_params=_params(1),
        name="mix_out",
    )(sb, fox, u, u, pool_w_bd, pool_scale, w_out, g, h)


def _xattn_kernel(h_ref, gpre_ref, wq_ref, k_ref, v_ref, wo_ref, gpost_ref, o_ref):
    h = h_ref[...]
    d = h.shape[-1]
    hd = d // XA_HEADS
    hn = _rms(h, gpre_ref[...]).astype(BF16)
    q = (_dot(hn, wq_ref[...]) * (hd ** -0.5)).astype(BF16)
    outs = []
    for a in range(XA_HEADS):
        sl = slice(a * hd, (a + 1) * hd)
        s = lax.dot_general(q[:, sl], k_ref[:, sl], _NT, preferred_element_type=F32)
        p = jnp.exp(s - jnp.max(s, axis=1, keepdims=True))
        p = p / jnp.sum(p, axis=1, keepdims=True)
        outs.append(_dot(p.astype(BF16), v_ref[:, sl]).astype(BF16))
    c = _dot(jnp.concatenate(outs, axis=1), wo_ref[...])
    o_ref[...] = h + _rms(c, gpost_ref[...])


def _xattn(h, g_pre, wq, k_mem, v_mem, wo, g_post, seq, tm=512):
    n, d = h.shape
    m_len = k_mem.shape[1]
    per_seq = seq // tm
    return pl.pallas_call(
        _xattn_kernel,
        grid=(n // tm,),
        in_specs=[pl.BlockSpec((tm, d), lambda i: (i, 0)),
                  pl.BlockSpec((1, d), lambda i: (0, 0)),
                  pl.BlockSpec((d, d), lambda i: (0, 0)),
                  pl.BlockSpec((None, m_len, d), lambda i: (i // per_seq, 0, 0)),
                  pl.BlockSpec((None, m_len, d), lambda i: (i // per_seq, 0, 0)),
                  pl.BlockSpec((d, d), lambda i: (0, 0)),
                  pl.BlockSpec((1, d), lambda i: (0, 0))],
        out_specs=pl.BlockSpec((tm, d), lambda i: (i, 0)),
        out_shape=jax.ShapeDtypeStruct((n, d), F32),
        compiler_params=_params(1),
        name="xattn",
    )(h, g_pre, wq, k_mem, v_mem, wo, g_post)


def _silu(x):
    return x / (1.0 + jnp.exp(-x))


def _ffn_kernel(h_ref, gpre_ref, wg_ref, wu_ref, wd_ref, gpost_ref, o_ref, hn_ref, acc_ref):
    j = pl.program_id(1)

    @pl.when(j == 0)
    def _():
        hn_ref[...] = _rms(h_ref[...], gpre_ref[...]).astype(BF16)
        acc_ref[...] = jnp.zeros_like(acc_ref)

    hn = hn_ref[...]
    act = _silu(_dot(hn, wg_ref[...])) * _dot(hn, wu_ref[...])
    acc_ref[...] += _dot(act.astype(BF16), wd_ref[...])

    @pl.when(j == pl.num_programs(1) - 1)
    def _():
        o_ref[...] = h_ref[...] + _rms(acc_ref[...], gpost_ref[...])


def _ffn(h, g_pre, wg, wu, wd, g_post, tm=1024, tf=256):
    n, d = h.shape
    ff = wg.shape[1]
    return pl.pallas_call(
        _ffn_kernel,
        grid=(n // tm, ff // tf),
        in_specs=[pl.BlockSpec((tm, d), lambda i, j: (i, 0)),
                  pl.BlockSpec((1, d), lambda i, j: (0, 0)),
                  pl.BlockSpec((d, tf), lambda i, j: (0, j)),
                  pl.BlockSpec((d, tf), lambda i, j: (0, j)),
                  pl.BlockSpec((tf, d), lambda i, j: (j, 0)),
                  pl.BlockSpec((1, d), lambda i, j: (0, 0))],
        out_specs=pl.BlockSpec((tm, d), lambda i, j: (i, 0)),
        out_shape=jax.ShapeDtypeStruct((n, d), F32),
        scratch_shapes=[pltpu.VMEM((tm, d), BF16), pltpu.VMEM((tm, d), F32)],
        compiler_params=_params(2),
        name="ffn_dense",
    )(h, g_pre, wg, wu, wd, g_post)


def _router_kernel(h_ref, g_ref, wr_ref, hn_ref, idx_ref, gate_ref):
    hn = _rms(h_ref[...], g_ref[...])
    hn_ref[...] = hn
    x_hi, x_mid, _ = _split3(hn)
    w_hi, w_mid = wr_ref[0], wr_ref[1]
    logits = _dot(x_hi, w_hi) + _dot(x_hi, w_mid) + _dot(x_mid, w_hi)
    lane = lax.broadcasted_iota(jnp.int32, logits.shape, 1)
    logits = jnp.where(lane < N_EXPERTS, logits, -jnp.inf)
    m1 = jnp.max(logits, axis=1, keepdims=True)
    i1 = jnp.min(jnp.where(logits == m1, lane, LANES), axis=1, keepdims=True)
    rest = jnp.where(lane == i1, -jnp.inf, logits)
    m2 = jnp.max(rest, axis=1, keepdims=True)
    i2 = jnp.min(jnp.where(rest == m2, lane, LANES), axis=1, keepdims=True)
    e = jnp.exp(m2 - m1)
    g1 = 1.0 / (1.0 + e)
    idx_ref[...] = jnp.where(lane == 0, i1, jnp.where(lane == 1, i2, 0))
    gate_ref[...] = jnp.where(lane == 0, g1, jnp.where(lane == 1, e * g1, 0.0))


def _router(h, g, wr_split, tm=512):
    n, d = h.shape
    return pl.pallas_call(
        _router_kernel,
        grid=(n // tm,),
        in_specs=[pl.BlockSpec((tm, d), lambda i: (i, 0)),
                  pl.BlockSpec((1, d), lambda i: (0, 0)),
                  pl.BlockSpec(wr_split.shape, lambda i: (0, 0, 0))],
        out_specs=[pl.BlockSpec((tm, d), lambda i: (i, 0)),
                   pl.BlockSpec((tm, LANES), lambda i: (i, 0)),
                   pl.BlockSpec((tm, LANES), lambda i: (i, 0))],
        out_shape=[jax.ShapeDtypeStruct((n, d), F32),
                   jax.ShapeDtypeStruct((n, LANES), jnp.int32),
                   jax.ShapeDtypeStruct((n, LANES), F32)],
        compiler_params=_params(1),
        name="router",
    )(h, g, wr_split)


def _expert_kernel(blk_e_ref, n_valid_ref, row_tok_ref, row_dst_ref,
                   hn_hbm, wg_ref, wu_ref, wd_ref, y_hbm, xs_ref, ys_ref, sem_in, sem_out, *, tf):
    i = pl.program_id(0)
    n_valid = n_valid_ref[i]
    base = i * MOE_BLOCK

    @pl.when(i == 0)
    def _():
        xs_ref[...] = jnp.zeros_like(xs_ref)

    def gather(r):
        tok = row_tok_ref[base + r]
        return pltpu.make_async_copy(hn_hbm.at[pl.ds(tok, 1)], xs_ref.at[pl.ds(r, 1)], sem_in)

    def scatter(r):
        dst = row_dst_ref[base + r]
        return pltpu.make_async_copy(ys_ref.at[pl.ds(r, 1)], y_hbm.at[pl.ds(dst, 1)], sem_out)

    def start(make):
        def body(r, _):
            make(r).start()
            return 0
        return body

    def wait(make):
        def body(r, _):
            make(r).wait()
            return 0
        return body

    @pl.when(n_valid > 0)
    def _():
        lax.fori_loop(0, n_valid, start(gather), 0)
        lax.fori_loop(0, n_valid, wait(gather), 0)
        x = xs_ref[...].astype(BF16)
        ff = wg_ref.shape[1]
        y = None
        for c in range(0, ff, tf):
            act = _silu(_dot(x, wg_ref[:, c:c + tf])) * _dot(x, wu_ref[:, c:c + tf])
            part = _dot(act.astype(BF16), wd_ref[c:c + tf, :])
            y = part if y is None else y + part
        ys_ref[...] = y
        lax.fori_loop(0, n_valid, start(scatter), 0)
        lax.fori_loop(0, n_valid, wait(scatter), 0)


def _experts(blk_e, n_valid, row_tok, row_dst, hn, wg, wu, wd, tf=256):
    n, d = hn.shape
    n_blocks = blk_e.shape[0]
    ff = wg.shape[2]
    grid_spec = pltpu.PrefetchScalarGridSpec(
        num_scalar_prefetch=4,
        grid=(n_blocks,),
        in_specs=[pl.BlockSpec(memory_space=pl.ANY),
                  pl.BlockSpec((None, d, ff), lambda i, be, nv, rt, rd: (be[i], 0, 0)),
                  pl.BlockSpec((None, d, ff), lambda i, be, nv, rt, rd: (be[i], 0, 0)),
                  pl.BlockSpec((None, ff, d), lambda i, be, nv, rt, rd: (be[i], 0, 0))],
        out_specs=pl.BlockSpec(memory_space=pl.ANY),
        scratch_shapes=[pltpu.VMEM((MOE_BLOCK, d), F32), pltpu.VMEM((MOE_BLOCK, d), F32),
                        pltpu.SemaphoreType.DMA, pltpu.SemaphoreType.DMA],
    )
    return pl.pallas_call(
        functools.partial(_expert_kernel, tf=tf),
        grid_spec=grid_spec,
        out_shape=jax.ShapeDtypeStruct((TOP_K * n, d), F32),
        compiler_params=_params(1),
        name="moe_experts",
    )(blk_e, n_valid, row_tok, row_dst, hn, wg, wu, wd)


def _combine_kernel(y0_ref, y1_ref, gate_ref, g_ref, h_ref, o_ref):
    gates = gate_ref[...]
    f = y0_ref[...] * gates[:, 0:1] + y1_ref[...] * gates[:, 1:2]
    o_ref[...] = h_ref[...] + _rms(f, g_ref[...])


def _combine(y, gates, g, h, tm=512):
    n, d = h.shape
    nb = n // tm
    return pl.pallas_call(
        _combine_kernel,
        grid=(nb,),
        in_specs=[pl.BlockSpec((tm, d), lambda i: (i, 0)),
                  pl.BlockSpec((tm, d), lambda i: (nb + i, 0)),
                  pl.BlockSpec((tm, LANES), lambda i: (i, 0)),
                  pl.BlockSpec((1, d), lambda i: (0, 0)),
                  pl.BlockSpec((tm, d), lambda i: (i, 0))],
        out_specs=pl.BlockSpec((tm, d), lambda i: (i, 0)),
        out_shape=jax.ShapeDtypeStruct((n, d), F32),
        compiler_params=_params(1),
        name="moe_combine",
    )(y, y, gates, g, h)


def _route_plan(top_idx, n_tok):
    n_pairs = n_tok * TOP_K
    e_flat = top_idx.reshape(-1)
    onehot = (e_flat[:, None] == jnp.arange(N_EXPERTS)[None, :]).astype(jnp.int32)
    csum = jnp.cumsum(onehot, axis=0)
    counts = csum[-1]
    rank = jnp.take_along_axis(csum, e_flat[:, None], axis=1)[:, 0] - 1
    padded = ((counts + MOE_BLOCK - 1) // MOE_BLOCK) * MOE_BLOCK
    pend = jnp.cumsum(padded)
    pstart = pend - padded
    dest = pstart[e_flat] + rank
    n_blocks = -(-n_pairs // MOE_BLOCK) + N_EXPERTS
    n_rows = n_blocks * MOE_BLOCK
    pair = jnp.arange(n_pairs, dtype=jnp.int32)
    row_tok = jnp.zeros((n_rows,), jnp.int32).at[dest].set(pair // TOP_K)
    row_dst = jnp.zeros((n_rows,), jnp.int32).at[dest].set((pair % TOP_K) * n_tok + pair // TOP_K)
    blk_start = jnp.arange(n_blocks, dtype=jnp.int32) * MOE_BLOCK
    blk_e = jnp.clip(jnp.searchsorted(pend, blk_start, side='right'), 0, N_EXPERTS - 1).astype(jnp.int32)
    n_valid = jnp.clip(pstart[blk_e] + counts[blk_e] - blk_start, 0, MOE_BLOCK).astype(jnp.int32)
    return blk_e, n_valid, row_tok, row_dst


def _moe(h, g_pre, router_w, wg, wu, wd, g_post):
    n, d = h.shape
    wr = jnp.pad(router_w, ((0, 0), (0, LANES - N_EXPERTS)))
    wr_hi = wr.astype(BF16)
    wr_mid = (wr - wr_hi.astype(F32)).astype(BF16)
    hn, idx, gates = _router(h, g_pre, jnp.stack([wr_hi, wr_mid]))
    blk_e, n_valid, row_tok, row_dst = _route_plan(idx[:, :TOP_K], n)
    y = _experts(blk_e, n_valid, row_tok, row_dst, hn, wg, wu, wd)
    return _combine(y, gates, g_post, h)


def _pair_cols(w, start):
    return w[:, start:start + GROUP_DIM]


def kernel(x, mem, mix_norm_pre, mix_norm_post, w_in, b_forget, pool_w, pool_scale, w_out,
           xa_norm_pre, xa_norm_post, mem_norm, xa_wq, xa_wkv, xa_wo,
           ffn_norm_pre, ffn_norm_post, dense_w_gate, dense_w_up, dense_w_down,
           router_w, moe_w_gate, moe_w_up, moe_w_down):
    bsz, seq, d = x.shape
    m_len = mem.shape[1]
    depth = w_in.shape[0]
    n = bsz * seq
    h = x.reshape(n, d)
    mem2 = mem.reshape(bsz * m_len, d)
    row = lambda v: v.reshape(1, -1)

    idx = jnp.arange(ATT_BLOCK)
    tri = (idx[:, None] >= idx[None, :]).astype(BF16)

    for li in range(depth):
        w = w_in[li]
        flog_w = jnp.pad(w[:, QKV_DIM:QKV_DIM + N_FOX], ((0, 0), (0, LANES - N_FOX)))
        w_cat = jnp.concatenate([w[:, :QKV_DIM], w[:, QKV_DIM + N_FOX:], flog_w], axis=1).astype(BF16)
        qkv, u, flog = _norm_matmul(
            h, row(mix_norm_pre[li]), w_cat,
            splits=[(0, QKV_DIM), (QKV_DIM, POOL_DIM), (QKV_DIM + POOL_DIM, LANES)],
            dtypes=[BF16, F32, F32], tm=512)
        qkv = qkv.reshape(bsz, seq, QKV_DIM)

        b_pad = jnp.pad(b_forget[li], (0, LANES - N_FOX)).reshape(1, LANES)
        c = _logf_cumsum(flog.reshape(bsz, seq, LANES), b_pad)
        c_rows = c[:, :, :N_FOX].transpose(0, 2, 1).reshape(bsz, N_PAIRS, 2, seq)

        sb = _sb_attention(qkv, tri).reshape(n, GROUP_DIM)
        fox = _fox_attention(qkv, c_rows).reshape(n, GROUP_DIM)

        pool_bd = jax.scipy.linalg.block_diag(*[pool_w[li, gi] for gi in range(len(POOL_WINDOWS))])
        h = _mix_out(sb, fox, u, pool_bd.astype(BF16), row(pool_scale[li]), w_out[li].astype(BF16),
                     row(mix_norm_post[li]), h, seq)

        k_mem, v_mem = _norm_matmul(mem2, row(mem_norm[li]), xa_wkv[li].astype(BF16),
                                    splits=[(0, d), (d, d)], dtypes=[BF16, BF16], tm=m_len)
        h = _xattn(h, row(xa_norm_pre[li]), xa_wq[li].astype(BF16),
                   k_mem.reshape(bsz, m_len, d), v_mem.reshape(bsz, m_len, d),
                   xa_wo[li].astype(BF16), row(xa_norm_post[li]), seq)

        j = li // 2
        if li % 2 == 0:
            h = _ffn(h, row(ffn_norm_pre[li]), dense_w_gate[j].astype(BF16), dense_w_up[j].astype(BF16),
                     dense_w_down[j].astype(BF16), row(ffn_norm_post[li]))
        else:
            h = _moe(h, row(ffn_norm_pre[li]), router_w[j], moe_w_gate[j].astype(BF16),
                     moe_w_up[j].astype(BF16), moe_w_down[j].astype(BF16), row(ffn_norm_post[li]))
    return h.reshape(bsz, seq, d)
```

```python
import functools

import jax
import jax.numpy as jnp
from jax import lax
from jax.experimental import pallas as pl
from jax.experimental.pallas import tpu as pltpu

F32 = jnp.float32
BF16 = jnp.bfloat16
EPS = 1e-6

HEAD_DIM = 64
LANES = 128
N_PAIRS = 3
GROUP_DIM = N_PAIRS * LANES
QKV_DIM = 6 * GROUP_DIM
POOL_DIM = 256
POOL_WINDOWS = (2, 4, 8, 16)
POOL_HALO = 16
N_FOX = 6
XA_HEADS = 4
N_EXPERTS = 8
TOP_K = 2
MOE_BLOCK = 256
ATT_BLOCK = 256
FOX_BLOCK = 512
EXP_CUTOFF = 105.0
VMEM_LIMIT = 56 * 1024 * 1024

_NT = (((1,), (1,)), ((), ()))


def _params(n_axes):
    return pltpu.CompilerParams(dimension_semantics=("arbitrary",) * n_axes,
                                vmem_limit_bytes=VMEM_LIMIT)


def _rms(x, g):
    return x * lax.rsqrt(jnp.mean(x * x, axis=-1, keepdims=True) + EPS) * g


def _dot(a, b):
    return jnp.dot(a, b, preferred_element_type=F32)


def _split3(x):
    hi = x.astype(BF16)
    r1 = x - hi.astype(F32)
    mid = r1.astype(BF16)
    lo = (r1 - mid.astype(F32)).astype(BF16)
    return hi, mid, lo


def _norm_matmul_kernel(x_ref, g_ref, w_ref, *out_refs, splits):
    yb = _rms(x_ref[...], g_ref[...]).astype(BF16)
    for o_ref, (c0, width) in zip(out_refs, splits):
        for c in range(0, width, 256):
            cw = min(256, width - c)
            o_ref[:, c:c + cw] = _dot(yb, w_ref[:, c0 + c:c0 + c + cw]).astype(o_ref.dtype)


def _norm_matmul(x, g, w, splits, dtypes, tm):
    n, d = x.shape
    kern = functools.partial(_norm_matmul_kernel, splits=tuple(splits))
    return pl.pallas_call(
        kern,
        grid=(n // tm,),
        in_specs=[pl.BlockSpec((tm, d), lambda i: (i, 0)),
                  pl.BlockSpec((1, d), lambda i: (0, 0)),
                  pl.BlockSpec(w.shape, lambda i: (0, 0))],
        out_specs=[pl.BlockSpec((tm, wd), lambda i: (i, 0)) for (_, wd) in splits],
        out_shape=[jax.ShapeDtypeStruct((n, wd), dt) for (_, wd), dt in zip(splits, dtypes)],
        compiler_params=_params(1),
        name="norm_matmul",
    )(x, g, w)


def _logf_cumsum_kernel(fl_ref, b_ref, c_ref, carry_ref, *, tc):
    @pl.when(pl.program_id(1) == 0)
    def _():
        carry_ref[...] = jnp.zeros_like(carry_ref)

    x = fl_ref[...] + b_ref[...]
    ls = jnp.minimum(x, 0.0) - jnp.log(1.0 + jnp.exp(-jnp.abs(x)))
    row = lax.broadcasted_iota(jnp.int32, (tc, tc), 0)
    col = lax.broadcasted_iota(jnp.int32, (tc, tc), 1)
    tri = jnp.where(row >= col, 1.0, 0.0).astype(BF16)
    hi, mid, lo = _split3(ls)
    c = _dot(tri, hi) + _dot(tri, mid) + _dot(tri, lo) + carry_ref[...]
    c_ref[...] = c
    carry_ref[...] = c[tc - 1:tc, :]


def _logf_cumsum(fl, b_pad, tc=256):
    bsz, s, _ = fl.shape
    return pl.pallas_call(
        functools.partial(_logf_cumsum_kernel, tc=tc),
        grid=(bsz, s // tc),
        in_specs=[pl.BlockSpec((None, tc, LANES), lambda b, j: (b, j, 0)),
                  pl.BlockSpec((1, LANES), lambda b, j: (0, 0))],
        out_specs=pl.BlockSpec((None, tc, LANES), lambda b, j: (b, j, 0)),
        out_shape=jax.ShapeDtypeStruct(fl.shape, F32),
        scratch_shapes=[pltpu.VMEM((1, LANES), F32)],
        compiler_params=_params(2),
        name="logf_cumsum",
    )(fl, b_pad)


def _head_masks(x):
    lane = lax.broadcasted_iota(jnp.int32, (1, LANES), 1)
    zero = jnp.zeros_like(x)
    return jnp.where(lane < HEAD_DIM, x, zero), jnp.where(lane >= HEAD_DIM, x, zero)


def _sb_kernel(q_ref, k_ref, v_ref, tri_ref, o_ref, acc_ref, *, blk):
    i = pl.program_id(2)
    q_heads = _head_masks(q_ref[...] * jnp.asarray(HEAD_DIM ** -0.5, BF16))
    row = lax.broadcasted_iota(jnp.int32, (blk, blk), 0)
    col = lax.broadcasted_iota(jnp.int32, (blk, blk), 1)
    acc_ref[...] = jnp.zeros_like(acc_ref)

    def tile(j, run, diag):
        start = pl.multiple_of(j * blk, blk)
        k = k_ref[pl.ds(start, blk), :]
        v_heads = _head_masks(v_ref[pl.ds(start, blk), :])
        new_run = []
        upd = None
        for hd in range(2):
            z = lax.dot_general(q_heads[hd], k, _NT, preferred_element_type=F32)
            sp = jnp.maximum(z, 0.0) + jnp.log(1.0 + jnp.exp(-jnp.abs(z)))
            if diag:
                sp = jnp.where(col < row, sp, 0.0)
            suffix = _dot(sp.astype(BF16), tri_ref[...])
            w = jnp.exp(z - suffix - run[hd])
            if diag:
                w = jnp.where(col < row, w, 0.0)
            part = _dot(w.astype(BF16), v_heads[hd])
            upd = part if upd is None else upd + part
            new_run.append(run[hd] + suffix[:, 0:1])
        acc_ref[...] += upd
        return tuple(new_run)

    def alive(run):
        return (jnp.minimum(jnp.min(run[0]), jnp.min(run[1])) < EXP_CUTOFF).astype(jnp.int32)

    def cond(carry):
        jj, go, _ = carry
        return jnp.logical_and(jj < i, go > 0)

    def body(carry):
        jj, _, run = carry
        run = tile(i - 1 - jj, run, False)
        return jj + 1, alive(run), run

    zero = jnp.zeros((blk, 1), F32)
    run = tile(i, (zero, zero), True)
    lax.while_loop(cond, body, (jnp.int32(0), alive(run), run))
    o_ref[...] = acc_ref[...].astype(o_ref.dtype)


def _sb_attention(qkv, tri, blk=ATT_BLOCK):
    bsz, s, _ = qkv.shape
    return pl.pallas_call(
        functools.partial(_sb_kernel, blk=blk),
        grid=(bsz, N_PAIRS, s // blk),
        in_specs=[pl.BlockSpec((None, blk, LANES), lambda b, p, i: (b, i, p)),
                  pl.BlockSpec((None, s, LANES), lambda b, p, i: (b, 0, N_PAIRS + p)),
                  pl.BlockSpec((None, s, LANES), lambda b, p, i: (b, 0, 2 * N_PAIRS + p)),
                  pl.BlockSpec((blk, blk), lambda b, p, i: (0, 0))],
        out_specs=pl.BlockSpec((None, blk, LANES), lambda b, p, i: (b, i, p)),
        out_shape=jax.ShapeDtypeStruct((bsz, s, GROUP_DIM), BF16),
        scratch_shapes=[pltpu.VMEM((blk, LANES), F32)],
        compiler_params=_params(3),
        name="sb_attention",
    )(qkv, qkv, qkv, tri)


def _fox_kernel(q_ref, k_ref, v_ref, c_ref, o_ref, acc_ref, m_ref, l_ref, kmax_ref, *, blk, seq):
    i = pl.program_id(2)
    lane = lax.broadcasted_iota(jnp.int32, (1, LANES), 1)
    in_head = (lane < HEAD_DIM, lane >= HEAD_DIM)

    @pl.when(i == 0)
    def _():
        def chunk(t, mx):
            k = k_ref[pl.ds(pl.multiple_of(t * blk, blk), blk), :].astype(F32)
            sq = k * k
            return tuple(jnp.maximum(mx[hd], jnp.max(jnp.sum(jnp.where(in_head[hd], sq, 0.0), axis=1,
                                                              keepdims=True))) for hd in range(2))
        mx = lax.fori_loop(0, seq // blk, chunk, (jnp.float32(0.0), jnp.float32(0.0)))
        kmax_ref[0] = mx[0]
        kmax_ref[1] = mx[1]

    q_heads = _head_masks(q_ref[...] * jnp.asarray(HEAD_DIM ** -0.5, BF16))
    row = lax.broadcasted_iota(jnp.int32, (blk, blk), 0)
    col = lax.broadcasted_iota(jnp.int32, (blk, blk), 1)
    q0 = pl.multiple_of(i * blk, blk)
    acc_ref[...] = jnp.zeros_like(acc_ref)

    for hd in range(2):
        qh = q_heads[hd]
        q32 = qh.astype(F32)
        z_bound = jnp.sqrt(jnp.sum(q32 * q32, axis=1, keepdims=True) * kmax_ref[hd]) * 1.001
        c_base = c_ref[hd:hd + 1, pl.ds(q0, LANES)][:, 0:1]
        m_ref[hd] = jnp.full((blk, 1), -1e30, F32)
        l_ref[hd] = jnp.zeros((blk, 1), F32)

        def tile(start, diag, hd=hd, qh=qh, c_base=c_base, z_bound=z_bound):
            k = k_ref[pl.ds(start, blk), :]
            v = jnp.where(in_head[hd], v_ref[pl.ds(start, blk), :], jnp.zeros((), BF16))
            z = lax.dot_general(qh, k, _NT, preferred_element_type=F32)
            s = z - (c_ref[hd:hd + 1, pl.ds(start, blk)] - c_base)
            if diag:
                s = jnp.where(col <= row, s, -1e30)
            m_old = m_ref[hd]
            m_new = jnp.maximum(m_old, jnp.max(s, axis=1, keepdims=True))
            alpha = jnp.exp(m_old - m_new)
            p = jnp.exp(s - m_new)
            l_ref[hd] = alpha * l_ref[hd] + jnp.sum(p, axis=1, keepdims=True)
            m_ref[hd] = m_new
            acc_ref[...] = acc_ref[...] * jnp.where(in_head[hd], alpha, 1.0) + _dot(p.astype(BF16), v)
            prev = pl.multiple_of(jnp.maximum(start - LANES, 0), LANES)
            c_last = c_ref[hd:hd + 1, pl.ds(prev, LANES)][:, LANES - 1:LANES]
            return (jnp.max(z_bound + (c_base - c_last) - m_new) > -EXP_CUTOFF).astype(jnp.int32)

        def cond(carry):
            jj, go = carry
            return jnp.logical_and(jj < i, go > 0)

        def body(carry, tile=tile):
            jj, _ = carry
            return jj + 1, tile(pl.multiple_of((i - 1 - jj) * blk, blk), False)

        lax.while_loop(cond, body, (jnp.int32(0), tile(q0, True)))

    inv = jnp.where(in_head[0], 1.0 / l_ref[0], 1.0 / l_ref[1])
    o_ref[...] = (acc_ref[...] * inv).astype(o_ref.dtype)


def _fox_attention(qkv, c_rows, blk=FOX_BLOCK):
    bsz, s, _ = qkv.shape
    return pl.pallas_call(
        functools.partial(_fox_kernel, blk=blk, seq=s),
        grid=(bsz, N_PAIRS, s // blk),
        in_specs=[pl.BlockSpec((None, blk, LANES), lambda b, p, i: (b, i, 3 * N_PAIRS + p)),
                  pl.BlockSpec((None, s, LANES), lambda b, p, i: (b, 0, 4 * N_PAIRS + p)),
                  pl.BlockSpec((None, s, LANES), lambda b, p, i: (b, 0, 5 * N_PAIRS + p)),
                  pl.BlockSpec((None, None, 2, s), lambda b, p, i: (b, p, 0, 0))],
        out_specs=pl.BlockSpec((None, blk, LANES), lambda b, p, i: (b, i, p)),
        out_shape=jax.ShapeDtypeStruct((bsz, s, GROUP_DIM), BF16),
        scratch_shapes=[pltpu.VMEM((blk, LANES), F32), pltpu.VMEM((2, blk, 1), F32),
                        pltpu.VMEM((2, blk, 1), F32), pltpu.SMEM((2,), F32)],
        compiler_params=_params(3),
        name="fox_attention",
    )(qkv, qkv, qkv, c_rows)


def _mix_out_kernel(sb_ref, fox_ref, u_ref, halo_ref, pw_ref, ps_ref, wo_ref, g_ref, h_ref, o_ref,
                    *, tm, seq):
    pos0 = (pl.program_id(0) * tm) % seq
    u = u_ref[...]
    halo = jnp.where(pos0 == 0, 0.0, halo_ref[...])
    x = jnp.concatenate([halo, u], axis=0)
    s2 = x + pltpu.roll(x, 1, 0)
    s4 = s2 + pltpu.roll(s2, 2, 0)
    s8 = s4 + pltpu.roll(s4, 4, 0)
    s16 = s8 + pltpu.roll(s8, 8, 0)
    lane = lax.broadcasted_iota(jnp.int32, (1, POOL_DIM), 1)
    grp = POOL_DIM // len(POOL_WINDOWS)
    wsum = jnp.where(lane < grp, s2, jnp.where(lane < 2 * grp, s4, jnp.where(lane < 3 * grp, s8, s16)))
    win = jnp.where(lane < grp, 2, jnp.where(lane < 2 * grp, 4, jnp.where(lane < 3 * grp, 8, 16)))
    pos = pos0 + lax.broadcasted_iota(jnp.int32, (tm, 1), 0)
    count = jnp.minimum(pos + 1, win).astype(F32)
    r = wsum[POOL_HALO:, :] / count - u
    pool = _dot(r.astype(BF16), pw_ref[...]) * ps_ref[...]
    gd = GROUP_DIM
    a = (_dot(sb_ref[...], wo_ref[0:gd, :]) + _dot(fox_ref[...], wo_ref[gd:2 * gd, :])
         + _dot(pool.astype(BF16), wo_ref[2 * gd:, :]))
    o_ref[...] = h_ref[...] + _rms(a, g_ref[...])


def _mix_out(sb, fox, u, pool_w_bd, pool_scale, w_out, g, h, seq, tm=512):
    n, d = h.shape
    hb = tm // POOL_HALO
    return pl.pallas_call(
        functools.partial(_mix_out_kernel, tm=tm, seq=seq),
        grid=(n // tm,),
        in_specs=[pl.BlockSpec((tm, GROUP_DIM), lambda i: (i, 0)),
                  pl.BlockSpec((tm, GROUP_DIM), lambda i: (i, 0)),
                  pl.BlockSpec((tm, POOL_DIM), lambda i: (i, 0)),
                  pl.BlockSpec((POOL_HALO, POOL_DIM), lambda i: (jnp.maximum(i * hb - 1, 0), 0)),
                  pl.BlockSpec((POOL_DIM, POOL_DIM), lambda i: (0, 0)),
                  pl.BlockSpec((1, POOL_DIM), lambda i: (0, 0)),
                  pl.BlockSpec(w_out.shape, lambda i: (0, 0)),
                  pl.BlockSpec((1, d), lambda i: (0, 0)),
                  pl.BlockSpec((tm, d), lambda i: (i, 0))],
        out_specs=pl.BlockSpec((tm, d), lambda i: (i, 0)),
        out_shape=jax.ShapeDtypeStruct((n, d), F32),
        compiler_params=_params(1),
        name="mix_out",
    )(sb, fox, u, u, pool_w_bd, pool_scale, w_out, g, h)


def _xattn_kernel(h_ref, gpre_ref, wq_ref, k_ref, v_ref, wo_ref, gpost_ref, o_ref):
    h = h_ref[...]
    d = h.shape[-1]
    hd = d // XA_HEADS
    hn = _rms(h, gpre_ref[...]).astype(BF16)
    q = (_dot(hn, wq_ref[...]) * (hd ** -0.5)).astype(BF16)
    outs = []
    for a in range(XA_HEADS):
        sl = slice(a * hd, (a + 1) * hd)
        s = lax.dot_general(q[:, sl], k_ref[:, sl], _NT, preferred_element_type=F32)
        p = jnp.exp(s - jnp.max(s, axis=1, keepdims=True))
        p = p / jnp.sum(p, axis=1, keepdims=True)
        outs.append(_dot(p.astype(BF16), v_ref[:, sl]).astype(BF16))
    c = _dot(jnp.concatenate(outs, axis=1), wo_ref[...])
    o_ref[...] = h + _rms(c, gpost_ref[...])


def _xattn(h, g_pre, wq, k_mem, v_mem, wo, g_post, seq, tm=512):
    n, d = h.shape
    m_len = k_mem.shape[1]
    per_seq = seq // tm
    return pl.pallas_call(
        _xattn_kernel,
        grid=(n // tm,),
        in_specs=[pl.BlockSpec((tm, d), lambda i: (i, 0)),
                  pl.BlockSpec((1, d), lambda i: (0, 0)),
                  pl.BlockSpec((d, d), lambda i: (0, 0)),
                  pl.BlockSpec((None, m_len, d), lambda i: (i // per_seq, 0, 0)),
                  pl.BlockSpec((None, m_len, d), lambda i: (i // per_seq, 0, 0)),
                  pl.BlockSpec((d, d), lambda i: (0, 0)),
                  pl.BlockSpec((1, d), lambda i: (0, 0))],
        out_specs=pl.BlockSpec((tm, d), lambda i: (i, 0)),
        out_shape=jax.ShapeDtypeStruct((n, d), F32),
        compiler_params=_params(1),
        name="xattn",
    )(h, g_pre, wq, k_mem, v_mem, wo, g_post)


def _silu(x):
    return x / (1.0 + jnp.exp(-x))


def _ffn_kernel(h_ref, gpre_ref, wg_ref, wu_ref, wd_ref, gpost_ref, o_ref, hn_ref, acc_ref):
    j = pl.program_id(1)

    @pl.when(j == 0)
    def _():
        hn_ref[...] = _rms(h_ref[...], gpre_ref[...]).astype(BF16)
        acc_ref[...] = jnp.zeros_like(acc_ref)

    hn = hn_ref[...]
    act = _silu(_dot(hn, wg_ref[...])) * _dot(hn, wu_ref[...])
    acc_ref[...] += _dot(act.astype(BF16), wd_ref[...])

    @pl.when(j == pl.num_programs(1) - 1)
    def _():
        o_ref[...] = h_ref[...] + _rms(acc_ref[...], gpost_ref[...])


def _ffn(h, g_pre, wg, wu, wd, g_post, tm=1024, tf=256):
    n, d = h.shape
    ff = wg.shape[1]
    return pl.pallas_call(
        _ffn_kernel,
        grid=(n // tm, ff // tf),
        in_specs=[pl.BlockSpec((tm, d), lambda i, j: (i, 0)),
                  pl.BlockSpec((1, d), lambda i, j: (0, 0)),
                  pl.BlockSpec((d, tf), lambda i, j: (0, j)),
                  pl.BlockSpec((d, tf), lambda i, j: (0, j)),
                  pl.BlockSpec((tf, d), lambda i, j: (j, 0)),
                  pl.BlockSpec((1, d), lambda i, j: (0, 0))],
        out_specs=pl.BlockSpec((tm, d), lambda i, j: (i, 0)),
        out_shape=jax.ShapeDtypeStruct((n, d), F32),
        scratch_shapes=[pltpu.VMEM((tm, d), BF16), pltpu.VMEM((tm, d), F32)],
        compiler_params=_params(2),
        name="ffn_dense",
    )(h, g_pre, wg, wu, wd, g_post)


def _router_kernel(h_ref, g_ref, wr_ref, hn_ref, idx_ref, gate_ref):
    hn = _rms(h_ref[...], g_ref[...])
    hn_ref[...] = hn
    x_hi, x_mid, _ = _split3(hn)
    w_hi, w_mid = wr_ref[0], wr_ref[1]
    logits = _dot(x_hi, w_hi) + _dot(x_hi, w_mid) + _dot(x_mid, w_hi)
    lane = lax.broadcasted_iota(jnp.int32, logits.shape, 1)
    logits = jnp.where(lane < N_EXPERTS, logits, -jnp.inf)
    m1 = jnp.max(logits, axis=1, keepdims=True)
    i1 = jnp.min(jnp.where(logits == m1, lane, LANES), axis=1, keepdims=True)
    rest = jnp.where(lane == i1, -jnp.inf, logits)
    m2 = jnp.max(rest, axis=1, keepdims=True)
    i2 = jnp.min(jnp.where(rest == m2, lane, LANES), axis=1, keepdims=True)
    e = jnp.exp(m2 - m1)
    g1 = 1.0 / (1.0 + e)
    idx_ref[...] = jnp.where(lane == 0, i1, jnp.where(lane == 1, i2, 0))
    gate_ref[...] = jnp.where(lane == 0, g1, jnp.where(lane == 1, e * g1, 0.0))


def _router(h, g, wr_split, tm=512):
    n, d = h.shape
    return pl.pallas_call(
        _router_kernel,
        grid=(n // tm,),
        in_specs=[pl.BlockSpec((tm, d), lambda i: (i, 0)),
                  pl.BlockSpec((1, d), lambda i: (0, 0)),
                  pl.BlockSpec(wr_split.shape, lambda i: (0, 0, 0))],
        out_specs=[pl.BlockSpec((tm, d), lambda i: (i, 0)),
                   pl.BlockSpec((tm, LANES), lambda i: (i, 0)),
                   pl.BlockSpec((tm, LANES), lambda i: (i, 0))],
        out_shape=[jax.ShapeDtypeStruct((n, d), F32),
                   jax.ShapeDtypeStruct((n, LANES), jnp.int32),
                   jax.ShapeDtypeStruct((n, LANES), F32)],
        compiler_params=_params(1),
        name="router",
    )(h, g, wr_split)


def _expert_kernel(blk_e_ref, n_valid_ref, row_tok_ref, row_dst_ref,
                   hn_hbm, wg_ref, wu_ref, wd_ref, y_hbm, xs_ref, ys_ref, sem_in, sem_out, *, tf):
    i = pl.program_id(0)
    n_valid = n_valid_ref[i]
    base = i * MOE_BLOCK

    @pl.when(i == 0)
    def _():
        xs_ref[...] = jnp.zeros_like(xs_ref)

    def gather(r):
        tok = row_tok_ref[base + r]
        return pltpu.make_async_copy(hn_hbm.at[pl.ds(tok, 1)], xs_ref.at[pl.ds(r, 1)], sem_in)

    def scatter(r):
        dst = row_dst_ref[base + r]
        return pltpu.make_async_copy(ys_ref.at[pl.ds(r, 1)], y_hbm.at[pl.ds(dst, 1)], sem_out)

    def start(make):
        def body(r, _):
            make(r).start()
            return 0
        return body

    def wait(make):
        def body(r, _):
            make(r).wait()
            return 0
        return body

    @pl.when(n_valid > 0)
    def _():
        lax.fori_loop(0, n_valid, start(gather), 0)
        lax.fori_loop(0, n_valid, wait(gather), 0)
        x = xs_ref[...].astype(BF16)
        ff = wg_ref.shape[1]
        y = None
        for c in range(0, ff, tf):
            act = _silu(_dot(x, wg_ref[:, c:c + tf])) * _dot(x, wu_ref[:, c:c + tf])
            part = _dot(act.astype(BF16), wd_ref[c:c + tf, :])
            y = part if y is None else y + part
        ys_ref[...] = y
        lax.fori_loop(0, n_valid, start(scatter), 0)
        lax.fori_loop(0, n_valid, wait(scatter), 0)


def _experts(blk_e, n_valid, row_tok, row_dst, hn, wg, wu, wd, tf=256):
    n, d = hn.shape
    n_blocks = blk_e.shape[0]
    ff = wg.shape[2]
    grid_spec = pltpu.PrefetchScalarGridSpec(
        num_scalar_prefetch=4,
        grid=(n_blocks,),
        in_specs=[pl.BlockSpec(memory_space=pl.ANY),
                  pl.BlockSpec((None, d, ff), lambda i, be, nv, rt, rd: (be[i], 0, 0)),
                  pl.BlockSpec((None, d, ff), lambda i, be, nv, rt, rd: (be[i], 0, 0)),
                  pl.BlockSpec((None, ff, d), lambda i, be, nv, rt, rd: (be[i], 0, 0))],
        out_specs=pl.BlockSpec(memory_space=pl.ANY),
        scratch_shapes=[pltpu.VMEM((MOE_BLOCK, d), F32), pltpu.VMEM((MOE_BLOCK, d), F32),
                        pltpu.SemaphoreType.DMA, pltpu.SemaphoreType.DMA],
    )
    return pl.pallas_call(
        functools.partial(_expert_kernel, tf=tf),
        grid_spec=grid_spec,
        out_shape=jax.ShapeDtypeStruct((TOP_K * n, d), F32),
        compiler_params=_params(1),
        name="moe_experts",
    )(blk_e, n_valid, row_tok, row_dst, hn, wg, wu, wd)


def _combine_kernel(y0_ref, y1_ref, gate_ref, g_ref, h_ref, o_ref):
    gates = gate_ref[...]
    f = y0_ref[...] * gates[:, 0:1] + y1_ref[...] * gates[:, 1:2]
    o_ref[...] = h_ref[...] + _rms(f, g_ref[...])


def _combine(y, gates, g, h, tm=512):
    n, d = h.shape
    nb = n // tm
    return pl.pallas_call(
        _combine_kernel,
        grid=(nb,),
        in_specs=[pl.BlockSpec((tm, d), lambda i: (i, 0)),
                  pl.BlockSpec((tm, d), lambda i: (nb + i, 0)),
                  pl.BlockSpec((tm, LANES), lambda i: (i, 0)),
                  pl.BlockSpec((1, d), lambda i: (0, 0)),
                  pl.BlockSpec((tm, d), lambda i: (i, 0))],
        out_specs=pl.BlockSpec((tm, d), lambda i: (i, 0)),
        out_shape=jax.ShapeDtypeStruct((n, d), F32),
        compiler_params=_params(1),
        name="moe_combine",
    )(y, y, gates, g, h)


def _route_plan(top_idx, n_tok):
    n_pairs = n_tok * TOP_K
    e_flat = top_idx.reshape(-1)
    onehot = (e_flat[:, None] == jnp.arange(N_EXPERTS)[None, :]).astype(jnp.int32)
    csum = jnp.cumsum(onehot, axis=0)
    counts = csum[-1]
    rank = jnp.take_along_axis(csum, e_flat[:, None], axis=1)[:, 0] - 1
    padded = ((counts + MOE_BLOCK - 1) // MOE_BLOCK) * MOE_BLOCK
    pend = jnp.cumsum(padded)
    pstart = pend - padded
    dest = pstart[e_flat] + rank
    n_blocks = -(-n_pairs // MOE_BLOCK) + N_EXPERTS
    n_rows = n_blocks * MOE_BLOCK
    pair = jnp.arange(n_pairs, dtype=jnp.int32)
    row_tok = jnp.zeros((n_rows,), jnp.int32).at[dest].set(pair // TOP_K)
    row_dst = jnp.zeros((n_rows,), jnp.int32).at[dest].set((pair % TOP_K) * n_tok + pair // TOP_K)
    blk_start = jnp.arange(n_blocks, dtype=jnp.int32) * MOE_BLOCK
    blk_e = jnp.clip(jnp.searchsorted(pend, blk_start, side='right'), 0, N_EXPERTS - 1).astype(jnp.int32)
    n_valid = jnp.clip(pstart[blk_e] + counts[blk_e] - blk_start, 0, MOE_BLOCK).astype(jnp.int32)
    return blk_e, n_valid, row_tok, row_dst


def _moe(h, g_pre, router_w, wg, wu, wd, g_post):
    n, d = h.shape
    wr = jnp.pad(router_w, ((0, 0), (0, LANES - N_EXPERTS)))
    wr_hi = wr.astype(BF16)
    wr_mid = (wr - wr_hi.astype(F32)).astype(BF16)
    hn, idx, gates = _router(h, g_pre, jnp.stack([wr_hi, wr_mid]))
    blk_e, n_valid, row_tok, row_dst = _route_plan(idx[:, :TOP_K], n)
    y = _experts(blk_e, n_valid, row_tok, row_dst, hn, wg, wu, wd)
    return _combine(y, gates, g_post, h)


def kernel(x, mem, mix_norm_pre, mix_norm_post, w_in, b_forget, pool_w, pool_scale, w_out,
           xa_norm_pre, xa_norm_post, mem_norm, xa_wq, xa_wkv, xa_wo,
           ffn_norm_pre, ffn_norm_post, dense_w_gate, dense_w_up, dense_w_down,
           router_w, moe_w_gate, moe_w_up, moe_w_down):
    bsz, seq, d = x.shape
    m_len = mem.shape[1]
    depth = w_in.shape[0]
    n = bsz * seq
    h = x.reshape(n, d)
    mem2 = mem.reshape(bsz * m_len, d)
    row = lambda v: v.reshape(1, -1)

    idx = jnp.arange(ATT_BLOCK)
    tri = (idx[:, None] >= idx[None, :]).astype(BF16)

    for li in range(depth):
        w = w_in[li]
        flog_w = jnp.pad(w[:, QKV_DIM:QKV_DIM + N_FOX], ((0, 0), (0, LANES - N_FOX)))
        w_cat = jnp.concatenate([w[:, :QKV_DIM], w[:, QKV_DIM + N_FOX:], flog_w], axis=1).astype(BF16)
        qkv, u, flog = _norm_matmul(
            h, row(mix_norm_pre[li]), w_cat,
            splits=[(0, QKV_DIM), (QKV_DIM, POOL_DIM), (QKV_DIM + POOL_DIM, LANES)],
            dtypes=[BF16, F32, F32], tm=512)
        qkv = qkv.reshape(bsz, seq, QKV_DIM)

        b_pad = jnp.pad(b_forget[li], (0, LANES - N_FOX)).reshape(1, LANES)
        c = _logf_cumsum(flog.reshape(bsz, seq, LANES), b_pad)
        c_rows = c[:, :, :N_FOX].transpose(0, 2, 1).reshape(bsz, N_PAIRS, 2, seq)

        sb = _sb_attention(qkv, tri).reshape(n, GROUP_DIM)
        fox = _fox_attention(qkv, c_rows).reshape(n, GROUP_DIM)

        pool_bd = jax.scipy.linalg.block_diag(*[pool_w[li, gi] for gi in range(len(POOL_WINDOWS))])
        h = _mix_out(sb, fox, u, pool_bd.astype(BF16), row(pool_scale[li]), w_out[li].astype(BF16),
                     row(mix_norm_post[li]), h, seq)

        k_mem, v_mem = _norm_matmul(mem2, row(mem_norm[li]), xa_wkv[li].astype(BF16),
                                    splits=[(0, d), (d, d)], dtypes=[BF16, BF16], tm=m_len)
        h = _xattn(h, row(xa_norm_pre[li]), xa_wq[li].astype(BF16),
                   k_mem.reshape(bsz, m_len, d), v_mem.reshape(bsz, m_len, d),
                   xa_wo[li].astype(BF16), row(xa_norm_post[li]), seq)

        j = li // 2
        if li % 2 == 0:
            h = _ffn(h, row(ffn_norm_pre[li]), dense_w_gate[j].astype(BF16), dense_w_up[j].astype(BF16),
                     dense_w_down[j].astype(BF16), row(ffn_norm_post[li]))
        else:
            h = _moe(h, row(ffn_norm_pre[li]), router_w[j], moe_w_gate[j].astype(BF16),
                     moe_w_up[j].astype(BF16), moe_w_down[j].astype(BF16), row(ffn_norm_post[li]))
    return h.reshape(bsz, seq, d)
```

```python
import functools

import jax
import jax.numpy as jnp
from jax import lax
from jax.experimental import pallas as pl
from jax.experimental.pallas import tpu as pltpu

F32 = jnp.float32
BF16 = jnp.bfloat16
EPS = 1e-6

HEAD_DIM = 64
LANES = 128
SUBLANES = 8
N_PAIRS = 3
GROUP_DIM = N_PAIRS * LANES
QKV_DIM = 6 * GROUP_DIM
POOL_DIM = 256
POOL_WINDOWS = (2, 4, 8, 16)
POOL_HALO = 16
N_FOX = 6
XA_HEADS = 4
N_EXPERTS = 8
TOP_K = 2
MOE_ROWS = 512
FF_CHUNK = 256
ATT_BLOCK = 256
FOX_BLOCK = 512
EXP_CUTOFF = 105.0
VMEM_LIMIT = 56 * 1024 * 1024

_NT = (((1,), (1,)), ((), ()))


def _params(n_axes):
    return pltpu.CompilerParams(dimension_semantics=("arbitrary",) * n_axes,
                                vmem_limit_bytes=VMEM_LIMIT)


def _rms(x, g):
    return x * lax.rsqrt(jnp.mean(x * x, axis=-1, keepdims=True) + EPS) * g


def _dot(a, b):
    return jnp.dot(a, b, preferred_element_type=F32)


def _split3(x):
    hi = x.astype(BF16)
    r1 = x - hi.astype(F32)
    mid = r1.astype(BF16)
    lo = (r1 - mid.astype(F32)).astype(BF16)
    return hi, mid, lo


def _norm_matmul_kernel(x_ref, g_ref, w_ref, *out_refs, splits):
    yb = _rms(x_ref[...], g_ref[...]).astype(BF16)
    for o_ref, (c0, width) in zip(out_refs, splits):
        for c in range(0, width, 256):
            cw = min(256, width - c)
            o_ref[:, c:c + cw] = _dot(yb, w_ref[:, c0 + c:c0 + c + cw]).astype(o_ref.dtype)


def _norm_matmul(x, g, w, splits, dtypes, tm):
    n, d = x.shape
    kern = functools.partial(_norm_matmul_kernel, splits=tuple(splits))
    return pl.pallas_call(
        kern,
        grid=(n // tm,),
        in_specs=[pl.BlockSpec((tm, d), lambda i: (i, 0)),
                  pl.BlockSpec((1, d), lambda i: (0, 0)),
                  pl.BlockSpec(w.shape, lambda i: (0, 0))],
        out_specs=[pl.BlockSpec((tm, wd), lambda i: (i, 0)) for (_, wd) in splits],
        out_shape=[jax.ShapeDtypeStruct((n, wd), dt) for (_, wd), dt in zip(splits, dtypes)],
        compiler_params=_params(1),
        name="norm_matmul",
    )(x, g, w)


def _logf_cumsum_kernel(fl_ref, b_ref, c_ref, carry_ref, *, tc):
    @pl.when(pl.program_id(1) == 0)
    def _():
        carry_ref[...] = jnp.zeros_like(carry_ref)

    x = fl_ref[...] + b_ref[...]
    ls = jnp.minimum(x, 0.0) - jnp.log(1.0 + jnp.exp(-jnp.abs(x)))
    row = lax.broadcasted_iota(jnp.int32, (tc, tc), 0)
    col = lax.broadcasted_iota(jnp.int32, (tc, tc), 1)
    tri = jnp.where(row >= col, 1.0, 0.0).astype(BF16)
    hi, mid, lo = _split3(ls)
    c = _dot(tri, hi) + _dot(tri, mid) + _dot(tri, lo) + carry_ref[...]
    c_ref[...] = c
    carry_ref[...] = c[tc - 1:tc, :]


def _logf_cumsum(fl, b_pad, tc=256):
    bsz, s, _ = fl.shape
    return pl.pallas_call(
        functools.partial(_logf_cumsum_kernel, tc=tc),
        grid=(bsz, s // tc),
        in_specs=[pl.BlockSpec((None, tc, LANES), lambda b, j: (b, j, 0)),
                  pl.BlockSpec((1, LANES), lambda b, j: (0, 0))],
        out_specs=pl.BlockSpec((None, tc, LANES), lambda b, j: (b, j, 0)),
        out_shape=jax.ShapeDtypeStruct(fl.shape, F32),
        scratch_shapes=[pltpu.VMEM((1, LANES), F32)],
        compiler_params=_params(2),
        name="logf_cumsum",
    )(fl, b_pad)


def _head_masks(x):
    lane = lax.broadcasted_iota(jnp.int32, (1, LANES), 1)
    zero = jnp.zeros_like(x)
    return jnp.where(lane < HEAD_DIM, x, zero), jnp.where(lane >= HEAD_DIM, x, zero)


def _sb_kernel(q_ref, k_ref, v_ref, tri_ref, o_ref, acc_ref, *, blk):
    i = pl.program_id(2)
    q_heads = _head_masks(q_ref[...] * jnp.asarray(HEAD_DIM ** -0.5, BF16))
    row = lax.broadcasted_iota(jnp.int32, (blk, blk), 0)
    col = lax.broadcasted_iota(jnp.int32, (blk, blk), 1)
    acc_ref[...] = jnp.zeros_like(acc_ref)

    def tile(j, run, diag):
        start = pl.multiple_of(j * blk, blk)
        k = k_ref[pl.ds(start, blk), :]
        v_heads = _head_masks(v_ref[pl.ds(start, blk), :])
        new_run = []
        upd = None
        for hd in range(2):
            z = lax.dot_general(q_heads[hd], k, _NT, preferred_element_type=F32)
            sp = jnp.maximum(z, 0.0) + jnp.log(1.0 + jnp.exp(-jnp.abs(z)))
            if diag:
                sp = jnp.where(col < row, sp, 0.0)
            suffix = _dot(sp.astype(BF16), tri_ref[...])
            w = jnp.exp(z - suffix - run[hd])
            if diag:
                w = jnp.where(col < row, w, 0.0)
            part = _dot(w.astype(BF16), v_heads[hd])
            upd = part if upd is None else upd + part
            new_run.append(run[hd] + suffix[:, 0:1])
        acc_ref[...] += upd
        return tuple(new_run)

    def alive(run):
        return (jnp.minimum(jnp.min(run[0]), jnp.min(run[1])) < EXP_CUTOFF).astype(jnp.int32)

    def cond(carry):
        jj, go, _ = carry
        return jnp.logical_and(jj < i, go > 0)

    def body(carry):
        jj, _, run = carry
        run = tile(i - 1 - jj, run, False)
        return jj + 1, alive(run), run

    zero = jnp.zeros((blk, 1), F32)
    run = tile(i, (zero, zero), True)
    lax.while_loop(cond, body, (jnp.int32(0), alive(run), run))
    o_ref[...] = acc_ref[...].astype(o_ref.dtype)


def _sb_attention(qkv, tri, blk=ATT_BLOCK):
    bsz, s, _ = qkv.shape
    return pl.pallas_call(
        functools.partial(_sb_kernel, blk=blk),
        grid=(bsz, N_PAIRS, s // blk),
        in_specs=[pl.BlockSpec((None, blk, LANES), lambda b, p, i: (b, i, p)),
                  pl.BlockSpec((None, s, LANES), lambda b, p, i: (b, 0, N_PAIRS + p)),
                  pl.BlockSpec((None, s, LANES), lambda b, p, i: (b, 0, 2 * N_PAIRS + p)),
                  pl.BlockSpec((blk, blk), lambda b, p, i: (0, 0))],
        out_specs=pl.BlockSpec((None, blk, LANES), lambda b, p, i: (b, i, p)),
        out_shape=jax.ShapeDtypeStruct((bsz, s, GROUP_DIM), BF16),
        scratch_shapes=[pltpu.VMEM((blk, LANES), F32)],
        compiler_params=_params(3),
        name="sb_attention",
    )(qkv, qkv, qkv, tri)


def _fox_kernel(q_ref, k_ref, v_ref, c_ref, o_ref, acc_ref, m_ref, l_ref, kmax_ref, *, blk, seq):
    i = pl.program_id(2)
    lane = lax.broadcasted_iota(jnp.int32, (1, LANES), 1)
    in_head = (lane < HEAD_DIM, lane >= HEAD_DIM)

    @pl.when(i == 0)
    def _():
        def chunk(t, mx):
            k = k_ref[pl.ds(pl.multiple_of(t * blk, blk), blk), :].astype(F32)
            sq = k * k
            return tuple(jnp.maximum(mx[hd], jnp.max(jnp.sum(jnp.where(in_head[hd], sq, 0.0), axis=1,
                                                              keepdims=True))) for hd in range(2))
        mx = lax.fori_loop(0, seq // blk, chunk, (jnp.float32(0.0), jnp.float32(0.0)))
        kmax_ref[0] = mx[0]
        kmax_ref[1] = mx[1]

    q_heads = _head_masks(q_ref[...] * jnp.asarray(HEAD_DIM ** -0.5, BF16))
    row = lax.broadcasted_iota(jnp.int32, (blk, blk), 0)
    col = lax.broadcasted_iota(jnp.int32, (blk, blk), 1)
    q0 = pl.multiple_of(i * blk, blk)
    acc_ref[...] = jnp.zeros_like(acc_ref)

    for hd in range(2):
        qh = q_heads[hd]
        q32 = qh.astype(F32)
        z_bound = jnp.sqrt(jnp.sum(q32 * q32, axis=1, keepdims=True) * kmax_ref[hd]) * 1.001
        c_base = c_ref[hd:hd + 1, pl.ds(q0, LANES)][:, 0:1]
        m_ref[hd] = jnp.full((blk, 1), -1e30, F32)
        l_ref[hd] = jnp.zeros((blk, 1), F32)

        def tile(start, diag, hd=hd, qh=qh, c_base=c_base, z_bound=z_bound):
            k = k_ref[pl.ds(start, blk), :]
            v = jnp.where(in_head[hd], v_ref[pl.ds(start, blk), :], jnp.zeros((), BF16))
            z = lax.dot_general(qh, k, _NT, preferred_element_type=F32)
            s = z - (c_ref[hd:hd + 1, pl.ds(start, blk)] - c_base)
            if diag:
                s = jnp.where(col <= row, s, -1e30)
            m_old = m_ref[hd]
            m_new = jnp.maximum(m_old, jnp.max(s, axis=1, keepdims=True))
            alpha = jnp.exp(m_old - m_new)
            p = jnp.exp(s - m_new)
            l_ref[hd] = alpha * l_ref[hd] + jnp.sum(p, axis=1, keepdims=True)
            m_ref[hd] = m_new
            acc_ref[...] = acc_ref[...] * jnp.where(in_head[hd], alpha, 1.0) + _dot(p.astype(BF16), v)
            prev = pl.multiple_of(jnp.maximum(start - LANES, 0), LANES)
            c_last = c_ref[hd:hd + 1, pl.ds(prev, LANES)][:, LANES - 1:LANES]
            return (jnp.max(z_bound + (c_base - c_last) - m_new) > -EXP_CUTOFF).astype(jnp.int32)

        def cond(carry):
            jj, go = carry
            return jnp.logical_and(jj < i, go > 0)

        def body(carry, tile=tile):
            jj, _ = carry
            return jj + 1, tile(pl.multiple_of((i - 1 - jj) * blk, blk), False)

        lax.while_loop(cond, body, (jnp.int32(0), tile(q0, True)))

    inv = jnp.where(in_head[0], 1.0 / l_ref[0], 1.0 / l_ref[1])
    o_ref[...] = (acc_ref[...] * inv).astype(o_ref.dtype)


def _fox_attention(qkv, c_rows, blk=FOX_BLOCK):
    bsz, s, _ = qkv.shape
    return pl.pallas_call(
        functools.partial(_fox_kernel, blk=blk, seq=s),
        grid=(bsz, N_PAIRS, s // blk),
        in_specs=[pl.BlockSpec((None, blk, LANES), lambda b, p, i: (b, i, 3 * N_PAIRS + p)),
                  pl.BlockSpec((None, s, LANES), lambda b, p, i: (b, 0, 4 * N_PAIRS + p)),
                  pl.BlockSpec((None, s, LANES), lambda b, p, i: (b, 0, 5 * N_PAIRS + p)),
                  pl.BlockSpec((None, None, 2, s), lambda b, p, i: (b, p, 0, 0))],
        out_specs=pl.BlockSpec((None, blk, LANES), lambda b, p, i: (b, i, p)),
        out_shape=jax.ShapeDtypeStruct((bsz, s, GROUP_DIM), BF16),
        scratch_shapes=[pltpu.VMEM((blk, LANES), F32), pltpu.VMEM((2, blk, 1), F32),
                        pltpu.VMEM((2, blk, 1), F32), pltpu.SMEM((2,), F32)],
        compiler_params=_params(3),
        name="fox_attention",
    )(qkv, qkv, qkv, c_rows)


def _mix_out_kernel(sb_ref, fox_ref, u_ref, halo_ref, pw_ref, ps_ref, wo_ref, g_ref, h_ref, o_ref,
                    *, tm, seq):
    pos0 = (pl.program_id(0) * tm) % seq
    u = u_ref[...]
    halo = jnp.where(pos0 == 0, 0.0, halo_ref[...])
    x = jnp.concatenate([halo, u], axis=0)
    s2 = x + pltpu.roll(x, 1, 0)
    s4 = s2 + pltpu.roll(s2, 2, 0)
    s8 = s4 + pltpu.roll(s4, 4, 0)
    s16 = s8 + pltpu.roll(s8, 8, 0)
    lane = lax.broadcasted_iota(jnp.int32, (1, POOL_DIM), 1)
    grp = POOL_DIM // len(POOL_WINDOWS)
    wsum = jnp.where(lane < grp, s2, jnp.where(lane < 2 * grp, s4, jnp.where(lane < 3 * grp, s8, s16)))
    win = jnp.where(lane < grp, 2, jnp.where(lane < 2 * grp, 4, jnp.where(lane < 3 * grp, 8, 16)))
    pos = pos0 + lax.broadcasted_iota(jnp.int32, (tm, 1), 0)
    count = jnp.minimum(pos + 1, win).astype(F32)
    r = wsum[POOL_HALO:, :] / count - u
    pool = _dot(r.astype(BF16), pw_ref[...]) * ps_ref[...]
    gd = GROUP_DIM
    a = (_dot(sb_ref[...], wo_ref[0:gd, :]) + _dot(fox_ref[...], wo_ref[gd:2 * gd, :])
         + _dot(pool.astype(BF16), wo_ref[2 * gd:, :]))
    o_ref[...] = h_ref[...] + _rms(a, g_ref[...])


def _mix_out(sb, fox, u, pool_w_bd, pool_scale, w_out, g, h, seq, tm=512):
    n, d = h.shape
    hb = tm // POOL_HALO
    return pl.pallas_call(
        functools.partial(_mix_out_kernel, tm=tm, seq=seq),
        grid=(n // tm,),
        in_specs=[pl.BlockSpec((tm, GROUP_DIM), lambda i: (i, 0)),
                  pl.BlockSpec((tm, GROUP_DIM), lambda i: (i, 0)),
                  pl.BlockSpec((tm, POOL_DIM), lambda i: (i, 0)),
                  pl.BlockSpec((POOL_HALO, POOL_DIM), lambda i: (jnp.maximum(i * hb - 1, 0), 0)),
                  pl.BlockSpec((POOL_DIM, POOL_DIM), lambda i: (0, 0)),
                  pl.BlockSpec((1, POOL_DIM), lambda i: (0, 0)),
                  pl.BlockSpec(w_out.shape, lambda i: (0, 0)),
                  pl.BlockSpec((1, d), lambda i: (0, 0)),
                  pl.BlockSpec((tm, d), lambda i: (i, 0))],
        out_specs=pl.BlockSpec((tm, d), lambda i: (i, 0)),
        out_shape=jax.ShapeDtypeStruct((n, d), F32),
        compiler_params=_params(1),
        name="mix_out",
    )(sb, fox, u, u, pool_w_bd, pool_scale, w_out, g, h)


def _xattn_kernel(h_ref, gpre_ref, wq_ref, k_ref, v_ref, wo_ref, gpost_ref, o_ref):
    h = h_ref[...]
    d = h.shape[-1]
    hd = d // XA_HEADS
    hn = _rms(h, gpre_ref[...]).astype(BF16)
    q = (_dot(hn, wq_ref[...]) * (hd ** -0.5)).astype(BF16)
    outs = []
    for a in range(XA_HEADS):
        sl = slice(a * hd, (a + 1) * hd)
        s = lax.dot_general(q[:, sl], k_ref[:, sl], _NT, preferred_element_type=F32)
        p = jnp.exp(s - jnp.max(s, axis=1, keepdims=True))
        p = p / jnp.sum(p, axis=1, keepdims=True)
        outs.append(_dot(p.astype(BF16), v_ref[:, sl]).astype(BF16))
    c = _dot(jnp.concatenate(outs, axis=1), wo_ref[...])
    o_ref[...] = h + _rms(c, gpost_ref[...])


def _xattn(h, g_pre, wq, k_mem, v_mem, wo, g_post, seq, tm=512):
    n, d = h.shape
    m_len = k_mem.shape[1]
    per_seq = seq // tm
    return pl.pallas_call(
        _xattn_kernel,
        grid=(n // tm,),
        in_specs=[pl.BlockSpec((tm, d), lambda i: (i, 0)),
                  pl.BlockSpec((1, d), lambda i: (0, 0)),
                  pl.BlockSpec((d, d), lambda i: (0, 0)),
                  pl.BlockSpec((None, m_len, d), lambda i: (i // per_seq, 0, 0)),
                  pl.BlockSpec((None, m_len, d), lambda i: (i // per_seq, 0, 0)),
                  pl.BlockSpec((d, d), lambda i: (0, 0)),
                  pl.BlockSpec((1, d), lambda i: (0, 0))],
        out_specs=pl.BlockSpec((tm, d), lambda i: (i, 0)),
        out_shape=jax.ShapeDtypeStruct((n, d), F32),
        compiler_params=_params(1),
        name="xattn",
    )(h, g_pre, wq, k_mem, v_mem, wo, g_post)


def _silu(x):
    return x / (1.0 + jnp.exp(-x))


def _swiglu(x, wg_ref, wu_ref, wd_ref, act_ref):
    ff = wg_ref.shape[1]
    for c in range(0, ff, FF_CHUNK):
        gate = _dot(x, wg_ref[:, c:c + FF_CHUNK])
        up = _dot(x, wu_ref[:, c:c + FF_CHUNK])
        act_ref[:, c:c + FF_CHUNK] = (_silu(gate) * up).astype(BF16)
    return _dot(act_ref[...], wd_ref[...])


def _ffn_kernel(h_ref, gpre_ref, wg_ref, wu_ref, wd_ref, gpost_ref, o_ref, act_ref):
    h = h_ref[...]
    f = _swiglu(_rms(h, gpre_ref[...]).astype(BF16), wg_ref, wu_ref, wd_ref, act_ref)
    o_ref[...] = h + _rms(f, gpost_ref[...])


def _ffn(h, g_pre, wg, wu, wd, g_post, tm=512):
    n, d = h.shape
    ff = wg.shape[1]
    resident = pl.Buffered(1)
    return pl.pallas_call(
        _ffn_kernel,
        grid=(n // tm,),
        in_specs=[pl.BlockSpec((tm, d), lambda i: (i, 0)),
                  pl.BlockSpec((1, d), lambda i: (0, 0)),
                  pl.BlockSpec((d, ff), lambda i: (0, 0), pipeline_mode=resident),
                  pl.BlockSpec((d, ff), lambda i: (0, 0), pipeline_mode=resident),
                  pl.BlockSpec((ff, d), lambda i: (0, 0), pipeline_mode=resident),
                  pl.BlockSpec((1, d), lambda i: (0, 0))],
        out_specs=pl.BlockSpec((tm, d), lambda i: (i, 0)),
        out_shape=jax.ShapeDtypeStruct((n, d), F32),
        scratch_shapes=[pltpu.VMEM((tm, ff), BF16)],
        compiler_params=_params(1),
        name="ffn_dense",
    )(h, g_pre, wg, wu, wd, g_post)


def _router_kernel(h_ref, g_ref, wr_ref, tri_ref, idx_ref, gate_ref, count_ref, carry_ref):
    @pl.when(pl.program_id(0) == 0)
    def _():
        carry_ref[...] = jnp.zeros_like(carry_ref)

    hn = _rms(h_ref[...], g_ref[...])
    x_hi, x_mid, _ = _split3(hn)
    w_hi, w_mid = wr_ref[0], wr_ref[1]
    logits = _dot(x_hi, w_hi) + _dot(x_hi, w_mid) + _dot(x_mid, w_hi)
    tm = logits.shape[0]
    lane = lax.broadcasted_iota(jnp.int32, logits.shape, 1)
    logits = jnp.where(lane < N_EXPERTS, logits, -jnp.inf)
    m1 = jnp.max(logits, axis=1, keepdims=True)
    i1 = jnp.min(jnp.where(logits == m1, lane, LANES), axis=1, keepdims=True)
    rest = jnp.where(lane == i1, -jnp.inf, logits)
    m2 = jnp.max(rest, axis=1, keepdims=True)
    i2 = jnp.min(jnp.where(rest == m2, lane, LANES), axis=1, keepdims=True)
    e = jnp.exp(m2 - m1)
    g1 = 1.0 / (1.0 + e)
    gate_ref[...] = jnp.where(lane == 0, g1, jnp.where(lane == 1, e * g1, 0.0))

    onehot = jnp.where(lane == i1, 1.0, jnp.where(lane == i2, 1.0, 0.0))
    before = _dot(tri_ref[...], onehot.astype(BF16)) + carry_ref[...]
    r1 = jnp.sum(jnp.where(lane == i1, before, 0.0), axis=1, keepdims=True).astype(jnp.int32)
    r2 = jnp.sum(jnp.where(lane == i2, before, 0.0), axis=1, keepdims=True).astype(jnp.int32)
    idx_ref[...] = jnp.where(lane == 0, i1, jnp.where(lane == 1, i2,
                             jnp.where(lane == 2, r1, jnp.where(lane == 3, r2, 0))))
    total = before[tm - 1:tm, :] + onehot[tm - 1:tm, :]
    carry_ref[...] = total
    count_ref[...] = total


def _router(h, g, wr_split, tm=512):
    n, d = h.shape
    t = jnp.arange(tm)
    tri = (t[:, None] > t[None, :]).astype(BF16)
    return pl.pallas_call(
        _router_kernel,
        grid=(n // tm,),
        in_specs=[pl.BlockSpec((tm, d), lambda i: (i, 0)),
                  pl.BlockSpec((1, d), lambda i: (0, 0)),
                  pl.BlockSpec(wr_split.shape, lambda i: (0, 0, 0)),
                  pl.BlockSpec((tm, tm), lambda i: (0, 0))],
        out_specs=[pl.BlockSpec((tm, LANES), lambda i: (i, 0)),
                   pl.BlockSpec((tm, LANES), lambda i: (i, 0)),
                   pl.BlockSpec((1, LANES), lambda i: (0, 0))],
        out_shape=[jax.ShapeDtypeStruct((n, LANES), jnp.int32),
                   jax.ShapeDtypeStruct((n, LANES), F32),
                   jax.ShapeDtypeStruct((1, LANES), F32)],
        scratch_shapes=[pltpu.VMEM((1, LANES), F32)],
        compiler_params=_params(1),
        name="router",
    )(h, g, wr_split, tri)


def _dispatch_kernel(dest_ref, pad_start_ref, pad_len_ref, n_valid_ref, h_ref, g_ref, xs_hbm,
                     buf_ref, zero_ref, sem, zero_sem, *, tm):
    base = pl.program_id(0) * tm * TOP_K

    @pl.when(pl.program_id(0) == 0)
    def _():
        zero_ref[...] = jnp.zeros_like(zero_ref)
        n_blocks = n_valid_ref.shape[0]

        def zero_copy(first_row, rows):
            return pltpu.make_async_copy(zero_ref.at[pl.ds(0, rows)], xs_hbm.at[pl.ds(first_row, rows)],
                                         zero_sem)

        def fill(wait):
            def go(cp):
                cp.wait() if wait else cp.start()

            for e in range(N_EXPERTS):
                pos = pad_start_ref[e]
                left = pad_len_ref[e]
                head = (-pos) & (SUBLANES - 1)
                for j in range(SUBLANES - 1):
                    @pl.when(j < head)
                    def _(pos=pos, j=j):
                        go(zero_copy(pos + j, 1))
                pos = pos + head
                left = left - head
                rows = MOE_ROWS // 2
                while rows >= SUBLANES:
                    take = (left & rows) != 0

                    @pl.when(take)
                    def _(pos=pos, rows=rows):
                        go(zero_copy(pl.multiple_of(pos, SUBLANES), rows))
                    pos = pos + jnp.where(take, rows, 0)
                    rows //= 2

            def blocks(b, carry):
                @pl.when(n_valid_ref[b] == 0)
                def _():
                    go(zero_copy(pl.multiple_of(b * MOE_ROWS, MOE_ROWS), MOE_ROWS))
                return carry
            lax.fori_loop(0, n_blocks, blocks, 0)

        fill(False)
        fill(True)

    buf_ref[...] = _rms(h_ref[...], g_ref[...])

    def row_copy(r, k):
        dst = dest_ref[base + r * TOP_K + k]
        return pltpu.make_async_copy(buf_ref.at[pl.ds(r, 1)], xs_hbm.at[pl.ds(dst, 1)], sem)

    def start(r, carry):
        for k in range(TOP_K):
            row_copy(r, k).start()
        return carry

    def wait(r, carry):
        for k in range(TOP_K):
            row_copy(r, k).wait()
        return carry

    lax.fori_loop(0, tm, start, 0)
    lax.fori_loop(0, tm, wait, 0)


def _dispatch(dest, pad_start, pad_len, n_valid, h, g, tm=512):
    n, d = h.shape
    n_rows = n_valid.shape[0] * MOE_ROWS
    grid_spec = pltpu.PrefetchScalarGridSpec(
        num_scalar_prefetch=4,
        grid=(n // tm,),
        in_specs=[pl.BlockSpec((tm, d), lambda i, *_: (i, 0)),
                  pl.BlockSpec((1, d), lambda i, *_: (0, 0))],
        out_specs=pl.BlockSpec(memory_space=pl.ANY),
        scratch_shapes=[pltpu.VMEM((tm, d), F32), pltpu.VMEM((MOE_ROWS, d), F32),
                        pltpu.SemaphoreType.DMA, pltpu.SemaphoreType.DMA],
    )
    return pl.pallas_call(
        functools.partial(_dispatch_kernel, tm=tm),
        grid_spec=grid_spec,
        out_shape=jax.ShapeDtypeStruct((n_rows, d), F32),
        compiler_params=_params(1),
        name="moe_dispatch",
    )(dest, pad_start, pad_len, n_valid, h, g)


def _expert_kernel(blk_e_ref, n_valid_ref, xs_ref, wg_ref, wu_ref, wd_ref, ys_ref, act_ref):
    n_valid = n_valid_ref[pl.program_id(0)]

    @pl.when(n_valid > 0)
    def _():
        ys_ref[...] = _swiglu(xs_ref[...].astype(BF16), wg_ref, wu_ref, wd_ref, act_ref)

    @pl.when(n_valid == 0)
    def _():
        ys_ref[...] = jnp.zeros_like(ys_ref)


def _experts(blk_e, n_valid, xs, wg, wu, wd):
    n_rows, d = xs.shape
    ff = wg.shape[2]
    grid_spec = pltpu.PrefetchScalarGridSpec(
        num_scalar_prefetch=2,
        grid=(n_rows // MOE_ROWS,),
        in_specs=[pl.BlockSpec((MOE_ROWS, d), lambda i, be, nv: (i, 0)),
                  pl.BlockSpec((None, d, ff), lambda i, be, nv: (be[i], 0, 0)),
                  pl.BlockSpec((None, d, ff), lambda i, be, nv: (be[i], 0, 0)),
                  pl.BlockSpec((None, ff, d), lambda i, be, nv: (be[i], 0, 0))],
        out_specs=pl.BlockSpec((MOE_ROWS, d), lambda i, be, nv: (i, 0)),
        scratch_shapes=[pltpu.VMEM((MOE_ROWS, ff), BF16)],
    )
    return pl.pallas_call(
        _expert_kernel,
        grid_spec=grid_spec,
        out_shape=jax.ShapeDtypeStruct((n_rows, d), F32),
        compiler_params=_params(1),
        name="moe_experts",
    )(blk_e, n_valid, xs, wg, wu, wd)


def _combine_kernel(dest_ref, ys_hbm, gate_ref, g_ref, h_ref, o_ref, buf_ref, sem, *, tm):
    i = pl.program_id(0)

    def row_copy(tile, slot, r, k):
        src = dest_ref[(tile * tm + r) * TOP_K + k]
        return pltpu.make_async_copy(ys_hbm.at[pl.ds(src, 1)], buf_ref.at[slot, k, pl.ds(r, 1)],
                                     sem.at[slot])

    def fetch(tile, slot):
        def body(r, carry):
            for k in range(TOP_K):
                row_copy(tile, slot, r, k).start()
            return carry
        lax.fori_loop(0, tm, body, 0)

    @pl.when(i == 0)
    def _():
        fetch(0, 0)

    @pl.when(i + 1 < pl.num_programs(0))
    def _():
        fetch(i + 1, (i + 1) % 2)

    slot = i % 2

    def wait(r, carry):
        for k in range(TOP_K):
            row_copy(i, slot, r, k).wait()
        return carry

    lax.fori_loop(0, tm, wait, 0)
    gates = gate_ref[...]
    f = buf_ref[slot, 0] * gates[:, 0:1] + buf_ref[slot, 1] * gates[:, 1:2]
    o_ref[...] = h_ref[...] + _rms(f, g_ref[...])


def _combine(dest, ys, gates, g, h, tm=512):
    n, d = h.shape
    grid_spec = pltpu.PrefetchScalarGridSpec(
        num_scalar_prefetch=1,
        grid=(n // tm,),
        in_specs=[pl.BlockSpec(memory_space=pl.ANY),
                  pl.BlockSpec((tm, LANES), lambda i, dst: (i, 0)),
                  pl.BlockSpec((1, d), lambda i, dst: (0, 0)),
                  pl.BlockSpec((tm, d), lambda i, dst: (i, 0))],
        out_specs=pl.BlockSpec((tm, d), lambda i, dst: (i, 0)),
        scratch_shapes=[pltpu.VMEM((2, TOP_K, tm, d), F32), pltpu.SemaphoreType.DMA((2,))],
    )
    return pl.pallas_call(
        functools.partial(_combine_kernel, tm=tm),
        grid_spec=grid_spec,
        out_shape=jax.ShapeDtypeStruct((n, d), F32),
        compiler_params=_params(1),
        name="moe_combine",
    )(dest, ys, gates, g, h)


def _route_plan(idx, counts_f, n_tok):
    counts = counts_f[0, :N_EXPERTS].astype(jnp.int32)
    padded = ((counts + MOE_ROWS - 1) // MOE_ROWS) * MOE_ROWS
    pend = jnp.cumsum(padded)
    pstart = pend - padded
    expert = idx[:, 0:TOP_K]
    rank = idx[:, TOP_K:2 * TOP_K]
    offset = jnp.zeros_like(expert)
    for e in range(N_EXPERTS):
        offset = jnp.where(expert == e, pstart[e], offset)
    dest = (offset + rank).reshape(-1)
    n_blocks = (n_tok * TOP_K) // MOE_ROWS + N_EXPERTS
    blk_start = jnp.arange(n_blocks, dtype=jnp.int32) * MOE_ROWS
    blk_e = jnp.clip(jnp.searchsorted(pend, blk_start, side='right'), 0, N_EXPERTS - 1).astype(jnp.int32)
    n_valid = jnp.clip(pstart[blk_e] + counts[blk_e] - blk_start, 0, MOE_ROWS).astype(jnp.int32)
    return dest, pstart + counts, padded - counts, blk_e, n_valid


def _moe(h, g_pre, router_w, wg, wu, wd, g_post):
    n, d = h.shape
    wr = jnp.pad(router_w, ((0, 0), (0, LANES - N_EXPERTS)))
    wr_hi = wr.astype(BF16)
    wr_mid = (wr - wr_hi.astype(F32)).astype(BF16)
    idx, gates, counts = _router(h, g_pre, jnp.stack([wr_hi, wr_mid]))
    dest, pad_start, pad_len, blk_e, n_valid = _route_plan(idx, counts, n)
    xs = _dispatch(dest, pad_start, pad_len, n_valid, h, g_pre)
    ys = _experts(blk_e, n_valid, xs, wg, wu, wd)
    return _combine(dest, ys, gates, g_post, h)


def kernel(x, mem, mix_norm_pre, mix_norm_post, w_in, b_forget, pool_w, pool_scale, w_out,
           xa_norm_pre, xa_norm_post, mem_norm, xa_wq, xa_wkv, xa_wo,
           ffn_norm_pre, ffn_norm_post, dense_w_gate, dense_w_up, dense_w_down,
           router_w, moe_w_gate, moe_w_up, moe_w_down):
    bsz, seq, d = x.shape
    m_len = mem.shape[1]
    depth = w_in.shape[0]
    n = bsz * seq
    h = x.reshape(n, d)
    mem2 = mem.reshape(bsz * m_len, d)
    row = lambda v: v.reshape(1, -1)

    idx = jnp.arange(ATT_BLOCK)
    tri = (idx[:, None] >= idx[None, :]).astype(BF16)

    for li in range(depth):
        w = w_in[li]
        flog_w = jnp.pad(w[:, QKV_DIM:QKV_DIM + N_FOX], ((0, 0), (0, LANES - N_FOX)))
        w_cat = jnp.concatenate([w[:, :QKV_DIM], w[:, QKV_DIM + N_FOX:], flog_w], axis=1).astype(BF16)
        qkv, u, flog = _norm_matmul(
            h, row(mix_norm_pre[li]), w_cat,
            splits=[(0, QKV_DIM), (QKV_DIM, POOL_DIM), (QKV_DIM + POOL_DIM, LANES)],
            dtypes=[BF16, F32, F32], tm=512)
        qkv = qkv.reshape(bsz, seq, QKV_DIM)

        b_pad = jnp.pad(b_forget[li], (0, LANES - N_FOX)).reshape(1, LANES)
        c = _logf_cumsum(flog.reshape(bsz, seq, LANES), b_pad)
        c_rows = c[:, :, :N_FOX].transpose(0, 2, 1).reshape(bsz, N_PAIRS, 2, seq)

        sb = _sb_attention(qkv, tri).reshape(n, GROUP_DIM)
        fox = _fox_attention(qkv, c_rows).reshape(n, GROUP_DIM)

        pool_bd = jax.scipy.linalg.block_diag(*[pool_w[li, gi] for gi in range(len(POOL_WINDOWS))])
        h = _mix_out(sb, fox, u, pool_bd.astype(BF16), row(pool_scale[li]), w_out[li].astype(BF16),
                     row(mix_norm_post[li]), h, seq)

        k_mem, v_mem = _norm_matmul(mem2, row(mem_norm[li]), xa_wkv[li].astype(BF16),
                                    splits=[(0, d), (d, d)], dtypes=[BF16, BF16], tm=m_len)
        h = _xattn(h, row(xa_norm_pre[li]), xa_wq[li].astype(BF16),
                   k_mem.reshape(bsz, m_len, d), v_mem.reshape(bsz, m_len, d),
                   xa_wo[li].astype(BF16), row(xa_norm_post[li]), seq)

        j = li // 2
        if li % 2 == 0:
            h = _ffn(h, row(ffn_norm_pre[li]), dense_w_gate[j].astype(BF16), dense_w_up[j].astype(BF16),
                     dense_w_down[j].astype(BF16), row(ffn_norm_post[li]))
        else:
            h = _moe(h, row(ffn_norm_pre[li]), router_w[j], moe_w_gate[j].astype(BF16),
                     moe_w_up[j].astype(BF16), moe_w_down[j].astype(BF16), row(ffn_norm_post[li]))
    return h.reshape(bsz, seq, d)
```

```python
import functools

import jax
import jax.numpy as jnp
from jax import lax
from jax.experimental import pallas as pl
from jax.experimental.pallas import tpu as pltpu

F32 = jnp.float32
BF16 = jnp.bfloat16
EPS = 1e-6

HEAD_DIM = 64
LANES = 128
SUBLANES = 8
N_PAIRS = 3
GROUP_DIM = N_PAIRS * LANES
QKV_DIM = 6 * GROUP_DIM
POOL_DIM = 256
POOL_WINDOWS = (2, 4, 8, 16)
POOL_HALO = 16
N_FOX = 6
XA_HEADS = 4
N_EXPERTS = 8
TOP_K = 2
MOE_ROWS = 512
FF_CHUNK = 256
DMA_UNROLL = 8
SB_SUB = 256
SB_CHAINS = 4
FOX_SUB = 256
FOX_CHAINS = 8
FOX_SAFE_BOUND = 40.0
EXP_CUTOFF = 105.0
VMEM_LIMIT = 56 * 1024 * 1024

_NT = (((1,), (1,)), ((), ()))


def _params(n_axes):
    return pltpu.CompilerParams(dimension_semantics=("arbitrary",) * n_axes,
                                vmem_limit_bytes=VMEM_LIMIT)


def _rms(x, g):
    return x * lax.rsqrt(jnp.mean(x * x, axis=-1, keepdims=True) + EPS) * g


def _dot(a, b):
    return jnp.dot(a, b, preferred_element_type=F32)


def _split3(x):
    hi = x.astype(BF16)
    r1 = x - hi.astype(F32)
    mid = r1.astype(BF16)
    lo = (r1 - mid.astype(F32)).astype(BF16)
    return hi, mid, lo


def _norm_matmul_kernel(x_ref, g_ref, w_ref, *out_refs, splits):
    yb = _rms(x_ref[...], g_ref[...]).astype(BF16)
    for o_ref, (c0, width) in zip(out_refs, splits):
        for c in range(0, width, 256):
            cw = min(256, width - c)
            o_ref[:, c:c + cw] = _dot(yb, w_ref[:, c0 + c:c0 + c + cw]).astype(o_ref.dtype)


def _norm_matmul(x, g, w, splits, dtypes, tm):
    n, d = x.shape
    kern = functools.partial(_norm_matmul_kernel, splits=tuple(splits))
    return pl.pallas_call(
        kern,
        grid=(n // tm,),
        in_specs=[pl.BlockSpec((tm, d), lambda i: (i, 0)),
                  pl.BlockSpec((1, d), lambda i: (0, 0)),
                  pl.BlockSpec(w.shape, lambda i: (0, 0))],
        out_specs=[pl.BlockSpec((tm, wd), lambda i: (i, 0)) for (_, wd) in splits],
        out_shape=[jax.ShapeDtypeStruct((n, wd), dt) for (_, wd), dt in zip(splits, dtypes)],
        compiler_params=_params(1),
        name="norm_matmul",
    )(x, g, w)


def _logf_cumsum_kernel(fl_ref, b_ref, c_ref, carry_ref, *, tc):
    @pl.when(pl.program_id(1) == 0)
    def _():
        carry_ref[...] = jnp.zeros_like(carry_ref)

    x = fl_ref[...] + b_ref[...]
    ls = jnp.minimum(x, 0.0) - jnp.log(1.0 + jnp.exp(-jnp.abs(x)))
    row = lax.broadcasted_iota(jnp.int32, (tc, tc), 0)
    col = lax.broadcasted_iota(jnp.int32, (tc, tc), 1)
    tri = jnp.where(row >= col, 1.0, 0.0).astype(BF16)
    hi, mid, lo = _split3(ls)
    c = _dot(tri, hi) + _dot(tri, mid) + _dot(tri, lo) + carry_ref[...]
    c_ref[...] = c
    carry_ref[...] = c[tc - 1:tc, :]


def _logf_cumsum(fl, b_pad, tc=256):
    bsz, s, _ = fl.shape
    return pl.pallas_call(
        functools.partial(_logf_cumsum_kernel, tc=tc),
        grid=(bsz, s // tc),
        in_specs=[pl.BlockSpec((None, tc, LANES), lambda b, j: (b, j, 0)),
                  pl.BlockSpec((1, LANES), lambda b, j: (0, 0))],
        out_specs=pl.BlockSpec((None, tc, LANES), lambda b, j: (b, j, 0)),
        out_shape=jax.ShapeDtypeStruct(fl.shape, F32),
        scratch_shapes=[pltpu.VMEM((1, LANES), F32)],
        compiler_params=_params(2),
        name="logf_cumsum",
    )(fl, b_pad)


def _sb_kernel(q_ref, k_ref, v_ref, tri_ref, o_ref, acc_ref, run_ref):
    sub, n_chain = SB_SUB, SB_CHAINS
    blk0 = pl.program_id(2) * n_chain
    lane = lax.broadcasted_iota(jnp.int32, (1, LANES), 1)
    in_head = (lane < HEAD_DIM, lane >= HEAD_DIM)
    q_all = q_ref[...] * jnp.asarray(HEAD_DIM ** -0.5, BF16)
    q_chain = [[jnp.where(in_head[hd], q_all[r * sub:(r + 1) * sub, :], jnp.zeros((), BF16))
                for r in range(n_chain)] for hd in range(2)]
    row = lax.broadcasted_iota(jnp.int32, (sub, sub), 0)
    col = lax.broadcasted_iota(jnp.int32, (sub, sub), 1)
    acc_ref[...] = jnp.zeros_like(acc_ref)
    run_ref[...] = jnp.zeros_like(run_ref)

    def tile(hd, r, d, diag):
        kb = blk0 + r - d
        start = pl.multiple_of(jnp.maximum(kb, 0) * sub, sub)
        k = k_ref[pl.ds(start, sub), :]
        v = jnp.where(in_head[hd], v_ref[pl.ds(start, sub), :], jnp.zeros((), BF16))
        z = lax.dot_general(q_chain[hd][r], k, _NT, preferred_element_type=F32)
        sp = jnp.maximum(z, 0.0) + jnp.log(1.0 + jnp.exp(-jnp.abs(z)))
        if diag:
            sp = jnp.where(col < row, sp, 0.0)
        suffix = _dot(sp.astype(BF16), tri_ref[...])
        run = run_ref[hd, r]
        w = jnp.exp(z - suffix - jnp.where(kb >= 0, run, 1e30))
        if diag:
            w = jnp.where(col < row, w, 0.0)
        acc_ref[hd, r] += _dot(w.astype(BF16), v)
        run_ref[hd, r] = run + suffix[:, 0:1]

    def more(d):
        go = jnp.bool_(False)
        for hd in range(2):
            for r in range(n_chain):
                unfinished = jnp.min(run_ref[hd, r]) < EXP_CUTOFF
                go = jnp.logical_or(go, jnp.logical_and(blk0 + r - d >= 0, unfinished))
        return go.astype(jnp.int32)

    def step(d, diag):
        for r in range(n_chain):
            for hd in range(2):
                tile(hd, r, d, diag)

    def body(carry):
        d, _ = carry
        step(d, False)
        return d + 1, more(d + 1)

    step(0, True)
    lax.while_loop(lambda carry: carry[1] > 0, body, (jnp.int32(1), more(1)))
    for r in range(n_chain):
        o_ref[r * sub:(r + 1) * sub, :] = (acc_ref[0, r] + acc_ref[1, r]).astype(o_ref.dtype)


def _sb_attention(qkv, tri):
    bsz, s, _ = qkv.shape
    tq = SB_SUB * SB_CHAINS
    return pl.pallas_call(
        _sb_kernel,
        grid=(bsz, N_PAIRS, s // tq),
        in_specs=[pl.BlockSpec((None, tq, LANES), lambda b, p, i: (b, i, p)),
                  pl.BlockSpec((None, s, LANES), lambda b, p, i: (b, 0, N_PAIRS + p)),
                  pl.BlockSpec((None, s, LANES), lambda b, p, i: (b, 0, 2 * N_PAIRS + p)),
                  pl.BlockSpec((SB_SUB, SB_SUB), lambda b, p, i: (0, 0))],
        out_specs=pl.BlockSpec((None, tq, LANES), lambda b, p, i: (b, i, p)),
        out_shape=jax.ShapeDtypeStruct((bsz, s, GROUP_DIM), BF16),
        scratch_shapes=[pltpu.VMEM((2, SB_CHAINS, SB_SUB, LANES), F32),
                        pltpu.VMEM((2, SB_CHAINS, SB_SUB, 1), F32)],
        compiler_params=_params(3),
        name="sb_attention",
    )(qkv, qkv, qkv, tri)


def _fox_kernel(q_ref, k_ref, v_ref, ccol_ref, crow_ref, cs_ref, o_ref, acc_ref, kmax_ref, *, seq):
    sub, n_chain = FOX_SUB, FOX_CHAINS
    pair = pl.program_id(1)
    i = pl.program_id(2)
    lane = lax.broadcasted_iota(jnp.int32, (1, LANES), 1)
    in_head = (lane < HEAD_DIM, lane >= HEAD_DIM)
    one_lane = (HEAD_DIM, 0)

    @pl.when(i == 0)
    def _():
        def chunk(t, mx):
            k = k_ref[pl.ds(pl.multiple_of(t * sub, sub), sub), :].astype(F32)
            sq = k * k
            return tuple(jnp.maximum(mx[hd], jnp.max(jnp.sum(jnp.where(in_head[hd], sq, 0.0), axis=1,
                                                              keepdims=True))) for hd in range(2))
        mx = lax.fori_loop(0, seq // sub, chunk, (jnp.float32(0.0), jnp.float32(0.0)))
        kmax_ref[0] = mx[0]
        kmax_ref[1] = mx[1]

    q_all = q_ref[...] * jnp.asarray(HEAD_DIM ** -0.5, BF16)
    row = lax.broadcasted_iota(jnp.int32, (sub, sub), 0)
    col = lax.broadcasted_iota(jnp.int32, (sub, sub), 1)
    blk0 = i * n_chain
    acc_ref[...] = jnp.zeros_like(acc_ref)

    for hd in range(2):
        q_head = jnp.where(in_head[hd], q_all, jnp.zeros((), BF16))
        q32 = q_head.astype(F32)
        zb = jnp.sqrt(jnp.sum(q32 * q32, axis=1, keepdims=True) * kmax_ref[hd]) * 1.001
        zb_max = jnp.max(zb)
        cutoff = EXP_CUTOFF + 2.0 * zb_max
        c_base = cs_ref[hd, blk0 * sub]
        c_col = jnp.sum(jnp.where(lane == 2 * pair + hd, ccol_ref[...], 0.0), axis=1, keepdims=True)
        a_all = (c_col - c_base) - zb
        v_one = jnp.where(lane == one_lane[hd], 1.0, 0.0).astype(BF16)
        q_chain = [q_head[r * sub:(r + 1) * sub, :] for r in range(n_chain)]
        a_chain = [a_all[r * sub:(r + 1) * sub, :] for r in range(n_chain)]

        def tile_inputs(kb, hd=hd, c_base=c_base, v_one=v_one):
            start = pl.multiple_of(jnp.maximum(kb, 0) * sub, sub)
            k = k_ref[pl.ds(start, sub), :]
            v = jnp.where(in_head[hd], v_ref[pl.ds(start, sub), :], v_one)
            c_row = crow_ref[hd:hd + 1, pl.ds(start, sub)] - c_base
            return k, v, jnp.where(kb >= 0, c_row, 1e30)

        def more(d, hd=hd, cutoff=cutoff):
            go = jnp.bool_(False)
            for r in range(n_chain):
                kb = blk0 + r - d
                first = cs_ref[hd, (blk0 + r) * sub]
                last = cs_ref[hd, jnp.maximum(kb, 0) * sub + sub - 1]
                go = jnp.logical_or(go, jnp.logical_and(kb >= 0, first - last > -cutoff))
            return go

        def fast_tile(r, d, diag, hd=hd, q_chain=q_chain, a_chain=a_chain, tile_inputs=tile_inputs):
            k, v, c_row = tile_inputs(blk0 + r - d)
            z = lax.dot_general(q_chain[r], k, _NT, preferred_element_type=F32)
            e = z + a_chain[r] - c_row
            if diag:
                e = jnp.where(col <= row, e, -1e30)
            acc_ref[hd, r] += _dot(jnp.exp(e).astype(BF16), v)

        def slow_tile(r, d, m_old, diag, hd=hd, q_chain=q_chain, tile_inputs=tile_inputs):
            k, v, c_row = tile_inputs(blk0 + r - d)
            s = lax.dot_general(q_chain[r], k, _NT, preferred_element_type=F32) - c_row
            if diag:
                s = jnp.where(col <= row, s, -1e30)
            m_new = jnp.maximum(m_old, jnp.max(s, axis=1, keepdims=True))
            acc_ref[hd, r] = (acc_ref[hd, r] * jnp.exp(m_old - m_new)
                              + _dot(jnp.exp(s - m_new).astype(BF16), v))
            return m_new

        @pl.when(zb_max <= FOX_SAFE_BOUND)
        def _(fast_tile=fast_tile, more=more):
            for r in range(n_chain):
                fast_tile(r, 0, True)

            def body(d):
                for r in range(n_chain):
                    fast_tile(r, d, False)
                return d + 1
            lax.while_loop(more, body, jnp.int32(1))

        @pl.when(zb_max > FOX_SAFE_BOUND)
        def _(slow_tile=slow_tile, more=more):
            m0 = jnp.full((sub, 1), -1e30, F32)
            ms = tuple(slow_tile(r, 0, m0, True) for r in range(n_chain))

            def body(carry):
                d, ms = carry
                return d + 1, tuple(slow_tile(r, d, ms[r], False) for r in range(n_chain))
            lax.while_loop(lambda carry: more(carry[0]), body, (jnp.int32(1), ms))

    for r in range(n_chain):
        acc = (acc_ref[0, r], acc_ref[1, r])
        total = [jnp.sum(jnp.where(lane == one_lane[hd], acc[hd], 0.0), axis=1, keepdims=True)
                 for hd in range(2)]
        out = jnp.where(in_head[0], acc[0] / total[0], acc[1] / total[1])
        o_ref[r * sub:(r + 1) * sub, :] = out.astype(o_ref.dtype)


def _fox_attention(qkv, c_cols, c_rows):
    bsz, s, _ = qkv.shape
    tq = FOX_SUB * FOX_CHAINS
    return pl.pallas_call(
        functools.partial(_fox_kernel, seq=s),
        grid=(bsz, N_PAIRS, s // tq),
        in_specs=[pl.BlockSpec((None, tq, LANES), lambda b, p, i: (b, i, 3 * N_PAIRS + p)),
                  pl.BlockSpec((None, s, LANES), lambda b, p, i: (b, 0, 4 * N_PAIRS + p)),
                  pl.BlockSpec((None, s, LANES), lambda b, p, i: (b, 0, 5 * N_PAIRS + p)),
                  pl.BlockSpec((None, tq, LANES), lambda b, p, i: (b, i, 0)),
                  pl.BlockSpec((None, None, 2, s), lambda b, p, i: (b, p, 0, 0)),
                  pl.BlockSpec((None, None, 2, s), lambda b, p, i: (b, p, 0, 0),
                               memory_space=pltpu.SMEM)],
        out_specs=pl.BlockSpec((None, tq, LANES), lambda b, p, i: (b, i, p)),
        out_shape=jax.ShapeDtypeStruct((bsz, s, GROUP_DIM), BF16),
        scratch_shapes=[pltpu.VMEM((2, FOX_CHAINS, FOX_SUB, LANES), F32), pltpu.SMEM((2,), F32)],
        compiler_params=_params(3),
        name="fox_attention",
    )(qkv, qkv, qkv, c_cols, c_rows, c_rows)


def _mix_out_kernel(sb_ref, fox_ref, u_ref, halo_ref, pw_ref, ps_ref, wo_ref, g_ref, h_ref, o_ref,
                    *, tm, seq):
    pos0 = (pl.program_id(0) * tm) % seq
    u = u_ref[...]
    halo = jnp.where(pos0 == 0, 0.0, halo_ref[...])
    x = jnp.concatenate([halo, u], axis=0)
    s2 = x + pltpu.roll(x, 1, 0)
    s4 = s2 + pltpu.roll(s2, 2, 0)
    s8 = s4 + pltpu.roll(s4, 4, 0)
    s16 = s8 + pltpu.roll(s8, 8, 0)
    lane = lax.broadcasted_iota(jnp.int32, (1, POOL_DIM), 1)
    grp = POOL_DIM // len(POOL_WINDOWS)
    wsum = jnp.where(lane < grp, s2, jnp.where(lane < 2 * grp, s4, jnp.where(lane < 3 * grp, s8, s16)))
    win = jnp.where(lane < grp, 2, jnp.where(lane < 2 * grp, 4, jnp.where(lane < 3 * grp, 8, 16)))
    pos = pos0 + lax.broadcasted_iota(jnp.int32, (tm, 1), 0)
    count = jnp.minimum(pos + 1, win).astype(F32)
    r = wsum[POOL_HALO:, :] / count - u
    pool = _dot(r.astype(BF16), pw_ref[...]) * ps_ref[...]
    gd = GROUP_DIM
    a = (_dot(sb_ref[...], wo_ref[0:gd, :]) + _dot(fox_ref[...], wo_ref[gd:2 * gd, :])
         + _dot(pool.astype(BF16), wo_ref[2 * gd:, :]))
    o_ref[...] = h_ref[...] + _rms(a, g_ref[...])


def _mix_out(sb, fox, u, pool_w_bd, pool_scale, w_out, g, h, seq, tm=512):
    n, d = h.shape
    hb = tm // POOL_HALO
    return pl.pallas_call(
        functools.partial(_mix_out_kernel, tm=tm, seq=seq),
        grid=(n // tm,),
        in_specs=[pl.BlockSpec((tm, GROUP_DIM), lambda i: (i, 0)),
                  pl.BlockSpec((tm, GROUP_DIM), lambda i: (i, 0)),
                  pl.BlockSpec((tm, POOL_DIM), lambda i: (i, 0)),
                  pl.BlockSpec((POOL_HALO, POOL_DIM), lambda i: (jnp.maximum(i * hb - 1, 0), 0)),
                  pl.BlockSpec((POOL_DIM, POOL_DIM), lambda i: (0, 0)),
                  pl.BlockSpec((1, POOL_DIM), lambda i: (0, 0)),
                  pl.BlockSpec(w_out.shape, lambda i: (0, 0)),
                  pl.BlockSpec((1, d), lambda i: (0, 0)),
                  pl.BlockSpec((tm, d), lambda i: (i, 0))],
        out_specs=pl.BlockSpec((tm, d), lambda i: (i, 0)),
        out_shape=jax.ShapeDtypeStruct((n, d), F32),
        compiler_params=_params(1),
        name="mix_out",
    )(sb, fox, u, u, pool_w_bd, pool_scale, w_out, g, h)


def _xattn_kernel(h_ref, gpre_ref, wq_ref, k_ref, v_ref, wo_ref, gpost_ref, o_ref):
    h = h_ref[...]
    d = h.shape[-1]
    hd = d // XA_HEADS
    hn = _rms(h, gpre_ref[...]).astype(BF16)
    q = (_dot(hn, wq_ref[...]) * (hd ** -0.5)).astype(BF16)
    outs = []
    for a in range(XA_HEADS):
        sl = slice(a * hd, (a + 1) * hd)
        s = lax.dot_general(q[:, sl], k_ref[:, sl], _NT, preferred_element_type=F32)
        p = jnp.exp(s - jnp.max(s, axis=1, keepdims=True))
        p = p / jnp.sum(p, axis=1, keepdims=True)
        outs.append(_dot(p.astype(BF16), v_ref[:, sl]).astype(BF16))
    c = _dot(jnp.concatenate(outs, axis=1), wo_ref[...])
    o_ref[...] = h + _rms(c, gpost_ref[...])


def _xattn(h, g_pre, wq, k_mem, v_mem, wo, g_post, seq, tm=512):
    n, d = h.shape
    m_len = k_mem.shape[1]
    per_seq = seq // tm
    return pl.pallas_call(
        _xattn_kernel,
        grid=(n // tm,),
        in_specs=[pl.BlockSpec((tm, d), lambda i: (i, 0)),
                  pl.BlockSpec((1, d), lambda i: (0, 0)),
                  pl.BlockSpec((d, d), lambda i: (0, 0)),
                  pl.BlockSpec((None, m_len, d), lambda i: (i // per_seq, 0, 0)),
                  pl.BlockSpec((None, m_len, d), lambda i: (i // per_seq, 0, 0)),
                  pl.BlockSpec((d, d), lambda i: (0, 0)),
                  pl.BlockSpec((1, d), lambda i: (0, 0))],
        out_specs=pl.BlockSpec((tm, d), lambda i: (i, 0)),
        out_shape=jax.ShapeDtypeStruct((n, d), F32),
        compiler_params=_params(1),
        name="xattn",
    )(h, g_pre, wq, k_mem, v_mem, wo, g_post)


def _silu(x):
    return x / (1.0 + jnp.exp(-x))


def _swiglu(x, wg_ref, wu_ref, wd_ref, act_ref):
    ff = wg_ref.shape[1]
    for c in range(0, ff, FF_CHUNK):
        gate = _dot(x, wg_ref[:, c:c + FF_CHUNK])
        up = _dot(x, wu_ref[:, c:c + FF_CHUNK])
        act_ref[:, c:c + FF_CHUNK] = (_silu(gate) * up).astype(BF16)
    return _dot(act_ref[...], wd_ref[...])


def _ffn_kernel(h_ref, gpre_ref, wg_ref, wu_ref, wd_ref, gpost_ref, o_ref, act_ref):
    h = h_ref[...]
    f = _swiglu(_rms(h, gpre_ref[...]).astype(BF16), wg_ref, wu_ref, wd_ref, act_ref)
    o_ref[...] = h + _rms(f, gpost_ref[...])


def _ffn(h, g_pre, wg, wu, wd, g_post, tm=512):
    n, d = h.shape
    ff = wg.shape[1]
    resident = pl.Buffered(1)
    return pl.pallas_call(
        _ffn_kernel,
        grid=(n // tm,),
        in_specs=[pl.BlockSpec((tm, d), lambda i: (i, 0)),
                  pl.BlockSpec((1, d), lambda i: (0, 0)),
                  pl.BlockSpec((d, ff), lambda i: (0, 0), pipeline_mode=resident),
                  pl.BlockSpec((d, ff), lambda i: (0, 0), pipeline_mode=resident),
                  pl.BlockSpec((ff, d), lambda i: (0, 0), pipeline_mode=resident),
                  pl.BlockSpec((1, d), lambda i: (0, 0))],
        out_specs=pl.BlockSpec((tm, d), lambda i: (i, 0)),
        out_shape=jax.ShapeDtypeStruct((n, d), F32),
        scratch_shapes=[pltpu.VMEM((tm, ff), BF16)],
        compiler_params=_params(1),
        name="ffn_dense",
    )(h, g_pre, wg, wu, wd, g_post)


def _router_kernel(h_ref, g_ref, wr_ref, tri_ref, idx_ref, gate_ref, count_ref, carry_ref):
    @pl.when(pl.program_id(0) == 0)
    def _():
        carry_ref[...] = jnp.zeros_like(carry_ref)

    hn = _rms(h_ref[...], g_ref[...])
    x_hi, x_mid, _ = _split3(hn)
    w_hi, w_mid = wr_ref[0], wr_ref[1]
    logits = _dot(x_hi, w_hi) + _dot(x_hi, w_mid) + _dot(x_mid, w_hi)
    tm = logits.shape[0]
    lane = lax.broadcasted_iota(jnp.int32, logits.shape, 1)
    logits = jnp.where(lane < N_EXPERTS, logits, -jnp.inf)
    m1 = jnp.max(logits, axis=1, keepdims=True)
    i1 = jnp.min(jnp.where(logits == m1, lane, LANES), axis=1, keepdims=True)
    rest = jnp.where(lane == i1, -jnp.inf, logits)
    m2 = jnp.max(rest, axis=1, keepdims=True)
    i2 = jnp.min(jnp.where(rest == m2, lane, LANES), axis=1, keepdims=True)
    e = jnp.exp(m2 - m1)
    g1 = 1.0 / (1.0 + e)
    gate_ref[...] = jnp.where(lane == 0, g1, jnp.where(lane == 1, e * g1, 0.0))

    onehot = jnp.where(lane == i1, 1.0, jnp.where(lane == i2, 1.0, 0.0))
    before = _dot(tri_ref[...], onehot.astype(BF16)) + carry_ref[...]
    r1 = jnp.sum(jnp.where(lane == i1, before, 0.0), axis=1, keepdims=True).astype(jnp.int32)
    r2 = jnp.sum(jnp.where(lane == i2, before, 0.0), axis=1, keepdims=True).astype(jnp.int32)
    idx_ref[...] = jnp.where(lane == 0, i1, jnp.where(lane == 1, i2,
                             jnp.where(lane == 2, r1, jnp.where(lane == 3, r2, 0))))
    total = before[tm - 1:tm, :] + onehot[tm - 1:tm, :]
    carry_ref[...] = total
    count_ref[...] = total


def _router(h, g, wr_split, tm=512):
    n, d = h.shape
    t = jnp.arange(tm)
    tri = (t[:, None] > t[None, :]).astype(BF16)
    return pl.pallas_call(
        _router_kernel,
        grid=(n // tm,),
        in_specs=[pl.BlockSpec((tm, d), lambda i: (i, 0)),
                  pl.BlockSpec((1, d), lambda i: (0, 0)),
                  pl.BlockSpec(wr_split.shape, lambda i: (0, 0, 0)),
                  pl.BlockSpec((tm, tm), lambda i: (0, 0))],
        out_specs=[pl.BlockSpec((tm, LANES), lambda i: (i, 0)),
                   pl.BlockSpec((tm, LANES), lambda i: (i, 0)),
                   pl.BlockSpec((1, LANES), lambda i: (0, 0))],
        out_shape=[jax.ShapeDtypeStruct((n, LANES), jnp.int32),
                   jax.ShapeDtypeStruct((n, LANES), F32),
                   jax.ShapeDtypeStruct((1, LANES), F32)],
        scratch_shapes=[pltpu.VMEM((1, LANES), F32)],
        compiler_params=_params(1),
        name="router",
    )(h, g, wr_split, tri)


def _dispatch_kernel(dest_ref, pad_start_ref, pad_len_ref, n_valid_ref, h_ref, g_ref, xs_hbm,
                     buf_ref, zero_ref, sem, zero_sem, *, tm):
    base = pl.program_id(0) * tm * TOP_K

    @pl.when(pl.program_id(0) == 0)
    def _():
        zero_ref[...] = jnp.zeros_like(zero_ref)
        n_blocks = n_valid_ref.shape[0]

        def zero_copy(first_row, rows):
            return pltpu.make_async_copy(zero_ref.at[pl.ds(0, rows)], xs_hbm.at[pl.ds(first_row, rows)],
                                         zero_sem)

        def fill(wait):
            def go(cp):
                cp.wait() if wait else cp.start()

            for e in range(N_EXPERTS):
                pos = pad_start_ref[e]
                left = pad_len_ref[e]
                head = (-pos) & (SUBLANES - 1)
                for j in range(SUBLANES - 1):
                    @pl.when(j < head)
                    def _(pos=pos, j=j):
                        go(zero_copy(pos + j, 1))
                pos = pos + head
                left = left - head
                rows = MOE_ROWS // 2
                while rows >= SUBLANES:
                    take = (left & rows) != 0

                    @pl.when(take)
                    def _(pos=pos, rows=rows):
                        go(zero_copy(pl.multiple_of(pos, SUBLANES), rows))
                    pos = pos + jnp.where(take, rows, 0)
                    rows //= 2

            def blocks(b, carry):
                @pl.when(n_valid_ref[b] == 0)
                def _():
                    go(zero_copy(pl.multiple_of(b * MOE_ROWS, MOE_ROWS), MOE_ROWS))
                return carry
            lax.fori_loop(0, n_blocks, blocks, 0)

        fill(False)
        fill(True)

    buf_ref[...] = _rms(h_ref[...], g_ref[...])

    def row_copy(r, k):
        dst = dest_ref[base + r * TOP_K + k]
        return pltpu.make_async_copy(buf_ref.at[pl.ds(r, 1)], xs_hbm.at[pl.ds(dst, 1)], sem)

    def start(r, carry):
        for k in range(TOP_K):
            row_copy(r, k).start()
        return carry

    lax.fori_loop(0, tm, start, 0, unroll=DMA_UNROLL)
    for _ in range(TOP_K):
        pltpu.make_async_copy(buf_ref, xs_hbm.at[pl.ds(0, tm)], sem).wait()


def _dispatch(dest, pad_start, pad_len, n_valid, h, g, tm=512):
    n, d = h.shape
    n_rows = n_valid.shape[0] * MOE_ROWS
    grid_spec = pltpu.PrefetchScalarGridSpec(
        num_scalar_prefetch=4,
        grid=(n // tm,),
        in_specs=[pl.BlockSpec((tm, d), lambda i, *_: (i, 0)),
                  pl.BlockSpec((1, d), lambda i, *_: (0, 0))],
        out_specs=pl.BlockSpec(memory_space=pl.ANY),
        scratch_shapes=[pltpu.VMEM((tm, d), F32), pltpu.VMEM((MOE_ROWS, d), F32),
                        pltpu.SemaphoreType.DMA, pltpu.SemaphoreType.DMA],
    )
    return pl.pallas_call(
        functools.partial(_dispatch_kernel, tm=tm),
        grid_spec=grid_spec,
        out_shape=jax.ShapeDtypeStruct((n_rows, d), F32),
        compiler_params=_params(1),
        name="moe_dispatch",
    )(dest, pad_start, pad_len, n_valid, h, g)


def _expert_kernel(blk_e_ref, n_valid_ref, xs_ref, wg_ref, wu_ref, wd_ref, ys_ref, act_ref):
    n_valid = n_valid_ref[pl.program_id(0)]

    @pl.when(n_valid > 0)
    def _():
        ys_ref[...] = _swiglu(xs_ref[...].astype(BF16), wg_ref, wu_ref, wd_ref, act_ref)

    @pl.when(n_valid == 0)
    def _():
        ys_ref[...] = jnp.zeros_like(ys_ref)


def _experts(blk_e, n_valid, xs, wg, wu, wd):
    n_rows, d = xs.shape
    ff = wg.shape[2]
    grid_spec = pltpu.PrefetchScalarGridSpec(
        num_scalar_prefetch=2,
        grid=(n_rows // MOE_ROWS,),
        in_specs=[pl.BlockSpec((MOE_ROWS, d), lambda i, be, nv: (i, 0)),
                  pl.BlockSpec((None, d, ff), lambda i, be, nv: (be[i], 0, 0)),
                  pl.BlockSpec((None, d, ff), lambda i, be, nv: (be[i], 0, 0)),
                  pl.BlockSpec((None, ff, d), lambda i, be, nv: (be[i], 0, 0))],
        out_specs=pl.BlockSpec((MOE_ROWS, d), lambda i, be, nv: (i, 0)),
        scratch_shapes=[pltpu.VMEM((MOE_ROWS, ff), BF16)],
    )
    return pl.pallas_call(
        _expert_kernel,
        grid_spec=grid_spec,
        out_shape=jax.ShapeDtypeStruct((n_rows, d), F32),
        compiler_params=_params(1),
        name="moe_experts",
    )(blk_e, n_valid, xs, wg, wu, wd)


def _combine_kernel(dest_ref, ys_hbm, gate_ref, g_ref, h_ref, o_ref, buf_ref, sem, *, tm):
    i = pl.program_id(0)

    def row_copy(tile, slot, r, k):
        src = dest_ref[(tile * tm + r) * TOP_K + k]
        return pltpu.make_async_copy(ys_hbm.at[pl.ds(src, 1)], buf_ref.at[slot, k, pl.ds(r, 1)],
                                     sem.at[slot])

    def fetch(tile, slot):
        def body(r, carry):
            for k in range(TOP_K):
                row_copy(tile, slot, r, k).start()
            return carry
        lax.fori_loop(0, tm, body, 0, unroll=DMA_UNROLL)

    @pl.when(i == 0)
    def _():
        fetch(0, 0)

    @pl.when(i + 1 < pl.num_programs(0))
    def _():
        fetch(i + 1, (i + 1) % 2)

    slot = i % 2

    for k in range(TOP_K):
        pltpu.make_async_copy(ys_hbm.at[pl.ds(0, tm)], buf_ref.at[slot, k], sem.at[slot]).wait()
    gates = gate_ref[...]
    f = buf_ref[slot, 0] * gates[:, 0:1] + buf_ref[slot, 1] * gates[:, 1:2]
    o_ref[...] = h_ref[...] + _rms(f, g_ref[...])


def _combine(dest, ys, gates, g, h, tm=512):
    n, d = h.shape
    grid_spec = pltpu.PrefetchScalarGridSpec(
        num_scalar_prefetch=1,
        grid=(n // tm,),
        in_specs=[pl.BlockSpec(memory_space=pl.ANY),
                  pl.BlockSpec((tm, LANES), lambda i, dst: (i, 0)),
                  pl.BlockSpec((1, d), lambda i, dst: (0, 0)),
                  pl.BlockSpec((tm, d), lambda i, dst: (i, 0))],
        out_specs=pl.BlockSpec((tm, d), lambda i, dst: (i, 0)),
        scratch_shapes=[pltpu.VMEM((2, TOP_K, tm, d), F32), pltpu.SemaphoreType.DMA((2,))],
    )
    return pl.pallas_call(
        functools.partial(_combine_kernel, tm=tm),
        grid_spec=grid_spec,
        out_shape=jax.ShapeDtypeStruct((n, d), F32),
        compiler_params=_params(1),
        name="moe_combine",
    )(dest, ys, gates, g, h)


def _route_plan(idx, counts_f, n_tok):
    counts = counts_f[0, :N_EXPERTS].astype(jnp.int32)
    padded = ((counts + MOE_ROWS - 1) // MOE_ROWS) * MOE_ROWS
    pend = jnp.cumsum(padded)
    pstart = pend - padded
    expert = idx[:, 0:TOP_K]
    rank = idx[:, TOP_K:2 * TOP_K]
    offset = jnp.zeros_like(expert)
    for e in range(N_EXPERTS):
        offset = jnp.where(expert == e, pstart[e], offset)
    dest = (offset + rank).reshape(-1)
    n_blocks = (n_tok * TOP_K) // MOE_ROWS + N_EXPERTS
    blk_start = jnp.arange(n_blocks, dtype=jnp.int32) * MOE_ROWS
    blk_e = jnp.clip(jnp.searchsorted(pend, blk_start, side='right'), 0, N_EXPERTS - 1).astype(jnp.int32)
    n_valid = jnp.clip(pstart[blk_e] + counts[blk_e] - blk_start, 0, MOE_ROWS).astype(jnp.int32)
    return dest, pstart + counts, padded - counts, blk_e, n_valid


def _moe(h, g_pre, router_w, wg, wu, wd, g_post):
    n, d = h.shape
    wr = jnp.pad(router_w, ((0, 0), (0, LANES - N_EXPERTS)))
    wr_hi = wr.astype(BF16)
    wr_mid = (wr - wr_hi.astype(F32)).astype(BF16)
    idx, gates, counts = _router(h, g_pre, jnp.stack([wr_hi, wr_mid]))
    dest, pad_start, pad_len, blk_e, n_valid = _route_plan(idx, counts, n)
    xs = _dispatch(dest, pad_start, pad_len, n_valid, h, g_pre)
    ys = _experts(blk_e, n_valid, xs, wg, wu, wd)
    return _combine(dest, ys, gates, g_post, h)


def kernel(x, mem, mix_norm_pre, mix_norm_post, w_in, b_forget, pool_w, pool_scale, w_out,
           xa_norm_pre, xa_norm_post, mem_norm, xa_wq, xa_wkv, xa_wo,
           ffn_norm_pre, ffn_norm_post, dense_w_gate, dense_w_up, dense_w_down,
           router_w, moe_w_gate, moe_w_up, moe_w_down):
    bsz, seq, d = x.shape
    m_len = mem.shape[1]
    depth = w_in.shape[0]
    n = bsz * seq
    h = x.reshape(n, d)
    mem2 = mem.reshape(bsz * m_len, d)
    row = lambda v: v.reshape(1, -1)

    idx = jnp.arange(SB_SUB)
    tri = (idx[:, None] >= idx[None, :]).astype(BF16)

    for li in range(depth):
        w = w_in[li]
        flog_w = jnp.pad(w[:, QKV_DIM:QKV_DIM + N_FOX], ((0, 0), (0, LANES - N_FOX)))
        w_cat = jnp.concatenate([w[:, :QKV_DIM], w[:, QKV_DIM + N_FOX:], flog_w], axis=1).astype(BF16)
        qkv, u, flog = _norm_matmul(
            h, row(mix_norm_pre[li]), w_cat,
            splits=[(0, QKV_DIM), (QKV_DIM, POOL_DIM), (QKV_DIM + POOL_DIM, LANES)],
            dtypes=[BF16, F32, F32], tm=512)
        qkv = qkv.reshape(bsz, seq, QKV_DIM)

        b_pad = jnp.pad(b_forget[li], (0, LANES - N_FOX)).reshape(1, LANES)
        c = _logf_cumsum(flog.reshape(bsz, seq, LANES), b_pad)
        c_rows = c[:, :, :N_FOX].transpose(0, 2, 1).reshape(bsz, N_PAIRS, 2, seq)

        sb = _sb_attention(qkv, tri).reshape(n, GROUP_DIM)
        fox = _fox_attention(qkv, c, c_rows).reshape(n, GROUP_DIM)

        pool_bd = jax.scipy.linalg.block_diag(*[pool_w[li, gi] for gi in range(len(POOL_WINDOWS))])
        h = _mix_out(sb, fox, u, pool_bd.astype(BF16), row(pool_scale[li]), w_out[li].astype(BF16),
                     row(mix_norm_post[li]), h, seq)

        k_mem, v_mem = _norm_matmul(mem2, row(mem_norm[li]), xa_wkv[li].astype(BF16),
                                    splits=[(0, d), (d, d)], dtypes=[BF16, BF16], tm=m_len)
        h = _xattn(h, row(xa_norm_pre[li]), xa_wq[li].astype(BF16),
                   k_mem.reshape(bsz, m_len, d), v_mem.reshape(bsz, m_len, d),
                   xa_wo[li].astype(BF16), row(xa_norm_post[li]), seq)

        j = li // 2
        if li % 2 == 0:
            h = _ffn(h, row(ffn_norm_pre[li]), dense_w_gate[j].astype(BF16), dense_w_up[j].astype(BF16),
                     dense_w_down[j].astype(BF16), row(ffn_norm_post[li]))
        else:
            h = _moe(h, row(ffn_norm_pre[li]), router_w[j], moe_w_gate[j].astype(BF16),
                     moe_w_up[j].astype(BF16), moe_w_down[j].astype(BF16), row(ffn_norm_post[li]))
    return h.reshape(bsz, seq, d)
```

```python
import functools

import jax
import jax.numpy as jnp
from jax import lax
from jax.experimental import pallas as pl
from jax.experimental.pallas import tpu as pltpu

F32 = jnp.float32
BF16 = jnp.bfloat16
EPS = 1e-6

HEAD_DIM = 64
LANES = 128
SUBLANES = 8
N_PAIRS = 3
GROUP_DIM = N_PAIRS * LANES
QKV_DIM = 6 * GROUP_DIM
POOL_DIM = 256
POOL_WINDOWS = (2, 4, 8, 16)
POOL_HALO = 16
N_FOX = 6
XA_HEADS = 4
N_EXPERTS = 8
TOP_K = 2
MOE_ROWS = 512
FF_CHUNK = 256
DMA_UNROLL = 8
SB_SUB = 256
SB_CHAINS = 4
FOX_SUB = 256
FOX_CHAINS = 8
FOX_SAFE_BOUND = 40.0
EXP_CUTOFF = 105.0
VMEM_LIMIT = 56 * 1024 * 1024

_NT = (((1,), (1,)), ((), ()))


def _params(n_axes):
    return pltpu.CompilerParams(dimension_semantics=("arbitrary",) * n_axes,
                                vmem_limit_bytes=VMEM_LIMIT)


def _rms(x, g):
    return x * lax.rsqrt(jnp.mean(x * x, axis=-1, keepdims=True) + EPS) * g


def _dot(a, b):
    return jnp.dot(a, b, preferred_element_type=F32)


def _split3(x):
    hi = x.astype(BF16)
    r1 = x - hi.astype(F32)
    mid = r1.astype(BF16)
    lo = (r1 - mid.astype(F32)).astype(BF16)
    return hi, mid, lo


def _norm_matmul_kernel(x_ref, g_ref, w_ref, *out_refs, splits):
    yb = _rms(x_ref[...], g_ref[...]).astype(BF16)
    for o_ref, (c0, width) in zip(out_refs, splits):
        for c in range(0, width, 256):
            cw = min(256, width - c)
            o_ref[:, c:c + cw] = _dot(yb, w_ref[:, c0 + c:c0 + c + cw]).astype(o_ref.dtype)


def _norm_matmul(x, g, w, splits, dtypes, tm):
    n, d = x.shape
    kern = functools.partial(_norm_matmul_kernel, splits=tuple(splits))
    return pl.pallas_call(
        kern,
        grid=(n // tm,),
        in_specs=[pl.BlockSpec((tm, d), lambda i: (i, 0)),
                  pl.BlockSpec((1, d), lambda i: (0, 0)),
                  pl.BlockSpec(w.shape, lambda i: (0, 0))],
        out_specs=[pl.BlockSpec((tm, wd), lambda i: (i, 0)) for (_, wd) in splits],
        out_shape=[jax.ShapeDtypeStruct((n, wd), dt) for (_, wd), dt in zip(splits, dtypes)],
        compiler_params=_params(1),
        name="norm_matmul",
    )(x, g, w)


def _logf_cumsum_kernel(fl_ref, b_ref, c_ref, carry_ref, *, tc):
    @pl.when(pl.program_id(1) == 0)
    def _():
        carry_ref[...] = jnp.zeros_like(carry_ref)

    x = fl_ref[...] + b_ref[...]
    ls = jnp.minimum(x, 0.0) - jnp.log(1.0 + jnp.exp(-jnp.abs(x)))
    row = lax.broadcasted_iota(jnp.int32, (tc, tc), 0)
    col = lax.broadcasted_iota(jnp.int32, (tc, tc), 1)
    tri = jnp.where(row >= col, 1.0, 0.0).astype(BF16)
    hi, mid, lo = _split3(ls)
    c = _dot(tri, hi) + _dot(tri, mid) + _dot(tri, lo) + carry_ref[...]
    c_ref[...] = c
    carry_ref[...] = c[tc - 1:tc, :]


def _logf_cumsum(fl, b_pad, tc=1024):
    bsz, s, _ = fl.shape
    return pl.pallas_call(
        functools.partial(_logf_cumsum_kernel, tc=tc),
        grid=(bsz, s // tc),
        in_specs=[pl.BlockSpec((None, tc, LANES), lambda b, j: (b, j, 0)),
                  pl.BlockSpec((1, LANES), lambda b, j: (0, 0))],
        out_specs=pl.BlockSpec((None, tc, LANES), lambda b, j: (b, j, 0)),
        out_shape=jax.ShapeDtypeStruct(fl.shape, F32),
        scratch_shapes=[pltpu.VMEM((1, LANES), F32)],
        compiler_params=_params(2),
        name="logf_cumsum",
    )(fl, b_pad)


def _sb_kernel(q_ref, k_ref, v_ref, tri_ref, o_ref, acc_ref, run_ref):
    sub, n_chain = SB_SUB, SB_CHAINS
    blk0 = pl.program_id(2) * n_chain
    lane = lax.broadcasted_iota(jnp.int32, (1, LANES), 1)
    in_head = (lane < HEAD_DIM, lane >= HEAD_DIM)
    q_all = q_ref[...] * jnp.asarray(HEAD_DIM ** -0.5, BF16)
    q_chain = [[jnp.where(in_head[hd], q_all[r * sub:(r + 1) * sub, :], jnp.zeros((), BF16))
                for r in range(n_chain)] for hd in range(2)]
    row = lax.broadcasted_iota(jnp.int32, (sub, sub), 0)
    col = lax.broadcasted_iota(jnp.int32, (sub, sub), 1)
    acc_ref[...] = jnp.zeros_like(acc_ref)
    run_ref[...] = jnp.zeros_like(run_ref)

    def tile(hd, r, d, diag):
        kb = blk0 + r - d
        start = pl.multiple_of(jnp.maximum(kb, 0) * sub, sub)
        k = k_ref[pl.ds(start, sub), :]
        v = jnp.where(in_head[hd], v_ref[pl.ds(start, sub), :], jnp.zeros((), BF16))
        z = lax.dot_general(q_chain[hd][r], k, _NT, preferred_element_type=F32)
        sp = jnp.maximum(z, 0.0) + jnp.log(1.0 + jnp.exp(-jnp.abs(z)))
        if diag:
            sp = jnp.where(col < row, sp, 0.0)
        suffix = _dot(sp.astype(BF16), tri_ref[...])
        run = run_ref[hd, r]
        w = jnp.exp(z - suffix - jnp.where(kb >= 0, run, 1e30))
        if diag:
            w = jnp.where(col < row, w, 0.0)
        acc_ref[hd, r] += _dot(w.astype(BF16), v)
        run_ref[hd, r] = run + suffix[:, 0:1]

    def more(d):
        go = jnp.bool_(False)
        for hd in range(2):
            for r in range(n_chain):
                unfinished = jnp.min(run_ref[hd, r]) < EXP_CUTOFF
                go = jnp.logical_or(go, jnp.logical_and(blk0 + r - d >= 0, unfinished))
        return go.astype(jnp.int32)

    def step(d, diag):
        for r in range(n_chain):
            for hd in range(2):
                tile(hd, r, d, diag)

    def body(carry):
        d, _ = carry
        step(d, False)
        return d + 1, more(d + 1)

    step(0, True)
    lax.while_loop(lambda carry: carry[1] > 0, body, (jnp.int32(1), more(1)))
    for r in range(n_chain):
        o_ref[r * sub:(r + 1) * sub, :] = (acc_ref[0, r] + acc_ref[1, r]).astype(o_ref.dtype)


def _sb_attention(qkv, tri):
    bsz, s, _ = qkv.shape
    tq = SB_SUB * SB_CHAINS
    return pl.pallas_call(
        _sb_kernel,
        grid=(bsz, N_PAIRS, s // tq),
        in_specs=[pl.BlockSpec((None, tq, LANES), lambda b, p, i: (b, i, p)),
                  pl.BlockSpec((None, s, LANES), lambda b, p, i: (b, 0, N_PAIRS + p)),
                  pl.BlockSpec((None, s, LANES), lambda b, p, i: (b, 0, 2 * N_PAIRS + p)),
                  pl.BlockSpec((SB_SUB, SB_SUB), lambda b, p, i: (0, 0))],
        out_specs=pl.BlockSpec((None, tq, LANES), lambda b, p, i: (b, i, p)),
        out_shape=jax.ShapeDtypeStruct((bsz, s, GROUP_DIM), BF16),
        scratch_shapes=[pltpu.VMEM((2, SB_CHAINS, SB_SUB, LANES), F32),
                        pltpu.VMEM((2, SB_CHAINS, SB_SUB, 1), F32)],
        compiler_params=_params(3),
        name="sb_attention",
    )(qkv, qkv, qkv, tri)


def _fox_kernel(q_ref, k_ref, v_ref, ccol_ref, crow_ref, cs_ref, o_ref, acc_ref, z_ref, kmax_ref,
                *, seq):
    sub, n_chain = FOX_SUB, FOX_CHAINS
    pair = pl.program_id(1)
    i = pl.program_id(2)
    lane = lax.broadcasted_iota(jnp.int32, (1, LANES), 1)
    in_head = (lane < HEAD_DIM, lane >= HEAD_DIM)
    one_lane = (HEAD_DIM, 0)

    @pl.when(i == 0)
    def _():
        def chunk(t, mx):
            k = k_ref[pl.ds(pl.multiple_of(t * sub, sub), sub), :].astype(F32)
            sq = k * k
            return tuple(jnp.maximum(mx[hd], jnp.max(jnp.sum(jnp.where(in_head[hd], sq, 0.0), axis=1,
                                                              keepdims=True))) for hd in range(2))
        mx = lax.fori_loop(0, seq // sub, chunk, (jnp.float32(0.0), jnp.float32(0.0)))
        kmax_ref[0] = mx[0]
        kmax_ref[1] = mx[1]

    q_all = q_ref[...] * jnp.asarray(HEAD_DIM ** -0.5, BF16)
    row = lax.broadcasted_iota(jnp.int32, (sub, sub), 0)
    col = lax.broadcasted_iota(jnp.int32, (sub, sub), 1)
    blk0 = i * n_chain
    acc_ref[...] = jnp.zeros_like(acc_ref)

    for hd in range(2):
        q_head = jnp.where(in_head[hd], q_all, jnp.zeros((), BF16))
        q32 = q_head.astype(F32)
        zb = jnp.sqrt(jnp.sum(q32 * q32, axis=1, keepdims=True) * kmax_ref[hd]) * 1.001
        zb_max = jnp.max(zb)
        cutoff = EXP_CUTOFF + 2.0 * zb_max
        c_base = cs_ref[hd, blk0 * sub]
        c_col = jnp.sum(jnp.where(lane == 2 * pair + hd, ccol_ref[...], 0.0), axis=1, keepdims=True)
        a_all = (c_col - c_base) - zb
        v_one = jnp.where(lane == one_lane[hd], 1.0, 0.0).astype(BF16)
        q_chain = [q_head[r * sub:(r + 1) * sub, :] for r in range(n_chain)]
        a_chain = [a_all[r * sub:(r + 1) * sub, :] for r in range(n_chain)]

        def tile_inputs(kb, hd=hd, c_base=c_base, v_one=v_one):
            start = pl.multiple_of(jnp.maximum(kb, 0) * sub, sub)
            k = k_ref[pl.ds(start, sub), :]
            v = jnp.where(in_head[hd], v_ref[pl.ds(start, sub), :], v_one)
            c_row = crow_ref[hd:hd + 1, pl.ds(start, sub)] - c_base
            return k, v, jnp.where(kb >= 0, c_row, 1e30)

        def more(d, cutoffs, hd=hd):
            go = jnp.bool_(False)
            for r in range(n_chain):
                kb = blk0 + r - d
                first = cs_ref[hd, (blk0 + r) * sub]
                last = cs_ref[hd, jnp.maximum(kb, 0) * sub + sub - 1]
                go = jnp.logical_or(go, jnp.logical_and(kb >= 0, first - last > -cutoffs[r]))
            return go

        def scores(r, d, q_chain=q_chain):
            start = pl.multiple_of(jnp.maximum(blk0 + r - d, 0) * sub, sub)
            return lax.dot_general(q_chain[r], k_ref[pl.ds(start, sub), :], _NT,
                                   preferred_element_type=F32)

        def fast_tile(r, d, z, diag, hd=hd, a_chain=a_chain, tile_inputs=tile_inputs):
            _, v, c_row = tile_inputs(blk0 + r - d)
            e = z + a_chain[r] - c_row
            if diag:
                e = jnp.where(col <= row, e, -1e30)
            acc_ref[hd, r] += _dot(jnp.exp(e).astype(BF16), v)
            return jnp.min(jnp.max(e, axis=1, keepdims=True)) if diag else None

        def slow_tile(r, d, m_old, diag, hd=hd, q_chain=q_chain, tile_inputs=tile_inputs):
            k, v, c_row = tile_inputs(blk0 + r - d)
            s = lax.dot_general(q_chain[r], k, _NT, preferred_element_type=F32) - c_row
            if diag:
                s = jnp.where(col <= row, s, -1e30)
            m_new = jnp.maximum(m_old, jnp.max(s, axis=1, keepdims=True))
            acc_ref[hd, r] = (acc_ref[hd, r] * jnp.exp(m_old - m_new)
                              + _dot(jnp.exp(s - m_new).astype(BF16), v))
            return m_new

        @pl.when(zb_max <= FOX_SAFE_BOUND)
        def _(fast_tile=fast_tile, scores=scores, more=more):
            cutoffs = []
            for r in range(n_chain):
                cutoffs.append(EXP_CUTOFF - fast_tile(r, 0, scores(r, 0), True))
                z_ref[r] = scores(r, 1)

            def body(d):
                for r in range(n_chain):
                    z = z_ref[r]
                    z_ref[r] = scores(r, d + 1)
                    fast_tile(r, d, z, False)
                return d + 1
            lax.while_loop(lambda d: more(d, cutoffs), body, jnp.int32(1))

        @pl.when(zb_max > FOX_SAFE_BOUND)
        def _(slow_tile=slow_tile, more=more):
            m0 = jnp.full((sub, 1), -1e30, F32)
            ms = tuple(slow_tile(r, 0, m0, True) for r in range(n_chain))

            def body(carry):
                d, ms = carry
                return d + 1, tuple(slow_tile(r, d, ms[r], False) for r in range(n_chain))
            lax.while_loop(lambda carry: more(carry[0], [cutoff] * n_chain), body, (jnp.int32(1), ms))

    for r in range(n_chain):
        acc = (acc_ref[0, r], acc_ref[1, r])
        total = [jnp.sum(jnp.where(lane == one_lane[hd], acc[hd], 0.0), axis=1, keepdims=True)
                 for hd in range(2)]
        out = jnp.where(in_head[0], acc[0] / total[0], acc[1] / total[1])
        o_ref[r * sub:(r + 1) * sub, :] = out.astype(o_ref.dtype)


def _fox_attention(qkv, c_cols, c_rows):
    bsz, s, _ = qkv.shape
    tq = FOX_SUB * FOX_CHAINS
    return pl.pallas_call(
        functools.partial(_fox_kernel, seq=s),
        grid=(bsz, N_PAIRS, s // tq),
        in_specs=[pl.BlockSpec((None, tq, LANES), lambda b, p, i: (b, i, 3 * N_PAIRS + p)),
                  pl.BlockSpec((None, s, LANES), lambda b, p, i: (b, 0, 4 * N_PAIRS + p)),
                  pl.BlockSpec((None, s, LANES), lambda b, p, i: (b, 0, 5 * N_PAIRS + p)),
                  pl.BlockSpec((None, tq, LANES), lambda b, p, i: (b, i, 0)),
                  pl.BlockSpec((None, None, 2, s), lambda b, p, i: (b, p, 0, 0)),
                  pl.BlockSpec((None, None, 2, s), lambda b, p, i: (b, p, 0, 0),
                               memory_space=pltpu.SMEM)],
        out_specs=pl.BlockSpec((None, tq, LANES), lambda b, p, i: (b, i, p)),
        out_shape=jax.ShapeDtypeStruct((bsz, s, GROUP_DIM), BF16),
        scratch_shapes=[pltpu.VMEM((2, FOX_CHAINS, FOX_SUB, LANES), F32),
                        pltpu.VMEM((FOX_CHAINS, FOX_SUB, FOX_SUB), F32), pltpu.SMEM((2,), F32)],
        compiler_params=_params(3),
        name="fox_attention",
    )(qkv, qkv, qkv, c_cols, c_rows, c_rows)


def _mix_out_kernel(sb_ref, fox_ref, u_ref, halo_ref, pw_ref, ps_ref, wo_ref, g_ref, h_ref, o_ref,
                    *, tm, seq):
    pos0 = (pl.program_id(0) * tm) % seq
    u = u_ref[...]
    halo = jnp.where(pos0 == 0, 0.0, halo_ref[...])
    x = jnp.concatenate([halo, u], axis=0)
    s2 = x + pltpu.roll(x, 1, 0)
    s4 = s2 + pltpu.roll(s2, 2, 0)
    s8 = s4 + pltpu.roll(s4, 4, 0)
    s16 = s8 + pltpu.roll(s8, 8, 0)
    lane = lax.broadcasted_iota(jnp.int32, (1, POOL_DIM), 1)
    grp = POOL_DIM // len(POOL_WINDOWS)
    wsum = jnp.where(lane < grp, s2, jnp.where(lane < 2 * grp, s4, jnp.where(lane < 3 * grp, s8, s16)))
    win = jnp.where(lane < grp, 2, jnp.where(lane < 2 * grp, 4, jnp.where(lane < 3 * grp, 8, 16)))
    pos = pos0 + lax.broadcasted_iota(jnp.int32, (tm, 1), 0)
    count = jnp.minimum(pos + 1, win).astype(F32)
    r = wsum[POOL_HALO:, :] / count - u
    pool = _dot(r.astype(BF16), pw_ref[...]) * ps_ref[...]
    gd = GROUP_DIM
    a = (_dot(sb_ref[...], wo_ref[0:gd, :]) + _dot(fox_ref[...], wo_ref[gd:2 * gd, :])
         + _dot(pool.astype(BF16), wo_ref[2 * gd:, :]))
    o_ref[...] = h_ref[...] + _rms(a, g_ref[...])


def _mix_out(sb, fox, u, pool_w_bd, pool_scale, w_out, g, h, seq, tm=512):
    n, d = h.shape
    hb = tm // POOL_HALO
    return pl.pallas_call(
        functools.partial(_mix_out_kernel, tm=tm, seq=seq),
        grid=(n // tm,),
        in_specs=[pl.BlockSpec((tm, GROUP_DIM), lambda i: (i, 0)),
                  pl.BlockSpec((tm, GROUP_DIM), lambda i: (i, 0)),
                  pl.BlockSpec((tm, POOL_DIM), lambda i: (i, 0)),
                  pl.BlockSpec((POOL_HALO, POOL_DIM), lambda i: (jnp.maximum(i * hb - 1, 0), 0)),
                  pl.BlockSpec((POOL_DIM, POOL_DIM), lambda i: (0, 0)),
                  pl.BlockSpec((1, POOL_DIM), lambda i: (0, 0)),
                  pl.BlockSpec(w_out.shape, lambda i: (0, 0)),
                  pl.BlockSpec((1, d), lambda i: (0, 0)),
                  pl.BlockSpec((tm, d), lambda i: (i, 0))],
        out_specs=pl.BlockSpec((tm, d), lambda i: (i, 0)),
        out_shape=jax.ShapeDtypeStruct((n, d), F32),
        compiler_params=_params(1),
        name="mix_out",
    )(sb, fox, u, u, pool_w_bd, pool_scale, w_out, g, h)


def _xattn_kernel(h_ref, gpre_ref, wq_ref, k_ref, v_ref, wo_ref, gpost_ref, o_ref):
    h = h_ref[...]
    d = h.shape[-1]
    hd = d // XA_HEADS
    hn = _rms(h, gpre_ref[...]).astype(BF16)
    q = (_dot(hn, wq_ref[...]) * (hd ** -0.5)).astype(BF16)
    outs = []
    for a in range(XA_HEADS):
        sl = slice(a * hd, (a + 1) * hd)
        s = lax.dot_general(q[:, sl], k_ref[:, sl], _NT, preferred_element_type=F32)
        p = jnp.exp(s - jnp.max(s, axis=1, keepdims=True))
        p = p / jnp.sum(p, axis=1, keepdims=True)
        outs.append(_dot(p.astype(BF16), v_ref[:, sl]).astype(BF16))
    c = _dot(jnp.concatenate(outs, axis=1), wo_ref[...])
    o_ref[...] = h + _rms(c, gpost_ref[...])


def _xattn(h, g_pre, wq, k_mem, v_mem, wo, g_post, seq, tm=512):
    n, d = h.shape
    m_len = k_mem.shape[1]
    per_seq = seq // tm
    return pl.pallas_call(
        _xattn_kernel,
        grid=(n // tm,),
        in_specs=[pl.BlockSpec((tm, d), lambda i: (i, 0)),
                  pl.BlockSpec((1, d), lambda i: (0, 0)),
                  pl.BlockSpec((d, d), lambda i: (0, 0)),
                  pl.BlockSpec((None, m_len, d), lambda i: (i // per_seq, 0, 0)),
                  pl.BlockSpec((None, m_len, d), lambda i: (i // per_seq, 0, 0)),
                  pl.BlockSpec((d, d), lambda i: (0, 0)),
                  pl.BlockSpec((1, d), lambda i: (0, 0))],
        out_specs=pl.BlockSpec((tm, d), lambda i: (i, 0)),
        out_shape=jax.ShapeDtypeStruct((n, d), F32),
        compiler_params=_params(1),
        name="xattn",
    )(h, g_pre, wq, k_mem, v_mem, wo, g_post)


def _silu(x):
    return x / (1.0 + jnp.exp(-x))


def _swiglu(x, wg_ref, wu_ref, wd_ref, act_ref):
    ff = wg_ref.shape[1]
    for c in range(0, ff, FF_CHUNK):
        gate = _dot(x, wg_ref[:, c:c + FF_CHUNK])
        up = _dot(x, wu_ref[:, c:c + FF_CHUNK])
        act_ref[:, c:c + FF_CHUNK] = (_silu(gate) * up).astype(BF16)
    return _dot(act_ref[...], wd_ref[...])


def _ffn_kernel(h_ref, gpre_ref, wg_ref, wu_ref, wd_ref, gpost_ref, o_ref, act_ref):
    h = h_ref[...]
    f = _swiglu(_rms(h, gpre_ref[...]).astype(BF16), wg_ref, wu_ref, wd_ref, act_ref)
    o_ref[...] = h + _rms(f, gpost_ref[...])


def _ffn(h, g_pre, wg, wu, wd, g_post, tm=512):
    n, d = h.shape
    ff = wg.shape[1]
    resident = pl.Buffered(1)
    return pl.pallas_call(
        _ffn_kernel,
        grid=(n // tm,),
        in_specs=[pl.BlockSpec((tm, d), lambda i: (i, 0)),
                  pl.BlockSpec((1, d), lambda i: (0, 0)),
                  pl.BlockSpec((d, ff), lambda i: (0, 0), pipeline_mode=resident),
                  pl.BlockSpec((d, ff), lambda i: (0, 0), pipeline_mode=resident),
                  pl.BlockSpec((ff, d), lambda i: (0, 0), pipeline_mode=resident),
                  pl.BlockSpec((1, d), lambda i: (0, 0))],
        out_specs=pl.BlockSpec((tm, d), lambda i: (i, 0)),
        out_shape=jax.ShapeDtypeStruct((n, d), F32),
        scratch_shapes=[pltpu.VMEM((tm, ff), BF16)],
        compiler_params=_params(1),
        name="ffn_dense",
    )(h, g_pre, wg, wu, wd, g_post)


def _router_kernel(h_ref, g_ref, wr_ref, tri_ref, idx_ref, gate_ref, count_ref, carry_ref):
    @pl.when(pl.program_id(0) == 0)
    def _():
        carry_ref[...] = jnp.zeros_like(carry_ref)

    hn = _rms(h_ref[...], g_ref[...])
    x_hi, x_mid, _ = _split3(hn)
    w_hi, w_mid = wr_ref[0], wr_ref[1]
    logits = _dot(x_hi, w_hi) + _dot(x_hi, w_mid) + _dot(x_mid, w_hi)
    tm = logits.shape[0]
    lane = lax.broadcasted_iota(jnp.int32, logits.shape, 1)
    logits = jnp.where(lane < N_EXPERTS, logits, -jnp.inf)
    m1 = jnp.max(logits, axis=1, keepdims=True)
    i1 = jnp.min(jnp.where(logits == m1, lane, LANES), axis=1, keepdims=True)
    rest = jnp.where(lane == i1, -jnp.inf, logits)
    m2 = jnp.max(rest, axis=1, keepdims=True)
    i2 = jnp.min(jnp.where(rest == m2, lane, LANES), axis=1, keepdims=True)
    e = jnp.exp(m2 - m1)
    g1 = 1.0 / (1.0 + e)
    gate_ref[...] = jnp.where(lane == 0, g1, jnp.where(lane == 1, e * g1, 0.0))

    onehot = jnp.where(lane == i1, 1.0, jnp.where(lane == i2, 1.0, 0.0))
    before = _dot(tri_ref[...], onehot.astype(BF16)) + carry_ref[...]
    r1 = jnp.sum(jnp.where(lane == i1, before, 0.0), axis=1, keepdims=True).astype(jnp.int32)
    r2 = jnp.sum(jnp.where(lane == i2, before, 0.0), axis=1, keepdims=True).astype(jnp.int32)
    idx_ref[...] = jnp.where(lane == 0, i1, jnp.where(lane == 1, i2,
                             jnp.where(lane == 2, r1, jnp.where(lane == 3, r2, 0))))
    total = before[tm - 1:tm, :] + onehot[tm - 1:tm, :]
    carry_ref[...] = total
    count_ref[...] = total


def _router(h, g, wr_split, tm=512):
    n, d = h.shape
    t = jnp.arange(tm)
    tri = (t[:, None] > t[None, :]).astype(BF16)
    return pl.pallas_call(
        _router_kernel,
        grid=(n // tm,),
        in_specs=[pl.BlockSpec((tm, d), lambda i: (i, 0)),
                  pl.BlockSpec((1, d), lambda i: (0, 0)),
                  pl.BlockSpec(wr_split.shape, lambda i: (0, 0, 0)),
                  pl.BlockSpec((tm, tm), lambda i: (0, 0))],
        out_specs=[pl.BlockSpec((tm, LANES), lambda i: (i, 0)),
                   pl.BlockSpec((tm, LANES), lambda i: (i, 0)),
                   pl.BlockSpec((1, LANES), lambda i: (0, 0))],
        out_shape=[jax.ShapeDtypeStruct((n, LANES), jnp.int32),
                   jax.ShapeDtypeStruct((n, LANES), F32),
                   jax.ShapeDtypeStruct((1, LANES), F32)],
        scratch_shapes=[pltpu.VMEM((1, LANES), F32)],
        compiler_params=_params(1),
        name="router",
    )(h, g, wr_split, tri)


def _dispatch_kernel(dest_ref, pad_start_ref, pad_len_ref, n_valid_ref, h_ref, g_ref, xs_hbm,
                     buf_ref, zero_ref, sem, zero_sem, *, tm):
    base = pl.program_id(0) * tm * TOP_K

    @pl.when(pl.program_id(0) == 0)
    def _():
        zero_ref[...] = jnp.zeros_like(zero_ref)
        n_blocks = n_valid_ref.shape[0]

        def zero_copy(first_row, rows):
            return pltpu.make_async_copy(zero_ref.at[pl.ds(0, rows)], xs_hbm.at[pl.ds(first_row, rows)],
                                         zero_sem)

        def fill(wait):
            def go(cp):
                cp.wait() if wait else cp.start()

            for e in range(N_EXPERTS):
                pos = pad_start_ref[e]
                left = pad_len_ref[e]
                head = (-pos) & (SUBLANES - 1)
                for j in range(SUBLANES - 1):
                    @pl.when(j < head)
                    def _(pos=pos, j=j):
                        go(zero_copy(pos + j, 1))
                pos = pos + head
                left = left - head
                rows = MOE_ROWS // 2
                while rows >= SUBLANES:
                    take = (left & rows) != 0

                    @pl.when(take)
                    def _(pos=pos, rows=rows):
                        go(zero_copy(pl.multiple_of(pos, SUBLANES), rows))
                    pos = pos + jnp.where(take, rows, 0)
                    rows //= 2

            def blocks(b, carry):
                @pl.when(n_valid_ref[b] == 0)
                def _():
                    go(zero_copy(pl.multiple_of(b * MOE_ROWS, MOE_ROWS), MOE_ROWS))
                return carry
            lax.fori_loop(0, n_blocks, blocks, 0)

        fill(False)
        fill(True)

    buf_ref[...] = _rms(h_ref[...], g_ref[...])

    def row_copy(r, k):
        dst = dest_ref[base + r * TOP_K + k]
        return pltpu.make_async_copy(buf_ref.at[pl.ds(r, 1)], xs_hbm.at[pl.ds(dst, 1)], sem)

    def start(r, carry):
        for k in range(TOP_K):
            row_copy(r, k).start()
        return carry

    lax.fori_loop(0, tm, start, 0, unroll=DMA_UNROLL)
    for _ in range(TOP_K):
        pltpu.make_async_copy(buf_ref, xs_hbm.at[pl.ds(0, tm)], sem).wait()


def _dispatch(dest, pad_start, pad_len, n_valid, h, g, tm=512):
    n, d = h.shape
    n_rows = n_valid.shape[0] * MOE_ROWS
    grid_spec = pltpu.PrefetchScalarGridSpec(
        num_scalar_prefetch=4,
        grid=(n // tm,),
        in_specs=[pl.BlockSpec((tm, d), lambda i, *_: (i, 0)),
                  pl.BlockSpec((1, d), lambda i, *_: (0, 0))],
        out_specs=pl.BlockSpec(memory_space=pl.ANY),
        scratch_shapes=[pltpu.VMEM((tm, d), F32), pltpu.VMEM((MOE_ROWS, d), F32),
                        pltpu.SemaphoreType.DMA, pltpu.SemaphoreType.DMA],
    )
    return pl.pallas_call(
        functools.partial(_dispatch_kernel, tm=tm),
        grid_spec=grid_spec,
        out_shape=jax.ShapeDtypeStruct((n_rows, d), F32),
        compiler_params=_params(1),
        name="moe_dispatch",
    )(dest, pad_start, pad_len, n_valid, h, g)


def _expert_kernel(blk_e_ref, n_valid_ref, xs_ref, wg_ref, wu_ref, wd_ref, ys_ref, act_ref):
    n_valid = n_valid_ref[pl.program_id(0)]

    @pl.when(n_valid > 0)
    def _():
        ys_ref[...] = _swiglu(xs_ref[...].astype(BF16), wg_ref, wu_ref, wd_ref, act_ref)

    @pl.when(n_valid == 0)
    def _():
        ys_ref[...] = jnp.zeros_like(ys_ref)


def _experts(blk_e, n_valid, xs, wg, wu, wd):
    n_rows, d = xs.shape
    ff = wg.shape[2]
    grid_spec = pltpu.PrefetchScalarGridSpec(
        num_scalar_prefetch=2,
        grid=(n_rows // MOE_ROWS,),
        in_specs=[pl.BlockSpec((MOE_ROWS, d), lambda i, be, nv: (i, 0)),
                  pl.BlockSpec((None, d, ff), lambda i, be, nv: (be[i], 0, 0)),
                  pl.BlockSpec((None, d, ff), lambda i, be, nv: (be[i], 0, 0)),
                  pl.BlockSpec((None, ff, d), lambda i, be, nv: (be[i], 0, 0))],
        out_specs=pl.BlockSpec((MOE_ROWS, d), lambda i, be, nv: (i, 0)),
        scratch_shapes=[pltpu.VMEM((MOE_ROWS, ff), BF16)],
    )
    return pl.pallas_call(
        _expert_kernel,
        grid_spec=grid_spec,
        out_shape=jax.ShapeDtypeStruct((n_rows, d), F32),
        compiler_params=_params(1),
        name="moe_experts",
    )(blk_e, n_valid, xs, wg, wu, wd)


def _combine_kernel(dest_ref, ys_hbm, gate_ref, g_ref, h_ref, o_ref, buf_ref, sem, *, tm):
    i = pl.program_id(0)

    def row_copy(tile, slot, r, k):
        src = dest_ref[(tile * tm + r) * TOP_K + k]
        return pltpu.make_async_copy(ys_hbm.at[pl.ds(src, 1)], buf_ref.at[slot, k, pl.ds(r, 1)],
                                     sem.at[slot])

    def fetch(tile, slot):
        def body(r, carry):
            for k in range(TOP_K):
                row_copy(tile, slot, r, k).start()
            return carry
        lax.fori_loop(0, tm, body, 0, unroll=DMA_UNROLL)

    @pl.when(i == 0)
    def _():
        fetch(0, 0)

    @pl.when(i + 1 < pl.num_programs(0))
    def _():
        fetch(i + 1, (i + 1) % 2)

    slot = i % 2

    for k in range(TOP_K):
        pltpu.make_async_copy(ys_hbm.at[pl.ds(0, tm)], buf_ref.at[slot, k], sem.at[slot]).wait()
    gates = gate_ref[...]
    f = buf_ref[slot, 0] * gates[:, 0:1] + buf_ref[slot, 1] * gates[:, 1:2]
    o_ref[...] = h_ref[...] + _rms(f, g_ref[...])


def _combine(dest, ys, gates, g, h, tm=512):
    n, d = h.shape
    grid_spec = pltpu.PrefetchScalarGridSpec(
        num_scalar_prefetch=1,
        grid=(n // tm,),
        in_specs=[pl.BlockSpec(memory_space=pl.ANY),
                  pl.BlockSpec((tm, LANES), lambda i, dst: (i, 0)),
                  pl.BlockSpec((1, d), lambda i, dst: (0, 0)),
                  pl.BlockSpec((tm, d), lambda i, dst: (i, 0))],
        out_specs=pl.BlockSpec((tm, d), lambda i, dst: (i, 0)),
        scratch_shapes=[pltpu.VMEM((2, TOP_K, tm, d), F32), pltpu.SemaphoreType.DMA((2,))],
    )
    return pl.pallas_call(
        functools.partial(_combine_kernel, tm=tm),
        grid_spec=grid_spec,
        out_shape=jax.ShapeDtypeStruct((n, d), F32),
        compiler_params=_params(1),
        name="moe_combine",
    )(dest, ys, gates, g, h)


def _route_plan(idx, counts_f, n_tok):
    counts = counts_f[0, :N_EXPERTS].astype(jnp.int32)
    padded = ((counts + MOE_ROWS - 1) // MOE_ROWS) * MOE_ROWS
    pend = jnp.cumsum(padded)
    pstart = pend - padded
    expert = idx[:, 0:TOP_K]
    rank = idx[:, TOP_K:2 * TOP_K]
    offset = jnp.zeros_like(expert)
    for e in range(N_EXPERTS):
        offset = jnp.where(expert == e, pstart[e], offset)
    dest = (offset + rank).reshape(-1)
    n_blocks = (n_tok * TOP_K) // MOE_ROWS + N_EXPERTS
    blk_start = jnp.arange(n_blocks, dtype=jnp.int32) * MOE_ROWS
    blk_e = jnp.minimum(jnp.sum(blk_start[:, None] >= pend[None, :], axis=1), N_EXPERTS - 1).astype(jnp.int32)
    n_valid = jnp.clip(pstart[blk_e] + counts[blk_e] - blk_start, 0, MOE_ROWS).astype(jnp.int32)
    return dest, pstart + counts, padded - counts, blk_e, n_valid


def _moe(h, g_pre, router_w, wg, wu, wd, g_post):
    n, d = h.shape
    wr = jnp.pad(router_w, ((0, 0), (0, LANES - N_EXPERTS)))
    wr_hi = wr.astype(BF16)
    wr_mid = (wr - wr_hi.astype(F32)).astype(BF16)
    idx, gates, counts = _router(h, g_pre, jnp.stack([wr_hi, wr_mid]))
    dest, pad_start, pad_len, blk_e, n_valid = _route_plan(idx, counts, n)
    xs = _dispatch(dest, pad_start, pad_len, n_valid, h, g_pre)
    ys = _experts(blk_e, n_valid, xs, wg, wu, wd)
    return _combine(dest, ys, gates, g_post, h)


def kernel(x, mem, mix_norm_pre, mix_norm_post, w_in, b_forget, pool_w, pool_scale, w_out,
           xa_norm_pre, xa_norm_post, mem_norm, xa_wq, xa_wkv, xa_wo,
           ffn_norm_pre, ffn_norm_post, dense_w_gate, dense_w_up, dense_w_down,
           router_w, moe_w_gate, moe_w_up, moe_w_down):
    bsz, seq, d = x.shape
    m_len = mem.shape[1]
    depth = w_in.shape[0]
    n = bsz * seq
    h = x.reshape(n, d)
    mem2 = mem.reshape(bsz * m_len, d)
    row = lambda v: v.reshape(1, -1)

    idx = jnp.arange(SB_SUB)
    tri = (idx[:, None] >= idx[None, :]).astype(BF16)

    for li in range(depth):
        w = w_in[li]
        flog_w = jnp.pad(w[:, QKV_DIM:QKV_DIM + N_FOX], ((0, 0), (0, LANES - N_FOX)))
        w_cat = jnp.concatenate([w[:, :QKV_DIM], w[:, QKV_DIM + N_FOX:], flog_w], axis=1).astype(BF16)
        qkv, u, flog = _norm_matmul(
            h, row(mix_norm_pre[li]), w_cat,
            splits=[(0, QKV_DIM), (QKV_DIM, POOL_DIM), (QKV_DIM + POOL_DIM, LANES)],
            dtypes=[BF16, F32, F32], tm=512)
        qkv = qkv.reshape(bsz, seq, QKV_DIM)

        b_pad = jnp.pad(b_forget[li], (0, LANES - N_FOX)).reshape(1, LANES)
        c = _logf_cumsum(flog.reshape(bsz, seq, LANES), b_pad)
        c_rows = c[:, :, :N_FOX].transpose(0, 2, 1).reshape(bsz, N_PAIRS, 2, seq)

        sb = _sb_attention(qkv, tri).reshape(n, GROUP_DIM)
        fox = _fox_attention(qkv, c, c_rows).reshape(n, GROUP_DIM)

        pool_bd = jax.scipy.linalg.block_diag(*[pool_w[li, gi] for gi in range(len(POOL_WINDOWS))])
        h = _mix_out(sb, fox, u, pool_bd.astype(BF16), row(pool_scale[li]), w_out[li].astype(BF16),
                     row(mix_norm_post[li]), h, seq)

        k_mem, v_mem = _norm_matmul(mem2, row(mem_norm[li]), xa_wkv[li].astype(BF16),
                                    splits=[(0, d), (d, d)], dtypes=[BF16, BF16], tm=m_len)
        h = _xattn(h, row(xa_norm_pre[li]), xa_wq[li].astype(BF16),
                   k_mem.reshape(bsz, m_len, d), v_mem.reshape(bsz, m_len, d),
                   xa_wo[li].astype(BF16), row(xa_norm_post[li]), seq)

        j = li // 2
        if li % 2 == 0:
            h = _ffn(h, row(ffn_norm_pre[li]), dense_w_gate[j].astype(BF16), dense_w_up[j].astype(BF16),
                     dense_w_down[j].astype(BF16), row(ffn_norm_post[li]))
        else:
            h = _moe(h, row(ffn_norm_pre[li]), router_w[j], moe_w_gate[j].astype(BF16),
                     moe_w_up[j].astype(BF16), moe_w_down[j].astype(BF16), row(ffn_norm_post[li]))
    return h.reshape(bsz, seq, d)
```

```python
import functools

import jax
import jax.numpy as jnp
from jax import lax
from jax.experimental import pallas as pl
from jax.experimental.pallas import tpu as pltpu

F32 = jnp.float32
BF16 = jnp.bfloat16
EPS = 1e-6

HEAD_DIM = 64
LANES = 128
SUBLANES = 8
N_PAIRS = 3
GROUP_DIM = N_PAIRS * LANES
QKV_DIM = 6 * GROUP_DIM
POOL_DIM = 256
POOL_WINDOWS = (2, 4, 8, 16)
POOL_HALO = 16
N_FOX = 6
XA_HEADS = 4
N_EXPERTS = 8
TOP_K = 2
MOE_ROWS = 512
FF_CHUNK = 256
DMA_UNROLL = 8
SB_SUB = 256
SB_CHAINS = 4
FOX_SUB = 256
FOX_CHAINS = 8
FOX_SAFE_BOUND = 40.0
EXP_CUTOFF = 105.0
VMEM_LIMIT = 56 * 1024 * 1024

_NT = (((1,), (1,)), ((), ()))


def _params(n_axes):
    return pltpu.CompilerParams(dimension_semantics=("arbitrary",) * n_axes,
                                vmem_limit_bytes=VMEM_LIMIT)


def _rms(x, g):
    return x * lax.rsqrt(jnp.mean(x * x, axis=-1, keepdims=True) + EPS) * g


def _dot(a, b):
    return jnp.dot(a, b, preferred_element_type=F32)


def _split3(x):
    hi = x.astype(BF16)
    r1 = x - hi.astype(F32)
    mid = r1.astype(BF16)
    lo = (r1 - mid.astype(F32)).astype(BF16)
    return hi, mid, lo


def _norm_matmul_kernel(x_ref, g_ref, w_ref, *out_refs, splits):
    yb = _rms(x_ref[...], g_ref[...]).astype(BF16)
    for o_ref, (c0, width) in zip(out_refs, splits):
        for c in range(0, width, 256):
            cw = min(256, width - c)
            o_ref[:, c:c + cw] = _dot(yb, w_ref[:, c0 + c:c0 + c + cw]).astype(o_ref.dtype)


def _norm_matmul(x, g, w, splits, dtypes, tm):
    n, d = x.shape
    kern = functools.partial(_norm_matmul_kernel, splits=tuple(splits))
    return pl.pallas_call(
        kern,
        grid=(n // tm,),
        in_specs=[pl.BlockSpec((tm, d), lambda i: (i, 0)),
                  pl.BlockSpec((1, d), lambda i: (0, 0)),
                  pl.BlockSpec(w.shape, lambda i: (0, 0))],
        out_specs=[pl.BlockSpec((tm, wd), lambda i: (i, 0)) for (_, wd) in splits],
        out_shape=[jax.ShapeDtypeStruct((n, wd), dt) for (_, wd), dt in zip(splits, dtypes)],
        compiler_params=_params(1),
        name="norm_matmul",
    )(x, g, w)


def _logf_cumsum_kernel(fl_ref, b_ref, c_ref, carry_ref, *, tc):
    @pl.when(pl.program_id(1) == 0)
    def _():
        carry_ref[...] = jnp.zeros_like(carry_ref)

    x = fl_ref[...] + b_ref[...]
    ls = jnp.minimum(x, 0.0) - jnp.log(1.0 + jnp.exp(-jnp.abs(x)))
    row = lax.broadcasted_iota(jnp.int32, (tc, tc), 0)
    col = lax.broadcasted_iota(jnp.int32, (tc, tc), 1)
    tri = jnp.where(row >= col, 1.0, 0.0).astype(BF16)
    hi, mid, lo = _split3(ls)
    c = _dot(tri, hi) + _dot(tri, mid) + _dot(tri, lo) + carry_ref[...]
    c_ref[...] = c
    carry_ref[...] = c[tc - 1:tc, :]


def _logf_cumsum(fl, b_pad, tc=1024):
    bsz, s, _ = fl.shape
    return pl.pallas_call(
        functools.partial(_logf_cumsum_kernel, tc=tc),
        grid=(bsz, s // tc),
        in_specs=[pl.BlockSpec((None, tc, LANES), lambda b, j: (b, j, 0)),
                  pl.BlockSpec((1, LANES), lambda b, j: (0, 0))],
        out_specs=pl.BlockSpec((None, tc, LANES), lambda b, j: (b, j, 0)),
        out_shape=jax.ShapeDtypeStruct(fl.shape, F32),
        scratch_shapes=[pltpu.VMEM((1, LANES), F32)],
        compiler_params=_params(2),
        name="logf_cumsum",
    )(fl, b_pad)


def _sb_kernel(q_ref, k_ref, v_ref, tri_ref, o_ref, acc_ref, run_ref):
    sub, n_chain = SB_SUB, SB_CHAINS
    blk0 = pl.program_id(2) * n_chain
    lane = lax.broadcasted_iota(jnp.int32, (1, LANES), 1)
    in_head = (lane < HEAD_DIM, lane >= HEAD_DIM)
    q_all = q_ref[...] * jnp.asarray(HEAD_DIM ** -0.5, BF16)
    q_chain = [[jnp.where(in_head[hd], q_all[r * sub:(r + 1) * sub, :], jnp.zeros((), BF16))
                for r in range(n_chain)] for hd in range(2)]
    row = lax.broadcasted_iota(jnp.int32, (sub, sub), 0)
    col = lax.broadcasted_iota(jnp.int32, (sub, sub), 1)
    acc_ref[...] = jnp.zeros_like(acc_ref)
    run_ref[...] = jnp.zeros_like(run_ref)

    def tile(hd, r, d, diag):
        kb = blk0 + r - d
        start = pl.multiple_of(jnp.maximum(kb, 0) * sub, sub)
        k = k_ref[pl.ds(start, sub), :]
        v = jnp.where(in_head[hd], v_ref[pl.ds(start, sub), :], jnp.zeros((), BF16))
        z = lax.dot_general(q_chain[hd][r], k, _NT, preferred_element_type=F32)
        sp = jnp.maximum(z, 0.0) + jnp.log(1.0 + jnp.exp(-jnp.abs(z)))
        if diag:
            sp = jnp.where(col < row, sp, 0.0)
        suffix = _dot(sp.astype(BF16), tri_ref[...])
        run = run_ref[hd, r]
        w = jnp.exp(z - suffix - jnp.where(kb >= 0, run, 1e30))
        if diag:
            w = jnp.where(col < row, w, 0.0)
        acc_ref[hd, r] += _dot(w.astype(BF16), v)
        run_ref[hd, r] = run + suffix[:, 0:1]

    def more(d):
        go = jnp.bool_(False)
        for hd in range(2):
            for r in range(n_chain):
                unfinished = jnp.min(run_ref[hd, r]) < EXP_CUTOFF
                go = jnp.logical_or(go, jnp.logical_and(blk0 + r - d >= 0, unfinished))
        return go.astype(jnp.int32)

    def step(d, diag):
        for r in range(n_chain):
            for hd in range(2):
                tile(hd, r, d, diag)

    def body(carry):
        d, _ = carry
        step(d, False)
        return d + 1, more(d + 1)

    step(0, True)
    lax.while_loop(lambda carry: carry[1] > 0, body, (jnp.int32(1), more(1)))
    for r in range(n_chain):
        o_ref[r * sub:(r + 1) * sub, :] = (acc_ref[0, r] + acc_ref[1, r]).astype(o_ref.dtype)


def _sb_attention(qkv, tri):
    bsz, s, _ = qkv.shape
    tq = SB_SUB * SB_CHAINS
    return pl.pallas_call(
        _sb_kernel,
        grid=(bsz, N_PAIRS, s // tq),
        in_specs=[pl.BlockSpec((None, tq, LANES), lambda b, p, i: (b, i, p)),
                  pl.BlockSpec((None, s, LANES), lambda b, p, i: (b, 0, N_PAIRS + p)),
                  pl.BlockSpec((None, s, LANES), lambda b, p, i: (b, 0, 2 * N_PAIRS + p)),
                  pl.BlockSpec((SB_SUB, SB_SUB), lambda b, p, i: (0, 0))],
        out_specs=pl.BlockSpec((None, tq, LANES), lambda b, p, i: (b, i, p)),
        out_shape=jax.ShapeDtypeStruct((bsz, s, GROUP_DIM), BF16),
        scratch_shapes=[pltpu.VMEM((2, SB_CHAINS, SB_SUB, LANES), F32),
                        pltpu.VMEM((2, SB_CHAINS, SB_SUB, 1), F32)],
        compiler_params=_params(3),
        name="sb_attention",
    )(qkv, qkv, qkv, tri)


def _fox_kernel(q_ref, k_ref, v_ref, ccol_ref, crow_ref, cs_ref, o_ref, acc_ref, z_ref, kmax_ref,
                *, seq):
    sub, n_chain = FOX_SUB, FOX_CHAINS
    pair = pl.program_id(1)
    i = pl.program_id(2)
    lane = lax.broadcasted_iota(jnp.int32, (1, LANES), 1)
    in_head = (lane < HEAD_DIM, lane >= HEAD_DIM)
    one_lane = (HEAD_DIM, 0)

    @pl.when(i == 0)
    def _():
        def chunk(t, mx):
            k = k_ref[pl.ds(pl.multiple_of(t * sub, sub), sub), :].astype(F32)
            sq = k * k
            return tuple(jnp.maximum(mx[hd], jnp.max(jnp.sum(jnp.where(in_head[hd], sq, 0.0), axis=1,
                                                              keepdims=True))) for hd in range(2))
        mx = lax.fori_loop(0, seq // sub, chunk, (jnp.float32(0.0), jnp.float32(0.0)))
        kmax_ref[0] = mx[0]
        kmax_ref[1] = mx[1]

    q_all = q_ref[...] * jnp.asarray(HEAD_DIM ** -0.5, BF16)
    row = lax.broadcasted_iota(jnp.int32, (sub, sub), 0)
    col = lax.broadcasted_iota(jnp.int32, (sub, sub), 1)
    blk0 = i * n_chain
    acc_ref[...] = jnp.zeros_like(acc_ref)

    for hd in range(2):
        q_head = jnp.where(in_head[hd], q_all, jnp.zeros((), BF16))
        q32 = q_head.astype(F32)
        zb = jnp.sqrt(jnp.sum(q32 * q32, axis=1, keepdims=True) * kmax_ref[hd]) * 1.001
        zb_max = jnp.max(zb)
        cutoff = EXP_CUTOFF + 2.0 * zb_max
        c_base = cs_ref[hd, blk0 * sub]
        c_col = jnp.sum(jnp.where(lane == 2 * pair + hd, ccol_ref[...], 0.0), axis=1, keepdims=True)
        a_all = (c_col - c_base) - zb
        v_one = jnp.where(lane == one_lane[hd], 1.0, 0.0).astype(BF16)
        q_chain = [q_head[r * sub:(r + 1) * sub, :] for r in range(n_chain)]
        a_chain = [a_all[r * sub:(r + 1) * sub, :] for r in range(n_chain)]

        def tile_inputs(kb, hd=hd, c_base=c_base, v_one=v_one):
            start = pl.multiple_of(jnp.maximum(kb, 0) * sub, sub)
            k = k_ref[pl.ds(start, sub), :]
            v = jnp.where(in_head[hd], v_ref[pl.ds(start, sub), :], v_one)
            c_row = crow_ref[hd:hd + 1, pl.ds(start, sub)] - c_base
            return k, v, jnp.where(kb >= 0, c_row, 1e30)

        def more(d, cutoffs, hd=hd):
            go = jnp.bool_(False)
            for r in range(n_chain):
                kb = blk0 + r - d
                first = cs_ref[hd, (blk0 + r) * sub]
                last = cs_ref[hd, jnp.maximum(kb, 0) * sub + sub - 1]
                go = jnp.logical_or(go, jnp.logical_and(kb >= 0, first - last > -cutoffs[r]))
            return go

        def scores(r, d, q_chain=q_chain):
            start = pl.multiple_of(jnp.maximum(blk0 + r - d, 0) * sub, sub)
            return lax.dot_general(q_chain[r], k_ref[pl.ds(start, sub), :], _NT,
                                   preferred_element_type=F32)

        def fast_tile(r, d, z, diag, hd=hd, a_chain=a_chain, tile_inputs=tile_inputs):
            _, v, c_row = tile_inputs(blk0 + r - d)
            e = z + a_chain[r] - c_row
            if diag:
                e = jnp.where(col <= row, e, -1e30)
            acc_ref[hd, r] += _dot(jnp.exp(e).astype(BF16), v)
            return jnp.min(jnp.max(e, axis=1, keepdims=True)) if diag else None

        def slow_tile(r, d, m_old, diag, hd=hd, q_chain=q_chain, tile_inputs=tile_inputs):
            k, v, c_row = tile_inputs(blk0 + r - d)
            s = lax.dot_general(q_chain[r], k, _NT, preferred_element_type=F32) - c_row
            if diag:
                s = jnp.where(col <= row, s, -1e30)
            m_new = jnp.maximum(m_old, jnp.max(s, axis=1, keepdims=True))
            acc_ref[hd, r] = (acc_ref[hd, r] * jnp.exp(m_old - m_new)
                              + _dot(jnp.exp(s - m_new).astype(BF16), v))
            return m_new

        @pl.when(zb_max <= FOX_SAFE_BOUND)
        def _(fast_tile=fast_tile, scores=scores, more=more):
            cutoffs = []
            for r in range(n_chain):
                cutoffs.append(EXP_CUTOFF - fast_tile(r, 0, scores(r, 0), True))
                z_ref[r] = scores(r, 1)

            def body(d):
                for r in range(n_chain):
                    z = z_ref[r]
                    z_ref[r] = scores(r, d + 1)
                    fast_tile(r, d, z, False)
                return d + 1
            lax.while_loop(lambda d: more(d, cutoffs), body, jnp.int32(1))

        @pl.when(zb_max > FOX_SAFE_BOUND)
        def _(slow_tile=slow_tile, more=more):
            m0 = jnp.full((sub, 1), -1e30, F32)
            ms = tuple(slow_tile(r, 0, m0, True) for r in range(n_chain))

            def body(carry):
                d, ms = carry
                return d + 1, tuple(slow_tile(r, d, ms[r], False) for r in range(n_chain))
            lax.while_loop(lambda carry: more(carry[0], [cutoff] * n_chain), body, (jnp.int32(1), ms))

    for r in range(n_chain):
        acc = (acc_ref[0, r], acc_ref[1, r])
        total = [jnp.sum(jnp.where(lane == one_lane[hd], acc[hd], 0.0), axis=1, keepdims=True)
                 for hd in range(2)]
        out = jnp.where(in_head[0], acc[0] / total[0], acc[1] / total[1])
        o_ref[r * sub:(r + 1) * sub, :] = out.astype(o_ref.dtype)


def _fox_attention(qkv, c_cols, c_rows):
    bsz, s, _ = qkv.shape
    tq = FOX_SUB * FOX_CHAINS
    return pl.pallas_call(
        functools.partial(_fox_kernel, seq=s),
        grid=(bsz, N_PAIRS, s // tq),
        in_specs=[pl.BlockSpec((None, tq, LANES), lambda b, p, i: (b, i, 3 * N_PAIRS + p)),
                  pl.BlockSpec((None, s, LANES), lambda b, p, i: (b, 0, 4 * N_PAIRS + p)),
                  pl.BlockSpec((None, s, LANES), lambda b, p, i: (b, 0, 5 * N_PAIRS + p)),
                  pl.BlockSpec((None, tq, LANES), lambda b, p, i: (b, i, 0)),
                  pl.BlockSpec((None, None, 2, s), lambda b, p, i: (b, p, 0, 0)),
                  pl.BlockSpec((None, None, 2, s), lambda b, p, i: (b, p, 0, 0),
                               memory_space=pltpu.SMEM)],
        out_specs=pl.BlockSpec((None, tq, LANES), lambda b, p, i: (b, i, p)),
        out_shape=jax.ShapeDtypeStruct((bsz, s, GROUP_DIM), BF16),
        scratch_shapes=[pltpu.VMEM((2, FOX_CHAINS, FOX_SUB, LANES), F32),
                        pltpu.VMEM((FOX_CHAINS, FOX_SUB, FOX_SUB), F32), pltpu.SMEM((2,), F32)],
        compiler_params=_params(3),
        name="fox_attention",
    )(qkv, qkv, qkv, c_cols, c_rows, c_rows)


def _mix_out_kernel(sb_ref, fox_ref, u_ref, halo_ref, pw_ref, ps_ref, wo_ref, g_ref, h_ref, o_ref,
                    *, tm, seq):
    pos0 = (pl.program_id(0) * tm) % seq
    u = u_ref[...]
    halo = jnp.where(pos0 == 0, 0.0, halo_ref[...])
    x = jnp.concatenate([halo, u], axis=0)
    s2 = x + pltpu.roll(x, 1, 0)
    s4 = s2 + pltpu.roll(s2, 2, 0)
    s8 = s4 + pltpu.roll(s4, 4, 0)
    s16 = s8 + pltpu.roll(s8, 8, 0)
    lane = lax.broadcasted_iota(jnp.int32, (1, POOL_DIM), 1)
    grp = POOL_DIM // len(POOL_WINDOWS)
    wsum = jnp.where(lane < grp, s2, jnp.where(lane < 2 * grp, s4, jnp.where(lane < 3 * grp, s8, s16)))
    win = jnp.where(lane < grp, 2, jnp.where(lane < 2 * grp, 4, jnp.where(lane < 3 * grp, 8, 16)))
    pos = pos0 + lax.broadcasted_iota(jnp.int32, (tm, 1), 0)
    count = jnp.minimum(pos + 1, win).astype(F32)
    r = wsum[POOL_HALO:, :] / count - u
    pool = _dot(r.astype(BF16), pw_ref[...]) * ps_ref[...]
    gd = GROUP_DIM
    a = (_dot(sb_ref[...], wo_ref[0:gd, :]) + _dot(fox_ref[...], wo_ref[gd:2 * gd, :])
         + _dot(pool.astype(BF16), wo_ref[2 * gd:, :]))
    o_ref[...] = h_ref[...] + _rms(a, g_ref[...])


def _mix_out(sb, fox, u, pool_w_bd, pool_scale, w_out, g, h, seq, tm=512):
    n, d = h.shape
    hb = tm // POOL_HALO
    return pl.pallas_call(
        functools.partial(_mix_out_kernel, tm=tm, seq=seq),
        grid=(n // tm,),
        in_specs=[pl.BlockSpec((tm, GROUP_DIM), lambda i: (i, 0)),
                  pl.BlockSpec((tm, GROUP_DIM), lambda i: (i, 0)),
                  pl.BlockSpec((tm, POOL_DIM), lambda i: (i, 0)),
                  pl.BlockSpec((POOL_HALO, POOL_DIM), lambda i: (jnp.maximum(i * hb - 1, 0), 0)),
                  pl.BlockSpec((POOL_DIM, POOL_DIM), lambda i: (0, 0)),
                  pl.BlockSpec((1, POOL_DIM), lambda i: (0, 0)),
                  pl.BlockSpec(w_out.shape, lambda i: (0, 0)),
                  pl.BlockSpec((1, d), lambda i: (0, 0)),
                  pl.BlockSpec((tm, d), lambda i: (i, 0))],
        out_specs=pl.BlockSpec((tm, d), lambda i: (i, 0)),
        out_shape=jax.ShapeDtypeStruct((n, d), F32),
        compiler_params=_params(1),
        name="mix_out",
    )(sb, fox, u, u, pool_w_bd, pool_scale, w_out, g, h)


def _xattn_kernel(h_ref, gpre_ref, wq_ref, k_ref, v_ref, wo_ref, gpost_ref, o_ref):
    h = h_ref[...]
    d = h.shape[-1]
    hd = d // XA_HEADS
    hn = _rms(h, gpre_ref[...]).astype(BF16)
    q = (_dot(hn, wq_ref[...]) * (hd ** -0.5)).astype(BF16)
    outs = []
    for a in range(XA_HEADS):
        sl = slice(a * hd, (a + 1) * hd)
        s = lax.dot_general(q[:, sl], k_ref[:, sl], _NT, preferred_element_type=F32)
        p = jnp.exp(s - jnp.max(s, axis=1, keepdims=True))
        p = p / jnp.sum(p, axis=1, keepdims=True)
        outs.append(_dot(p.astype(BF16), v_ref[:, sl]).astype(BF16))
    c = _dot(jnp.concatenate(outs, axis=1), wo_ref[...])
    o_ref[...] = h + _rms(c, gpost_ref[...])


def _xattn(h, g_pre, wq, k_mem, v_mem, wo, g_post, seq, tm=512):
    n, d = h.shape
    m_len = k_mem.shape[1]
    per_seq = seq // tm
    return pl.pallas_call(
        _xattn_kernel,
        grid=(n // tm,),
        in_specs=[pl.BlockSpec((tm, d), lambda i: (i, 0)),
                  pl.BlockSpec((1, d), lambda i: (0, 0)),
                  pl.BlockSpec((d, d), lambda i: (0, 0)),
                  pl.BlockSpec((None, m_len, d), lambda i: (i // per_seq, 0, 0)),
                  pl.BlockSpec((None, m_len, d), lambda i: (i // per_seq, 0, 0)),
                  pl.BlockSpec((d, d), lambda i: (0, 0)),
                  pl.BlockSpec((1, d), lambda i: (0, 0))],
        out_specs=pl.BlockSpec((tm, d), lambda i: (i, 0)),
        out_shape=jax.ShapeDtypeStruct((n, d), F32),
        compiler_params=_params(1),
        name="xattn",
    )(h, g_pre, wq, k_mem, v_mem, wo, g_post)


def _silu(x):
    return x / (1.0 + jnp.exp(-x))


def _swiglu(x, wg_ref, wu_ref, wd_ref, act_ref):
    ff = wg_ref.shape[1]
    for c in range(0, ff, FF_CHUNK):
        gate = _dot(x, wg_ref[:, c:c + FF_CHUNK])
        up = _dot(x, wu_ref[:, c:c + FF_CHUNK])
        act_ref[:, c:c + FF_CHUNK] = (_silu(gate) * up).astype(BF16)
    return _dot(act_ref[...], wd_ref[...])


def _ffn_kernel(h_ref, gpre_ref, wg_ref, wu_ref, wd_ref, gpost_ref, o_ref, act_ref):
    h = h_ref[...]
    f = _swiglu(_rms(h, gpre_ref[...]).astype(BF16), wg_ref, wu_ref, wd_ref, act_ref)
    o_ref[...] = h + _rms(f, gpost_ref[...])


def _ffn(h, g_pre, wg, wu, wd, g_post, tm=512):
    n, d = h.shape
    ff = wg.shape[1]
    resident = pl.Buffered(1)
    return pl.pallas_call(
        _ffn_kernel,
        grid=(n // tm,),
        in_specs=[pl.BlockSpec((tm, d), lambda i: (i, 0)),
                  pl.BlockSpec((1, d), lambda i: (0, 0)),
                  pl.BlockSpec((d, ff), lambda i: (0, 0), pipeline_mode=resident),
                  pl.BlockSpec((d, ff), lambda i: (0, 0), pipeline_mode=resident),
                  pl.BlockSpec((ff, d), lambda i: (0, 0), pipeline_mode=resident),
                  pl.BlockSpec((1, d), lambda i: (0, 0))],
        out_specs=pl.BlockSpec((tm, d), lambda i: (i, 0)),
        out_shape=jax.ShapeDtypeStruct((n, d), F32),
        scratch_shapes=[pltpu.VMEM((tm, ff), BF16)],
        compiler_params=_params(1),
        name="ffn_dense",
    )(h, g_pre, wg, wu, wd, g_post)


def _router_kernel(h_ref, g_ref, wr_ref, tri_ref, idx_ref, gate_ref, count_ref, carry_ref):
    @pl.when(pl.program_id(0) == 0)
    def _():
        carry_ref[...] = jnp.zeros_like(carry_ref)

    hn = _rms(h_ref[...], g_ref[...])
    x_hi, x_mid, _ = _split3(hn)
    w_hi, w_mid = wr_ref[0], wr_ref[1]
    logits = _dot(x_hi, w_hi) + _dot(x_hi, w_mid) + _dot(x_mid, w_hi)
    tm = logits.shape[0]
    lane = lax.broadcasted_iota(jnp.int32, logits.shape, 1)
    logits = jnp.where(lane < N_EXPERTS, logits, -jnp.inf)
    m1 = jnp.max(logits, axis=1, keepdims=True)
    i1 = jnp.min(jnp.where(logits == m1, lane, LANES), axis=1, keepdims=True)
    rest = jnp.where(lane == i1, -jnp.inf, logits)
    m2 = jnp.max(rest, axis=1, keepdims=True)
    i2 = jnp.min(jnp.where(rest == m2, lane, LANES), axis=1, keepdims=True)
    e = jnp.exp(m2 - m1)
    g1 = 1.0 / (1.0 + e)
    gate_ref[...] = jnp.where(lane == 0, g1, jnp.where(lane == 1, e * g1, 0.0))

    onehot = jnp.where(lane == i1, 1.0, jnp.where(lane == i2, 1.0, 0.0))
    before = _dot(tri_ref[...], onehot.astype(BF16)) + carry_ref[...]
    r1 = jnp.sum(jnp.where(lane == i1, before, 0.0), axis=1, keepdims=True).astype(jnp.int32)
    r2 = jnp.sum(jnp.where(lane == i2, before, 0.0), axis=1, keepdims=True).astype(jnp.int32)
    idx_ref[...] = jnp.where(lane == 0, i1, jnp.where(lane == 1, i2,
                             jnp.where(lane == 2, r1, jnp.where(lane == 3, r2, 0))))
    total = before[tm - 1:tm, :] + onehot[tm - 1:tm, :]
    carry_ref[...] = total
    count_ref[...] = total


def _router(h, g, wr_split, tm=512):
    n, d = h.shape
    t = jnp.arange(tm)
    tri = (t[:, None] > t[None, :]).astype(BF16)
    return pl.pallas_call(
        _router_kernel,
        grid=(n // tm,),
        in_specs=[pl.BlockSpec((tm, d), lambda i: (i, 0)),
                  pl.BlockSpec((1, d), lambda i: (0, 0)),
                  pl.BlockSpec(wr_split.shape, lambda i: (0, 0, 0)),
                  pl.BlockSpec((tm, tm), lambda i: (0, 0))],
        out_specs=[pl.BlockSpec((tm, LANES), lambda i: (i, 0)),
                   pl.BlockSpec((tm, LANES), lambda i: (i, 0)),
                   pl.BlockSpec((1, LANES), lambda i: (0, 0))],
        out_shape=[jax.ShapeDtypeStruct((n, LANES), jnp.int32),
                   jax.ShapeDtypeStruct((n, LANES), F32),
                   jax.ShapeDtypeStruct((1, LANES), F32)],
        scratch_shapes=[pltpu.VMEM((1, LANES), F32)],
        compiler_params=_params(1),
        name="router",
    )(h, g, wr_split, tri)


def _rows_to_tiles(ref, x):
    m = x.shape[0]
    for s in range(SUBLANES):
        ref[pl.ds(s, m, stride=SUBLANES), :] = x[:, s * LANES:(s + 1) * LANES]


def _tiles_to_rows(ref, m):
    return jnp.concatenate([ref[pl.ds(s, m, stride=SUBLANES), :] for s in range(SUBLANES)], axis=1)


def _tile_rows(ref, first_row, rows):
    return ref.at[pl.ds(pl.multiple_of(first_row * SUBLANES, SUBLANES), rows * SUBLANES)]


def _dispatch_kernel(dest_ref, pad_start_ref, pad_len_ref, n_valid_ref, h_ref, g_ref, xs_hbm,
                     buf_ref, zero_ref, sem, zero_sem, *, tm):
    base = pl.program_id(0) * tm * TOP_K

    @pl.when(pl.program_id(0) == 0)
    def _():
        zero_ref[...] = jnp.zeros_like(zero_ref)
        n_blocks = n_valid_ref.shape[0]

        def zero_copy(first_row, rows):
            return pltpu.make_async_copy(_tile_rows(zero_ref, 0, rows), _tile_rows(xs_hbm, first_row, rows),
                                         zero_sem)

        def fill(wait):
            def go(cp):
                cp.wait() if wait else cp.start()

            for e in range(N_EXPERTS):
                pos = pad_start_ref[e]
                left = pad_len_ref[e]
                rows = MOE_ROWS // 2
                while rows >= 1:
                    take = (left & rows) != 0

                    @pl.when(take)
                    def _(pos=pos, rows=rows):
                        go(zero_copy(pos, rows))
                    pos = pos + jnp.where(take, rows, 0)
                    rows //= 2

            def blocks(b, carry):
                @pl.when(n_valid_ref[b] == 0)
                def _():
                    go(zero_copy(b * MOE_ROWS, MOE_ROWS))
                return carry
            lax.fori_loop(0, n_blocks, blocks, 0)

        fill(False)
        fill(True)

    _rows_to_tiles(buf_ref, _rms(h_ref[...], g_ref[...]))

    def start(r, carry):
        for k in range(TOP_K):
            dst = dest_ref[base + r * TOP_K + k]
            pltpu.make_async_copy(_tile_rows(buf_ref, r, 1), _tile_rows(xs_hbm, dst, 1), sem).start()
        return carry

    lax.fori_loop(0, tm, start, 0, unroll=DMA_UNROLL)
    for _ in range(TOP_K):
        pltpu.make_async_copy(buf_ref, _tile_rows(xs_hbm, 0, tm), sem).wait()


def _dispatch(dest, pad_start, pad_len, n_valid, h, g, tm=512):
    n, d = h.shape
    assert d == SUBLANES * LANES
    n_rows = n_valid.shape[0] * MOE_ROWS
    grid_spec = pltpu.PrefetchScalarGridSpec(
        num_scalar_prefetch=4,
        grid=(n // tm,),
        in_specs=[pl.BlockSpec((tm, d), lambda i, *_: (i, 0)),
                  pl.BlockSpec((1, d), lambda i, *_: (0, 0))],
        out_specs=pl.BlockSpec(memory_space=pl.ANY),
        scratch_shapes=[pltpu.VMEM((tm * SUBLANES, LANES), F32), pltpu.VMEM((MOE_ROWS * SUBLANES, LANES), F32),
                        pltpu.SemaphoreType.DMA, pltpu.SemaphoreType.DMA],
    )
    return pl.pallas_call(
        functools.partial(_dispatch_kernel, tm=tm),
        grid_spec=grid_spec,
        out_shape=jax.ShapeDtypeStruct((n_rows * SUBLANES, LANES), F32),
        compiler_params=_params(1),
        name="moe_dispatch",
    )(dest, pad_start, pad_len, n_valid, h, g)


def _expert_kernel(blk_e_ref, n_valid_ref, xs_ref, wg_ref, wu_ref, wd_ref, ys_ref, act_ref):
    n_valid = n_valid_ref[pl.program_id(0)]

    @pl.when(n_valid > 0)
    def _():
        x = _tiles_to_rows(xs_ref, MOE_ROWS).astype(BF16)
        _rows_to_tiles(ys_ref, _swiglu(x, wg_ref, wu_ref, wd_ref, act_ref))

    @pl.when(n_valid == 0)
    def _():
        ys_ref[...] = jnp.zeros_like(ys_ref)


def _experts(blk_e, n_valid, xs, wg, wu, wd):
    d, ff = wg.shape[1], wg.shape[2]
    blk = MOE_ROWS * SUBLANES
    grid_spec = pltpu.PrefetchScalarGridSpec(
        num_scalar_prefetch=2,
        grid=(xs.shape[0] // blk,),
        in_specs=[pl.BlockSpec((blk, LANES), lambda i, be, nv: (i, 0)),
                  pl.BlockSpec((None, d, ff), lambda i, be, nv: (be[i], 0, 0)),
                  pl.BlockSpec((None, d, ff), lambda i, be, nv: (be[i], 0, 0)),
                  pl.BlockSpec((None, ff, d), lambda i, be, nv: (be[i], 0, 0))],
        out_specs=pl.BlockSpec((blk, LANES), lambda i, be, nv: (i, 0)),
        scratch_shapes=[pltpu.VMEM((MOE_ROWS, ff), BF16)],
    )
    return pl.pallas_call(
        _expert_kernel,
        grid_spec=grid_spec,
        out_shape=jax.ShapeDtypeStruct(xs.shape, F32),
        compiler_params=_params(1),
        name="moe_experts",
    )(blk_e, n_valid, xs, wg, wu, wd)


def _combine_kernel(dest_ref, ys_hbm, gate_ref, g_ref, h_ref, o_ref, buf_ref, sem, *, tm):
    i = pl.program_id(0)

    def fetch(tile, slot):
        def body(r, carry):
            for k in range(TOP_K):
                src = dest_ref[(tile * tm + r) * TOP_K + k]
                pltpu.make_async_copy(_tile_rows(ys_hbm, src, 1), _tile_rows(buf_ref.at[slot, k], r, 1),
                                      sem.at[slot]).start()
            return carry
        lax.fori_loop(0, tm, body, 0, unroll=DMA_UNROLL)

    @pl.when(i == 0)
    def _():
        fetch(0, 0)

    @pl.when(i + 1 < pl.num_programs(0))
    def _():
        fetch(i + 1, (i + 1) % 2)

    slot = i % 2

    for k in range(TOP_K):
        pltpu.make_async_copy(_tile_rows(ys_hbm, 0, tm), buf_ref.at[slot, k], sem.at[slot]).wait()
    gates = gate_ref[...]
    f = (_tiles_to_rows(buf_ref.at[slot, 0], tm) * gates[:, 0:1]
         + _tiles_to_rows(buf_ref.at[slot, 1], tm) * gates[:, 1:2])
    o_ref[...] = h_ref[...] + _rms(f, g_ref[...])


def _combine(dest, ys, gates, g, h, tm=512):
    n, d = h.shape
    grid_spec = pltpu.PrefetchScalarGridSpec(
        num_scalar_prefetch=1,
        grid=(n // tm,),
        in_specs=[pl.BlockSpec(memory_space=pl.ANY),
                  pl.BlockSpec((tm, LANES), lambda i, dst: (i, 0)),
                  pl.BlockSpec((1, d), lambda i, dst: (0, 0)),
                  pl.BlockSpec((tm, d), lambda i, dst: (i, 0))],
        out_specs=pl.BlockSpec((tm, d), lambda i, dst: (i, 0)),
        scratch_shapes=[pltpu.VMEM((2, TOP_K, tm * SUBLANES, LANES), F32), pltpu.SemaphoreType.DMA((2,))],
    )
    return pl.pallas_call(
        functools.partial(_combine_kernel, tm=tm),
        grid_spec=grid_spec,
        out_shape=jax.ShapeDtypeStruct((n, d), F32),
        compiler_params=_params(1),
        name="moe_combine",
    )(dest, ys, gates, g, h)


def _route_plan(idx, counts_f, n_tok):
    counts = counts_f[0, :N_EXPERTS].astype(jnp.int32)
    padded = ((counts + MOE_ROWS - 1) // MOE_ROWS) * MOE_ROWS
    pend = jnp.cumsum(padded)
    pstart = pend - padded
    expert = idx[:, 0:TOP_K]
    rank = idx[:, TOP_K:2 * TOP_K]
    offset = jnp.zeros_like(expert)
    for e in range(N_EXPERTS):
        offset = jnp.where(expert == e, pstart[e], offset)
    dest = (offset + rank).reshape(-1)
    n_blocks = (n_tok * TOP_K) // MOE_ROWS + N_EXPERTS
    blk_start = jnp.arange(n_blocks, dtype=jnp.int32) * MOE_ROWS
    blk_e = jnp.minimum(jnp.sum(blk_start[:, None] >= pend[None, :], axis=1), N_EXPERTS - 1).astype(jnp.int32)
    n_valid = jnp.clip(pstart[blk_e] + counts[blk_e] - blk_start, 0, MOE_ROWS).astype(jnp.int32)
    return dest, pstart + counts, padded - counts, blk_e, n_valid


def _moe(h, g_pre, router_w, wg, wu, wd, g_post):
    n, d = h.shape
    wr = jnp.pad(router_w, ((0, 0), (0, LANES - N_EXPERTS)))
    wr_hi = wr.astype(BF16)
    wr_mid = (wr - wr_hi.astype(F32)).astype(BF16)
    idx, gates, counts = _router(h, g_pre, jnp.stack([wr_hi, wr_mid]))
    dest, pad_start, pad_len, blk_e, n_valid = _route_plan(idx, counts, n)
    xs = _dispatch(dest, pad_start, pad_len, n_valid, h, g_pre)
    ys = _experts(blk_e, n_valid, xs, wg, wu, wd)
    return _combine(dest, ys, gates, g_post, h)


def kernel(x, mem, mix_norm_pre, mix_norm_post, w_in, b_forget, pool_w, pool_scale, w_out,
           xa_norm_pre, xa_norm_post, mem_norm, xa_wq, xa_wkv, xa_wo,
           ffn_norm_pre, ffn_norm_post, dense_w_gate, dense_w_up, dense_w_down,
           router_w, moe_w_gate, moe_w_up, moe_w_down):
    bsz, seq, d = x.shape
    m_len = mem.shape[1]
    depth = w_in.shape[0]
    n = bsz * seq
    h = x.reshape(n, d)
    mem2 = mem.reshape(bsz * m_len, d)
    row = lambda v: v.reshape(1, -1)

    idx = jnp.arange(SB_SUB)
    tri = (idx[:, None] >= idx[None, :]).astype(BF16)

    for li in range(depth):
        w = w_in[li]
        flog_w = jnp.pad(w[:, QKV_DIM:QKV_DIM + N_FOX], ((0, 0), (0, LANES - N_FOX)))
        w_cat = jnp.concatenate([w[:, :QKV_DIM], w[:, QKV_DIM + N_FOX:], flog_w], axis=1).astype(BF16)
        qkv, u, flog = _norm_matmul(
            h, row(mix_norm_pre[li]), w_cat,
            splits=[(0, QKV_DIM), (QKV_DIM, POOL_DIM), (QKV_DIM + POOL_DIM, LANES)],
            dtypes=[BF16, F32, F32], tm=512)
        qkv = qkv.reshape(bsz, seq, QKV_DIM)

        b_pad = jnp.pad(b_forget[li], (0, LANES - N_FOX)).reshape(1, LANES)
        c = _logf_cumsum(flog.reshape(bsz, seq, LANES), b_pad)
        c_rows = c[:, :, :N_FOX].transpose(0, 2, 1).reshape(bsz, N_PAIRS, 2, seq)

        sb = _sb_attention(qkv, tri).reshape(n, GROUP_DIM)
        fox = _fox_attention(qkv, c, c_rows).reshape(n, GROUP_DIM)

        pool_bd = jax.scipy.linalg.block_diag(*[pool_w[li, gi] for gi in range(len(POOL_WINDOWS))])
        h = _mix_out(sb, fox, u, pool_bd.astype(BF16), row(pool_scale[li]), w_out[li].astype(BF16),
                     row(mix_norm_post[li]), h, seq)

        k_mem, v_mem = _norm_matmul(mem2, row(mem_norm[li]), xa_wkv[li].astype(BF16),
                                    splits=[(0, d), (d, d)], dtypes=[BF16, BF16], tm=m_len)
        h = _xattn(h, row(xa_norm_pre[li]), xa_wq[li].astype(BF16),
                   k_mem.reshape(bsz, m_len, d), v_mem.reshape(bsz, m_len, d),
                   xa_wo[li].astype(BF16), row(xa_norm_post[li]), seq)

        j = li // 2
        if li % 2 == 0:
            h = _ffn(h, row(ffn_norm_pre[li]), dense_w_gate[j].astype(BF16), dense_w_up[j].astype(BF16),
                     dense_w_down[j].astype(BF16), row(ffn_norm_post[li]))
        else:
            h = _moe(h, row(ffn_norm_pre[li]), router_w[j], moe_w_gate[j].astype(BF16),
                     moe_w_up[j].astype(BF16), moe_w_down[j].astype(BF16), row(ffn_norm_post[li]))
    return h.reshape(bsz, seq, d)
```

```python
import functools

import jax
import jax.numpy as jnp
from jax import lax
from jax.experimental import pallas as pl
from jax.experimental.pallas import tpu as pltpu

F32 = jnp.float32
BF16 = jnp.bfloat16
EPS = 1e-6

HEAD_DIM = 64
LANES = 128
SUBLANES = 8
N_PAIRS = 3
GROUP_DIM = N_PAIRS * LANES
QKV_DIM = 6 * GROUP_DIM
POOL_DIM = 256
POOL_WINDOWS = (2, 4, 8, 16)
POOL_HALO = 16
N_FOX = 6
XA_HEADS = 4
N_EXPERTS = 8
TOP_K = 2
MOE_ROWS = 512
FF_CHUNK = 256
DMA_UNROLL = 8
SB_SUB = 256
SB_CHAINS = 4
FOX_SUB = 256
FOX_CHAINS = 8
FOX_SAFE_BOUND = 40.0
EXP_CUTOFF = 105.0
VMEM_LIMIT = 56 * 1024 * 1024

_NT = (((1,), (1,)), ((), ()))


def _params(n_axes):
    return pltpu.CompilerParams(dimension_semantics=("arbitrary",) * n_axes,
                                vmem_limit_bytes=VMEM_LIMIT)


def _rms(x, g):
    return x * lax.rsqrt(jnp.mean(x * x, axis=-1, keepdims=True) + EPS) * g


def _dot(a, b):
    return jnp.dot(a, b, preferred_element_type=F32)


def _split3(x):
    hi = x.astype(BF16)
    r1 = x - hi.astype(F32)
    mid = r1.astype(BF16)
    lo = (r1 - mid.astype(F32)).astype(BF16)
    return hi, mid, lo


def _norm_matmul_kernel(x_ref, g_ref, w_ref, *out_refs, splits):
    yb = _rms(x_ref[...], g_ref[...]).astype(BF16)
    for o_ref, (c0, width) in zip(out_refs, splits):
        for c in range(0, width, 256):
            cw = min(256, width - c)
            o_ref[:, c:c + cw] = _dot(yb, w_ref[:, c0 + c:c0 + c + cw]).astype(o_ref.dtype)


def _norm_matmul(x, g, w, splits, dtypes, tm):
    n, d = x.shape
    kern = functools.partial(_norm_matmul_kernel, splits=tuple(splits))
    return pl.pallas_call(
        kern,
        grid=(n // tm,),
        in_specs=[pl.BlockSpec((tm, d), lambda i: (i, 0)),
                  pl.BlockSpec((1, d), lambda i: (0, 0)),
                  pl.BlockSpec(w.shape, lambda i: (0, 0))],
        out_specs=[pl.BlockSpec((tm, wd), lambda i: (i, 0)) for (_, wd) in splits],
        out_shape=[jax.ShapeDtypeStruct((n, wd), dt) for (_, wd), dt in zip(splits, dtypes)],
        compiler_params=_params(1),
        name="norm_matmul",
    )(x, g, w)


def _logf_cumsum_kernel(fl_ref, b_ref, c_ref, carry_ref, *, tc):
    @pl.when(pl.program_id(1) == 0)
    def _():
        carry_ref[...] = jnp.zeros_like(carry_ref)

    x = fl_ref[...] + b_ref[...]
    ls = jnp.minimum(x, 0.0) - jnp.log(1.0 + jnp.exp(-jnp.abs(x)))
    row = lax.broadcasted_iota(jnp.int32, (tc, tc), 0)
    col = lax.broadcasted_iota(jnp.int32, (tc, tc), 1)
    tri = jnp.where(row >= col, 1.0, 0.0).astype(BF16)
    hi, mid, lo = _split3(ls)
    c = _dot(tri, hi) + _dot(tri, mid) + _dot(tri, lo) + carry_ref[...]
    c_ref[...] = c
    carry_ref[...] = c[tc - 1:tc, :]


def _logf_cumsum(fl, b_pad, tc=1024):
    bsz, s, _ = fl.shape
    return pl.pallas_call(
        functools.partial(_logf_cumsum_kernel, tc=tc),
        grid=(bsz, s // tc),
        in_specs=[pl.BlockSpec((None, tc, LANES), lambda b, j: (b, j, 0)),
                  pl.BlockSpec((1, LANES), lambda b, j: (0, 0))],
        out_specs=pl.BlockSpec((None, tc, LANES), lambda b, j: (b, j, 0)),
        out_shape=jax.ShapeDtypeStruct(fl.shape, F32),
        scratch_shapes=[pltpu.VMEM((1, LANES), F32)],
        compiler_params=_params(2),
        name="logf_cumsum",
    )(fl, b_pad)


def _sb_kernel(q_ref, k_ref, v_ref, tri_ref, o_ref, acc_ref, run_ref):
    sub, n_chain = SB_SUB, SB_CHAINS
    blk0 = pl.program_id(2) * n_chain
    lane = lax.broadcasted_iota(jnp.int32, (1, LANES), 1)
    in_head = (lane < HEAD_DIM, lane >= HEAD_DIM)
    q_all = q_ref[...] * jnp.asarray(HEAD_DIM ** -0.5, BF16)
    q_chain = [[jnp.where(in_head[hd], q_all[r * sub:(r + 1) * sub, :], jnp.zeros((), BF16))
                for r in range(n_chain)] for hd in range(2)]
    row = lax.broadcasted_iota(jnp.int32, (sub, sub), 0)
    col = lax.broadcasted_iota(jnp.int32, (sub, sub), 1)
    acc_ref[...] = jnp.zeros_like(acc_ref)
    run_ref[...] = jnp.zeros_like(run_ref)

    def tile(hd, r, d, diag):
        kb = blk0 + r - d
        start = pl.multiple_of(jnp.maximum(kb, 0) * sub, sub)
        k = k_ref[pl.ds(start, sub), :]
        v = jnp.where(in_head[hd], v_ref[pl.ds(start, sub), :], jnp.zeros((), BF16))
        z = lax.dot_general(q_chain[hd][r], k, _NT, preferred_element_type=F32)
        sp = jnp.maximum(z, 0.0) + jnp.log(1.0 + jnp.exp(-jnp.abs(z)))
        if diag:
            sp = jnp.where(col < row, sp, 0.0)
        suffix = _dot(sp.astype(BF16), tri_ref[...])
        run = run_ref[hd, r]
        w = jnp.exp(z - suffix - jnp.where(kb >= 0, run, 1e30))
        if diag:
            w = jnp.where(col < row, w, 0.0)
        acc_ref[hd, r] += _dot(w.astype(BF16), v)
        run_ref[hd, r] = run + suffix[:, 0:1]

    def more(d):
        go = jnp.bool_(False)
        for hd in range(2):
            for r in range(n_chain):
                unfinished = jnp.min(run_ref[hd, r]) < EXP_CUTOFF
                go = jnp.logical_or(go, jnp.logical_and(blk0 + r - d >= 0, unfinished))
        return go.astype(jnp.int32)

    def step(d, diag):
        for r in range(n_chain):
            for hd in range(2):
                tile(hd, r, d, diag)

    def body(carry):
        d, _ = carry
        step(d, False)
        return d + 1, more(d + 1)

    step(0, True)
    lax.while_loop(lambda carry: carry[1] > 0, body, (jnp.int32(1), more(1)))
    for r in range(n_chain):
        o_ref[r * sub:(r + 1) * sub, :] = (acc_ref[0, r] + acc_ref[1, r]).astype(o_ref.dtype)


def _sb_attention(qkv, tri):
    bsz, s, _ = qkv.shape
    tq = SB_SUB * SB_CHAINS
    return pl.pallas_call(
        _sb_kernel,
        grid=(bsz, N_PAIRS, s // tq),
        in_specs=[pl.BlockSpec((None, tq, LANES), lambda b, p, i: (b, i, p)),
                  pl.BlockSpec((None, s, LANES), lambda b, p, i: (b, 0, N_PAIRS + p)),
                  pl.BlockSpec((None, s, LANES), lambda b, p, i: (b, 0, 2 * N_PAIRS + p)),
                  pl.BlockSpec((SB_SUB, SB_SUB), lambda b, p, i: (0, 0))],
        out_specs=pl.BlockSpec((None, tq, LANES), lambda b, p, i: (b, i, p)),
        out_shape=jax.ShapeDtypeStruct((bsz, s, GROUP_DIM), BF16),
        scratch_shapes=[pltpu.VMEM((2, SB_CHAINS, SB_SUB, LANES), F32),
                        pltpu.VMEM((2, SB_CHAINS, SB_SUB, 1), F32)],
        compiler_params=_params(3),
        name="sb_attention",
    )(qkv, qkv, qkv, tri)


def _fox_kernel(q_ref, k_ref, v_ref, ccol_ref, crow_ref, cs_ref, o_ref, acc_ref, z_ref, kmax_ref,
                *, seq):
    sub, n_chain = FOX_SUB, FOX_CHAINS
    pair = pl.program_id(1)
    i = pl.program_id(2)
    lane = lax.broadcasted_iota(jnp.int32, (1, LANES), 1)
    in_head = (lane < HEAD_DIM, lane >= HEAD_DIM)
    one_lane = (HEAD_DIM, 0)

    @pl.when(i == 0)
    def _():
        def chunk(t, mx):
            k = k_ref[pl.ds(pl.multiple_of(t * sub, sub), sub), :].astype(F32)
            sq = k * k
            return tuple(jnp.maximum(mx[hd], jnp.max(jnp.sum(jnp.where(in_head[hd], sq, 0.0), axis=1,
                                                              keepdims=True))) for hd in range(2))
        mx = lax.fori_loop(0, seq // sub, chunk, (jnp.float32(0.0), jnp.float32(0.0)))
        kmax_ref[0] = mx[0]
        kmax_ref[1] = mx[1]

    q_all = q_ref[...] * jnp.asarray(HEAD_DIM ** -0.5, BF16)
    row = lax.broadcasted_iota(jnp.int32, (sub, sub), 0)
    col = lax.broadcasted_iota(jnp.int32, (sub, sub), 1)
    blk0 = i * n_chain
    acc_ref[...] = jnp.zeros_like(acc_ref)

    for hd in range(2):
        q_head = jnp.where(in_head[hd], q_all, jnp.zeros((), BF16))
        q32 = q_head.astype(F32)
        zb = jnp.sqrt(jnp.sum(q32 * q32, axis=1, keepdims=True) * kmax_ref[hd]) * 1.001
        zb_max = jnp.max(zb)
        cutoff = EXP_CUTOFF + 2.0 * zb_max
        c_base = cs_ref[hd, blk0 * sub]
        c_col = jnp.sum(jnp.where(lane == 2 * pair + hd, ccol_ref[...], 0.0), axis=1, keepdims=True)
        a_all = (c_col - c_base) - zb
        v_one = jnp.where(lane == one_lane[hd], 1.0, 0.0).astype(BF16)
        q_chain = [q_head[r * sub:(r + 1) * sub, :] for r in range(n_chain)]
        a_chain = [a_all[r * sub:(r + 1) * sub, :] for r in range(n_chain)]

        def tile_inputs(kb, hd=hd, c_base=c_base, v_one=v_one):
            start = pl.multiple_of(jnp.maximum(kb, 0) * sub, sub)
            k = k_ref[pl.ds(start, sub), :]
            v = jnp.where(in_head[hd], v_ref[pl.ds(start, sub), :], v_one)
            c_row = crow_ref[hd:hd + 1, pl.ds(start, sub)] - c_base
            return k, v, jnp.where(kb >= 0, c_row, 1e30)

        def more(d, cutoffs, hd=hd):
            go = jnp.bool_(False)
            for r in range(n_chain):
                kb = blk0 + r - d
                first = cs_ref[hd, (blk0 + r) * sub]
                last = cs_ref[hd, jnp.maximum(kb, 0) * sub + sub - 1]
                go = jnp.logical_or(go, jnp.logical_and(kb >= 0, first - last > -cutoffs[r]))
            return go

        def scores(r, d, q_chain=q_chain):
            start = pl.multiple_of(jnp.maximum(blk0 + r - d, 0) * sub, sub)
            return lax.dot_general(q_chain[r], k_ref[pl.ds(start, sub), :], _NT,
                                   preferred_element_type=F32)

        def fast_tile(r, d, z, diag, hd=hd, a_chain=a_chain, tile_inputs=tile_inputs):
            _, v, c_row = tile_inputs(blk0 + r - d)
            e = z + a_chain[r] - c_row
            if diag:
                e = jnp.where(col <= row, e, -1e30)
            acc_ref[hd, r] += _dot(jnp.exp(e).astype(BF16), v)
            return jnp.min(jnp.max(e, axis=1, keepdims=True)) if diag else None

        def slow_tile(r, d, m_old, diag, hd=hd, q_chain=q_chain, tile_inputs=tile_inputs):
            k, v, c_row = tile_inputs(blk0 + r - d)
            s = lax.dot_general(q_chain[r], k, _NT, preferred_element_type=F32) - c_row
            if diag:
                s = jnp.where(col <= row, s, -1e30)
            m_new = jnp.maximum(m_old, jnp.max(s, axis=1, keepdims=True))
            acc_ref[hd, r] = (acc_ref[hd, r] * jnp.exp(m_old - m_new)
                              + _dot(jnp.exp(s - m_new).astype(BF16), v))
            return m_new

        @pl.when(zb_max <= FOX_SAFE_BOUND)
        def _(fast_tile=fast_tile, scores=scores, more=more):
            cutoffs = []
            for r in range(n_chain):
                cutoffs.append(EXP_CUTOFF - fast_tile(r, 0, scores(r, 0), True))
                z_ref[r] = scores(r, 1)

            def body(d):
                for r in range(n_chain):
                    z = z_ref[r]
                    z_ref[r] = scores(r, d + 1)
                    fast_tile(r, d, z, False)
                return d + 1
            lax.while_loop(lambda d: more(d, cutoffs), body, jnp.int32(1))

        @pl.when(zb_max > FOX_SAFE_BOUND)
        def _(slow_tile=slow_tile, more=more):
            m0 = jnp.full((sub, 1), -1e30, F32)
            ms = tuple(slow_tile(r, 0, m0, True) for r in range(n_chain))

            def body(carry):
                d, ms = carry
                return d + 1, tuple(slow_tile(r, d, ms[r], False) for r in range(n_chain))
            lax.while_loop(lambda carry: more(carry[0], [cutoff] * n_chain), body, (jnp.int32(1), ms))

    for r in range(n_chain):
        acc = (acc_ref[0, r], acc_ref[1, r])
        total = [jnp.sum(jnp.where(lane == one_lane[hd], acc[hd], 0.0), axis=1, keepdims=True)
                 for hd in range(2)]
        out = jnp.where(in_head[0], acc[0] / total[0], acc[1] / total[1])
        o_ref[r * sub:(r + 1) * sub, :] = out.astype(o_ref.dtype)


def _fox_attention(qkv, c_cols, c_rows):
    bsz, s, _ = qkv.shape
    tq = FOX_SUB * FOX_CHAINS
    return pl.pallas_call(
        functools.partial(_fox_kernel, seq=s),
        grid=(bsz, N_PAIRS, s // tq),
        in_specs=[pl.BlockSpec((None, tq, LANES), lambda b, p, i: (b, i, 3 * N_PAIRS + p)),
                  pl.BlockSpec((None, s, LANES), lambda b, p, i: (b, 0, 4 * N_PAIRS + p)),
                  pl.BlockSpec((None, s, LANES), lambda b, p, i: (b, 0, 5 * N_PAIRS + p)),
                  pl.BlockSpec((None, tq, LANES), lambda b, p, i: (b, i, 0)),
                  pl.BlockSpec((None, None, 2, s), lambda b, p, i: (b, p, 0, 0)),
                  pl.BlockSpec((None, None, 2, s), lambda b, p, i: (b, p, 0, 0),
                               memory_space=pltpu.SMEM)],
        out_specs=pl.BlockSpec((None, tq, LANES), lambda b, p, i: (b, i, p)),
        out_shape=jax.ShapeDtypeStruct((bsz, s, GROUP_DIM), BF16),
        scratch_shapes=[pltpu.VMEM((2, FOX_CHAINS, FOX_SUB, LANES), F32),
                        pltpu.VMEM((FOX_CHAINS, FOX_SUB, FOX_SUB), F32), pltpu.SMEM((2,), F32)],
        compiler_params=_params(3),
        name="fox_attention",
    )(qkv, qkv, qkv, c_cols, c_rows, c_rows)


def _pool(u, halo, pw_ref, ps_ref, pos0, tm):
    x = jnp.concatenate([halo, u], axis=0)
    s2 = x + pltpu.roll(x, 1, 0)
    s4 = s2 + pltpu.roll(s2, 2, 0)
    s8 = s4 + pltpu.roll(s4, 4, 0)
    s16 = s8 + pltpu.roll(s8, 8, 0)
    lane = lax.broadcasted_iota(jnp.int32, (1, POOL_DIM), 1)
    grp = POOL_DIM // len(POOL_WINDOWS)
    wsum = jnp.where(lane < grp, s2, jnp.where(lane < 2 * grp, s4, jnp.where(lane < 3 * grp, s8, s16)))
    win = jnp.where(lane < grp, 2, jnp.where(lane < 2 * grp, 4, jnp.where(lane < 3 * grp, 8, 16)))
    pos = pos0 + lax.broadcasted_iota(jnp.int32, (tm, 1), 0)
    count = jnp.minimum(pos + 1, win).astype(F32)
    r = wsum[POOL_HALO:, :] / count - u
    return _dot(r.astype(BF16), pw_ref[...]) * ps_ref[...]


def _mix_xattn_kernel(sb_ref, fox_ref, u_ref, halo_ref, pw_ref, ps_ref, wout_ref, gmix_ref, h_ref,
                      gpre_ref, wq_ref, k_ref, v_ref, wo_ref, gpost_ref, o_ref, *, tm, seq):
    pos0 = (pl.program_id(0) * tm) % seq
    halo = jnp.where(pos0 == 0, 0.0, halo_ref[...])
    pool = _pool(u_ref[...], halo, pw_ref, ps_ref, pos0, tm)
    gd = GROUP_DIM
    a = (_dot(sb_ref[...], wout_ref[0:gd, :]) + _dot(fox_ref[...], wout_ref[gd:2 * gd, :])
         + _dot(pool.astype(BF16), wout_ref[2 * gd:, :]))
    h = h_ref[...] + _rms(a, gmix_ref[...])

    d = h.shape[-1]
    hd = d // XA_HEADS
    hn = _rms(h, gpre_ref[...]).astype(BF16)
    q = (_dot(hn, wq_ref[...]) * (hd ** -0.5)).astype(BF16)
    outs = []
    for head in range(XA_HEADS):
        sl = slice(head * hd, (head + 1) * hd)
        s = lax.dot_general(q[:, sl], k_ref[:, sl], _NT, preferred_element_type=F32)
        p = jnp.exp(s - jnp.max(s, axis=1, keepdims=True))
        p = p * (1.0 / jnp.sum(p, axis=1, keepdims=True))
        outs.append(_dot(p.astype(BF16), v_ref[:, sl]).astype(BF16))
    c = _dot(jnp.concatenate(outs, axis=1), wo_ref[...])
    o_ref[...] = h + _rms(c, gpost_ref[...])


def _mix_xattn(sb, fox, u, pool_w_bd, pool_scale, w_out, g_mix, h, g_pre, wq, k_mem, v_mem, wo, g_post,
               seq, tm=512):
    n, d = h.shape
    m_len = k_mem.shape[1]
    per_seq = seq // tm
    hb = tm // POOL_HALO
    const = lambda shape: pl.BlockSpec(shape, lambda i: (0,) * len(shape))
    rows = lambda width: pl.BlockSpec((tm, width), lambda i: (i, 0))
    mem = pl.BlockSpec((None, m_len, d), lambda i: (i // per_seq, 0, 0))
    return pl.pallas_call(
        functools.partial(_mix_xattn_kernel, tm=tm, seq=seq),
        grid=(n // tm,),
        in_specs=[rows(GROUP_DIM), rows(GROUP_DIM), rows(POOL_DIM),
                  pl.BlockSpec((POOL_HALO, POOL_DIM), lambda i: (jnp.maximum(i * hb - 1, 0), 0)),
                  const((POOL_DIM, POOL_DIM)), const((1, POOL_DIM)), const(w_out.shape), const((1, d)),
                  rows(d), const((1, d)), const((d, d)), mem, mem, const((d, d)), const((1, d))],
        out_specs=rows(d),
        out_shape=jax.ShapeDtypeStruct((n, d), F32),
        compiler_params=_params(1),
        name="mix_xattn",
    )(sb, fox, u, u, pool_w_bd, pool_scale, w_out, g_mix, h, g_pre, wq, k_mem, v_mem, wo, g_post)


def _silu(x):
    return x / (1.0 + jnp.exp(-x))


def _swiglu(x, wg_ref, wu_ref, wd_ref, act_ref):
    ff = wg_ref.shape[1]
    for c in range(0, ff, FF_CHUNK):
        gate = _dot(x, wg_ref[:, c:c + FF_CHUNK])
        up = _dot(x, wu_ref[:, c:c + FF_CHUNK])
        act_ref[:, c:c + FF_CHUNK] = (_silu(gate) * up).astype(BF16)
    return _dot(act_ref[...], wd_ref[...])


def _ffn_kernel(h_ref, gpre_ref, wg_ref, wu_ref, wd_ref, gpost_ref, o_ref, act_ref):
    h = h_ref[...]
    f = _swiglu(_rms(h, gpre_ref[...]).astype(BF16), wg_ref, wu_ref, wd_ref, act_ref)
    o_ref[...] = h + _rms(f, gpost_ref[...])


def _ffn(h, g_pre, wg, wu, wd, g_post, tm=512):
    n, d = h.shape
    ff = wg.shape[1]
    resident = pl.Buffered(1)
    return pl.pallas_call(
        _ffn_kernel,
        grid=(n // tm,),
        in_specs=[pl.BlockSpec((tm, d), lambda i: (i, 0)),
                  pl.BlockSpec((1, d), lambda i: (0, 0)),
                  pl.BlockSpec((d, ff), lambda i: (0, 0), pipeline_mode=resident),
                  pl.BlockSpec((d, ff), lambda i: (0, 0), pipeline_mode=resident),
                  pl.BlockSpec((ff, d), lambda i: (0, 0), pipeline_mode=resident),
                  pl.BlockSpec((1, d), lambda i: (0, 0))],
        out_specs=pl.BlockSpec((tm, d), lambda i: (i, 0)),
        out_shape=jax.ShapeDtypeStruct((n, d), F32),
        scratch_shapes=[pltpu.VMEM((tm, ff), BF16)],
        compiler_params=_params(1),
        name="ffn_dense",
    )(h, g_pre, wg, wu, wd, g_post)


def _router_kernel(h_ref, g_ref, wr_ref, tri_ref, idx_ref, gate_ref, count_ref, carry_ref):
    @pl.when(pl.program_id(0) == 0)
    def _():
        carry_ref[...] = jnp.zeros_like(carry_ref)

    hn = _rms(h_ref[...], g_ref[...])
    x_hi, x_mid, _ = _split3(hn)
    w_hi, w_mid = wr_ref[0], wr_ref[1]
    logits = _dot(x_hi, w_hi) + _dot(x_hi, w_mid) + _dot(x_mid, w_hi)
    tm = logits.shape[0]
    lane = lax.broadcasted_iota(jnp.int32, logits.shape, 1)
    logits = jnp.where(lane < N_EXPERTS, logits, -jnp.inf)
    m1 = jnp.max(logits, axis=1, keepdims=True)
    i1 = jnp.min(jnp.where(logits == m1, lane, LANES), axis=1, keepdims=True)
    rest = jnp.where(lane == i1, -jnp.inf, logits)
    m2 = jnp.max(rest, axis=1, keepdims=True)
    i2 = jnp.min(jnp.where(rest == m2, lane, LANES), axis=1, keepdims=True)
    e = jnp.exp(m2 - m1)
    g1 = 1.0 / (1.0 + e)
    gate_ref[...] = jnp.where(lane == 0, g1, jnp.where(lane == 1, e * g1, 0.0))

    onehot = jnp.where(lane == i1, 1.0, jnp.where(lane == i2, 1.0, 0.0))
    before = _dot(tri_ref[...], onehot.astype(BF16)) + carry_ref[...]
    r1 = jnp.sum(jnp.where(lane == i1, before, 0.0), axis=1, keepdims=True).astype(jnp.int32)
    r2 = jnp.sum(jnp.where(lane == i2, before, 0.0), axis=1, keepdims=True).astype(jnp.int32)
    idx_ref[...] = jnp.where(lane == 0, i1, jnp.where(lane == 1, i2,
                             jnp.where(lane == 2, r1, jnp.where(lane == 3, r2, 0))))
    total = before[tm - 1:tm, :] + onehot[tm - 1:tm, :]
    carry_ref[...] = total
    count_ref[...] = total


def _router(h, g, wr_split, tm=512):
    n, d = h.shape
    t = jnp.arange(tm)
    tri = (t[:, None] > t[None, :]).astype(BF16)
    return pl.pallas_call(
        _router_kernel,
        grid=(n // tm,),
        in_specs=[pl.BlockSpec((tm, d), lambda i: (i, 0)),
                  pl.BlockSpec((1, d), lambda i: (0, 0)),
                  pl.BlockSpec(wr_split.shape, lambda i: (0, 0, 0)),
                  pl.BlockSpec((tm, tm), lambda i: (0, 0))],
        out_specs=[pl.BlockSpec((tm, LANES), lambda i: (i, 0)),
                   pl.BlockSpec((tm, LANES), lambda i: (i, 0)),
                   pl.BlockSpec((1, LANES), lambda i: (0, 0))],
        out_shape=[jax.ShapeDtypeStruct((n, LANES), jnp.int32),
                   jax.ShapeDtypeStruct((n, LANES), F32),
                   jax.ShapeDtypeStruct((1, LANES), F32)],
        scratch_shapes=[pltpu.VMEM((1, LANES), F32)],
        compiler_params=_params(1),
        name="router",
    )(h, g, wr_split, tri)


def _rows_to_tiles(ref, x):
    m = x.shape[0]
    for s in range(SUBLANES):
        ref[pl.ds(s, m, stride=SUBLANES), :] = x[:, s * LANES:(s + 1) * LANES]


def _tiles_to_rows(ref, m):
    return jnp.concatenate([ref[pl.ds(s, m, stride=SUBLANES), :] for s in range(SUBLANES)], axis=1)


def _tile_rows(ref, first_row, rows):
    return ref.at[pl.ds(pl.multiple_of(first_row * SUBLANES, SUBLANES), rows * SUBLANES)]


def _dispatch_kernel(dest_ref, pad_start_ref, pad_len_ref, n_valid_ref, h_ref, g_ref, xs_hbm,
                     buf_ref, zero_ref, sem, zero_sem, *, tm):
    base = pl.program_id(0) * tm * TOP_K

    @pl.when(pl.program_id(0) == 0)
    def _():
        zero_ref[...] = jnp.zeros_like(zero_ref)
        n_blocks = n_valid_ref.shape[0]

        def zero_copy(first_row, rows):
            return pltpu.make_async_copy(_tile_rows(zero_ref, 0, rows), _tile_rows(xs_hbm, first_row, rows),
                                         zero_sem)

        def fill(wait):
            def go(cp):
                cp.wait() if wait else cp.start()

            for e in range(N_EXPERTS):
                pos = pad_start_ref[e]
                left = pad_len_ref[e]
                rows = MOE_ROWS // 2
                while rows >= 1:
                    take = (left & rows) != 0

                    @pl.when(take)
                    def _(pos=pos, rows=rows):
                        go(zero_copy(pos, rows))
                    pos = pos + jnp.where(take, rows, 0)
                    rows //= 2

            def blocks(b, carry):
                @pl.when(n_valid_ref[b] == 0)
                def _():
                    go(zero_copy(b * MOE_ROWS, MOE_ROWS))
                return carry
            lax.fori_loop(0, n_blocks, blocks, 0)

        fill(False)
        fill(True)

    _rows_to_tiles(buf_ref, _rms(h_ref[...], g_ref[...]))

    def start(r, carry):
        for k in range(TOP_K):
            dst = dest_ref[base + r * TOP_K + k]
            pltpu.make_async_copy(_tile_rows(buf_ref, r, 1), _tile_rows(xs_hbm, dst, 1), sem).start()
        return carry

    lax.fori_loop(0, tm, start, 0, unroll=DMA_UNROLL)
    for _ in range(TOP_K):
        pltpu.make_async_copy(buf_ref, _tile_rows(xs_hbm, 0, tm), sem).wait()


def _dispatch(dest, pad_start, pad_len, n_valid, h, g, tm=512):
    n, d = h.shape
    assert d == SUBLANES * LANES
    n_rows = n_valid.shape[0] * MOE_ROWS
    grid_spec = pltpu.PrefetchScalarGridSpec(
        num_scalar_prefetch=4,
        grid=(n // tm,),
        in_specs=[pl.BlockSpec((tm, d), lambda i, *_: (i, 0)),
                  pl.BlockSpec((1, d), lambda i, *_: (0, 0))],
        out_specs=pl.BlockSpec(memory_space=pl.ANY),
        scratch_shapes=[pltpu.VMEM((tm * SUBLANES, LANES), F32), pltpu.VMEM((MOE_ROWS * SUBLANES, LANES), F32),
                        pltpu.SemaphoreType.DMA, pltpu.SemaphoreType.DMA],
    )
    return pl.pallas_call(
        functools.partial(_dispatch_kernel, tm=tm),
        grid_spec=grid_spec,
        out_shape=jax.ShapeDtypeStruct((n_rows * SUBLANES, LANES), F32),
        compiler_params=_params(1),
        name="moe_dispatch",
    )(dest, pad_start, pad_len, n_valid, h, g)


def _expert_kernel(blk_e_ref, n_valid_ref, xs_ref, wg_ref, wu_ref, wd_ref, ys_ref, act_ref):
    n_valid = n_valid_ref[pl.program_id(0)]

    @pl.when(n_valid > 0)
    def _():
        x = _tiles_to_rows(xs_ref, MOE_ROWS).astype(BF16)
        _rows_to_tiles(ys_ref, _swiglu(x, wg_ref, wu_ref, wd_ref, act_ref))

    @pl.when(n_valid == 0)
    def _():
        ys_ref[...] = jnp.zeros_like(ys_ref)


def _experts(blk_e, n_valid, xs, wg, wu, wd):
    d, ff = wg.shape[1], wg.shape[2]
    blk = MOE_ROWS * SUBLANES
    grid_spec = pltpu.PrefetchScalarGridSpec(
        num_scalar_prefetch=2,
        grid=(xs.shape[0] // blk,),
        in_specs=[pl.BlockSpec((blk, LANES), lambda i, be, nv: (i, 0)),
                  pl.BlockSpec((None, d, ff), lambda i, be, nv: (be[i], 0, 0)),
                  pl.BlockSpec((None, d, ff), lambda i, be, nv: (be[i], 0, 0)),
                  pl.BlockSpec((None, ff, d), lambda i, be, nv: (be[i], 0, 0))],
        out_specs=pl.BlockSpec((blk, LANES), lambda i, be, nv: (i, 0)),
        scratch_shapes=[pltpu.VMEM((MOE_ROWS, ff), BF16)],
    )
    return pl.pallas_call(
        _expert_kernel,
        grid_spec=grid_spec,
        out_shape=jax.ShapeDtypeStruct(xs.shape, F32),
        compiler_params=_params(1),
        name="moe_experts",
    )(blk_e, n_valid, xs, wg, wu, wd)


def _combine_kernel(dest_ref, ys_hbm, gate_ref, g_ref, h_ref, o_ref, buf_ref, sem, *, tm):
    i = pl.program_id(0)

    def fetch(tile, slot):
        def body(r, carry):
            for k in range(TOP_K):
                src = dest_ref[(tile * tm + r) * TOP_K + k]
                pltpu.make_async_copy(_tile_rows(ys_hbm, src, 1), _tile_rows(buf_ref.at[slot, k], r, 1),
                                      sem.at[slot]).start()
            return carry
        lax.fori_loop(0, tm, body, 0, unroll=DMA_UNROLL)

    @pl.when(i == 0)
    def _():
        fetch(0, 0)

    @pl.when(i + 1 < pl.num_programs(0))
    def _():
        fetch(i + 1, (i + 1) % 2)

    slot = i % 2

    for k in range(TOP_K):
        pltpu.make_async_copy(_tile_rows(ys_hbm, 0, tm), buf_ref.at[slot, k], sem.at[slot]).wait()
    gates = gate_ref[...]
    f = (_tiles_to_rows(buf_ref.at[slot, 0], tm) * gates[:, 0:1]
         + _tiles_to_rows(buf_ref.at[slot, 1], tm) * gates[:, 1:2])
    o_ref[...] = h_ref[...] + _rms(f, g_ref[...])


def _combine(dest, ys, gates, g, h, tm=512):
    n, d = h.shape
    grid_spec = pltpu.PrefetchScalarGridSpec(
        num_scalar_prefetch=1,
        grid=(n // tm,),
        in_specs=[pl.BlockSpec(memory_space=pl.ANY),
                  pl.BlockSpec((tm, LANES), lambda i, dst: (i, 0)),
                  pl.BlockSpec((1, d), lambda i, dst: (0, 0)),
                  pl.BlockSpec((tm, d), lambda i, dst: (i, 0))],
        out_specs=pl.BlockSpec((tm, d), lambda i, dst: (i, 0)),
        scratch_shapes=[pltpu.VMEM((2, TOP_K, tm * SUBLANES, LANES), F32), pltpu.SemaphoreType.DMA((2,))],
    )
    return pl.pallas_call(
        functools.partial(_combine_kernel, tm=tm),
        grid_spec=grid_spec,
        out_shape=jax.ShapeDtypeStruct((n, d), F32),
        compiler_params=_params(1),
        name="moe_combine",
    )(dest, ys, gates, g, h)


def _route_plan(idx, counts_f, n_tok):
    counts = counts_f[0, :N_EXPERTS].astype(jnp.int32)
    padded = ((counts + MOE_ROWS - 1) // MOE_ROWS) * MOE_ROWS
    pend = jnp.cumsum(padded)
    pstart = pend - padded
    expert = idx[:, 0:TOP_K]
    rank = idx[:, TOP_K:2 * TOP_K]
    offset = jnp.zeros_like(expert)
    for e in range(N_EXPERTS):
        offset = jnp.where(expert == e, pstart[e], offset)
    dest = (offset + rank).reshape(-1)
    n_blocks = (n_tok * TOP_K) // MOE_ROWS + N_EXPERTS
    blk_start = jnp.arange(n_blocks, dtype=jnp.int32) * MOE_ROWS
    blk_e = jnp.minimum(jnp.sum(blk_start[:, None] >= pend[None, :], axis=1), N_EXPERTS - 1).astype(jnp.int32)
    n_valid = jnp.clip(pstart[blk_e] + counts[blk_e] - blk_start, 0, MOE_ROWS).astype(jnp.int32)
    return dest, pstart + counts, padded - counts, blk_e, n_valid


def _moe(h, g_pre, router_w, wg, wu, wd, g_post):
    n, d = h.shape
    wr = jnp.pad(router_w, ((0, 0), (0, LANES - N_EXPERTS)))
    wr_hi = wr.astype(BF16)
    wr_mid = (wr - wr_hi.astype(F32)).astype(BF16)
    idx, gates, counts = _router(h, g_pre, jnp.stack([wr_hi, wr_mid]))
    dest, pad_start, pad_len, blk_e, n_valid = _route_plan(idx, counts, n)
    xs = _dispatch(dest, pad_start, pad_len, n_valid, h, g_pre)
    ys = _experts(blk_e, n_valid, xs, wg, wu, wd)
    return _combine(dest, ys, gates, g_post, h)


def kernel(x, mem, mix_norm_pre, mix_norm_post, w_in, b_forget, pool_w, pool_scale, w_out,
           xa_norm_pre, xa_norm_post, mem_norm, xa_wq, xa_wkv, xa_wo,
           ffn_norm_pre, ffn_norm_post, dense_w_gate, dense_w_up, dense_w_down,
           router_w, moe_w_gate, moe_w_up, moe_w_down):
    bsz, seq, d = x.shape
    m_len = mem.shape[1]
    depth = w_in.shape[0]
    n = bsz * seq
    h = x.reshape(n, d)
    mem2 = mem.reshape(bsz * m_len, d)
    row = lambda v: v.reshape(1, -1)

    idx = jnp.arange(SB_SUB)
    tri = (idx[:, None] >= idx[None, :]).astype(BF16)

    for li in range(depth):
        w = w_in[li]
        flog_w = jnp.pad(w[:, QKV_DIM:QKV_DIM + N_FOX], ((0, 0), (0, LANES - N_FOX)))
        w_cat = jnp.concatenate([w[:, :QKV_DIM], w[:, QKV_DIM + N_FOX:], flog_w], axis=1).astype(BF16)
        qkv, u, flog = _norm_matmul(
            h, row(mix_norm_pre[li]), w_cat,
            splits=[(0, QKV_DIM), (QKV_DIM, POOL_DIM), (QKV_DIM + POOL_DIM, LANES)],
            dtypes=[BF16, F32, F32], tm=512)
        qkv = qkv.reshape(bsz, seq, QKV_DIM)

        b_pad = jnp.pad(b_forget[li], (0, LANES - N_FOX)).reshape(1, LANES)
        c = _logf_cumsum(flog.reshape(bsz, seq, LANES), b_pad)
        c_rows = c[:, :, :N_FOX].transpose(0, 2, 1).reshape(bsz, N_PAIRS, 2, seq)

        sb = _sb_attention(qkv, tri).reshape(n, GROUP_DIM)
        fox = _fox_attention(qkv, c, c_rows).reshape(n, GROUP_DIM)

        pool_bd = jax.scipy.linalg.block_diag(*[pool_w[li, gi] for gi in range(len(POOL_WINDOWS))])
        k_mem, v_mem = _norm_matmul(mem2, row(mem_norm[li]), xa_wkv[li].astype(BF16),
                                    splits=[(0, d), (d, d)], dtypes=[BF16, BF16], tm=m_len)
        h = _mix_xattn(sb, fox, u, pool_bd.astype(BF16), row(pool_scale[li]), w_out[li].astype(BF16),
                       row(mix_norm_post[li]), h, row(xa_norm_pre[li]), xa_wq[li].astype(BF16),
                       k_mem.reshape(bsz, m_len, d), v_mem.reshape(bsz, m_len, d),
                       xa_wo[li].astype(BF16), row(xa_norm_post[li]), seq)

        j = li // 2
        if li % 2 == 0:
            h = _ffn(h, row(ffn_norm_pre[li]), dense_w_gate[j].astype(BF16), dense_w_up[j].astype(BF16),
                     dense_w_down[j].astype(BF16), row(ffn_norm_post[li]))
        else:
            h = _moe(h, row(ffn_norm_pre[li]), router_w[j], moe_w_gate[j].astype(BF16),
                     moe_w_up[j].astype(BF16), moe_w_down[j].astype(BF16), row(ffn_norm_post[li]))
    return h.reshape(bsz, seq, d)
```

```python
import functools

import jax
import jax.numpy as jnp
from jax import lax
from jax.experimental import pallas as pl
from jax.experimental.pallas import tpu as pltpu

F32 = jnp.float32
BF16 = jnp.bfloat16
EPS = 1e-6

HEAD_DIM = 64
LANES = 128
SUBLANES = 8
N_PAIRS = 3
GROUP_DIM = N_PAIRS * LANES
QKV_DIM = 6 * GROUP_DIM
POOL_DIM = 256
POOL_WINDOWS = (2, 4, 8, 16)
POOL_HALO = 16
N_FOX = 6
XA_HEADS = 4
N_EXPERTS = 8
TOP_K = 2
MOE_ROWS = 512
FF_CHUNK = 256
DMA_UNROLL = 8
SB_SUB = 256
SB_CHAINS = 4
FOX_SUB = 256
FOX_CHAINS = 8
FOX_SAFE_BOUND = 40.0
FOX_BOUND_ROWS = 1024
EXP_CUTOFF = 105.0
VMEM_LIMIT = 56 * 1024 * 1024

_NT = (((1,), (1,)), ((), ()))


def _params(n_axes):
    return pltpu.CompilerParams(dimension_semantics=("arbitrary",) * n_axes,
                                vmem_limit_bytes=VMEM_LIMIT)


def _rms(x, g):
    return x * lax.rsqrt(jnp.mean(x * x, axis=-1, keepdims=True) + EPS) * g


def _dot(a, b):
    return jnp.dot(a, b, preferred_element_type=F32)


def _split3(x):
    hi = x.astype(BF16)
    r1 = x - hi.astype(F32)
    mid = r1.astype(BF16)
    lo = (r1 - mid.astype(F32)).astype(BF16)
    return hi, mid, lo


def _norm_matmul_kernel(x_ref, g_ref, w_ref, *out_refs, splits):
    yb = _rms(x_ref[...], g_ref[...]).astype(BF16)
    for o_ref, (c0, width) in zip(out_refs, splits):
        for c in range(0, width, 256):
            cw = min(256, width - c)
            o_ref[:, c:c + cw] = _dot(yb, w_ref[:, c0 + c:c0 + c + cw]).astype(o_ref.dtype)


def _norm_matmul(x, g, w, splits, dtypes, tm):
    n, d = x.shape
    kern = functools.partial(_norm_matmul_kernel, splits=tuple(splits))
    return pl.pallas_call(
        kern,
        grid=(n // tm,),
        in_specs=[pl.BlockSpec((tm, d), lambda i: (i, 0)),
                  pl.BlockSpec((1, d), lambda i: (0, 0)),
                  pl.BlockSpec(w.shape, lambda i: (0, 0))],
        out_specs=[pl.BlockSpec((tm, wd), lambda i: (i, 0)) for (_, wd) in splits],
        out_shape=[jax.ShapeDtypeStruct((n, wd), dt) for (_, wd), dt in zip(splits, dtypes)],
        compiler_params=_params(1),
        name="norm_matmul",
    )(x, g, w)


def _in_proj_kernel(x_ref, g_ref, w_ref, sel_ref, qkv_ref, u_ref, fl_ref, qn_ref, kmax_ref):
    yb = _rms(x_ref[...], g_ref[...]).astype(BF16)
    for c in range(0, QKV_DIM, 256):
        qkv_ref[:, c:c + 256] = _dot(yb, w_ref[:, c:c + 256]).astype(BF16)
    u_ref[...] = _dot(yb, w_ref[:, QKV_DIM:QKV_DIM + POOL_DIM])
    fl_ref[...] = _dot(yb, w_ref[:, QKV_DIM + POOL_DIM:])

    def head_norms(c0):
        x = qkv_ref[:, c0:c0 + GROUP_DIM].astype(F32)
        sq = x * x
        hi = sq.astype(BF16)
        lo = (sq - hi.astype(F32)).astype(BF16)
        return _dot(hi, sel_ref[...]) + _dot(lo, sel_ref[...])

    qn_ref[...] = head_norms(3 * GROUP_DIM)
    kmax = jnp.max(head_norms(4 * GROUP_DIM), axis=0, keepdims=True)
    kmax_ref[...] = jnp.broadcast_to(kmax, kmax_ref.shape)


def _in_proj(x, g, w, tm=512):
    n, d = x.shape
    lanes = jnp.arange(GROUP_DIM)[:, None] // HEAD_DIM == jnp.arange(LANES)[None, :]
    rows = lambda width: pl.BlockSpec((tm, width), lambda i: (i, 0))
    return pl.pallas_call(
        _in_proj_kernel,
        grid=(n // tm,),
        in_specs=[rows(d), pl.BlockSpec((1, d), lambda i: (0, 0)), pl.BlockSpec(w.shape, lambda i: (0, 0)),
                  pl.BlockSpec((GROUP_DIM, LANES), lambda i: (0, 0))],
        out_specs=[rows(QKV_DIM), rows(POOL_DIM), rows(LANES), rows(LANES),
                   pl.BlockSpec((None, SUBLANES, LANES), lambda i: (i, 0, 0))],
        out_shape=[jax.ShapeDtypeStruct((n, QKV_DIM), BF16), jax.ShapeDtypeStruct((n, POOL_DIM), F32),
                   jax.ShapeDtypeStruct((n, LANES), F32), jax.ShapeDtypeStruct((n, LANES), F32),
                   jax.ShapeDtypeStruct((n // tm, SUBLANES, LANES), F32)],
        compiler_params=_params(1),
        name="in_proj",
    )(x, g, w, lanes.astype(BF16))


def _logf_cumsum_kernel(fl_ref, b_ref, qn_ref, kmax_ref, c_ref, a_ref, zbmax_ref, carry_ref, *, tc):
    @pl.when(pl.program_id(1) == 0)
    def _():
        carry_ref[...] = jnp.zeros_like(carry_ref)

    x = fl_ref[...] + b_ref[...]
    ls = jnp.minimum(x, 0.0) - jnp.log(1.0 + jnp.exp(-jnp.abs(x)))
    row = lax.broadcasted_iota(jnp.int32, (tc, tc), 0)
    col = lax.broadcasted_iota(jnp.int32, (tc, tc), 1)
    tri = jnp.where(row >= col, 1.0, 0.0).astype(BF16)
    hi, mid, lo = _split3(ls)
    c = _dot(tri, hi) + _dot(tri, mid) + _dot(tri, lo) + carry_ref[...]
    c_ref[...] = c
    carry_ref[...] = c[tc - 1:tc, :]
    zb = jnp.sqrt(qn_ref[...] * kmax_ref[...] * (1.0 / HEAD_DIM)) * 1.001
    a_ref[...] = c - zb
    zbmax_ref[...] = jnp.broadcast_to(jnp.max(zb, axis=0, keepdims=True), zbmax_ref.shape)


def _logf_cumsum(fl, b_pad, qn, kmax, tc=FOX_BOUND_ROWS):
    bsz, s, _ = fl.shape
    rows = pl.BlockSpec((None, tc, LANES), lambda b, j: (b, j, 0))
    return pl.pallas_call(
        functools.partial(_logf_cumsum_kernel, tc=tc),
        grid=(bsz, s // tc),
        in_specs=[rows, pl.BlockSpec((1, LANES), lambda b, j: (0, 0)), rows,
                  pl.BlockSpec((None, 1, LANES), lambda b, j: (b, 0, 0))],
        out_specs=[rows, rows, pl.BlockSpec((None, None, SUBLANES, LANES), lambda b, j: (b, j, 0, 0))],
        out_shape=[jax.ShapeDtypeStruct(fl.shape, F32), jax.ShapeDtypeStruct(fl.shape, F32),
                   jax.ShapeDtypeStruct((bsz, s // tc, SUBLANES, LANES), F32)],
        scratch_shapes=[pltpu.VMEM((1, LANES), F32)],
        compiler_params=_params(2),
        name="logf_cumsum",
    )(fl, b_pad, qn, kmax)


def _sb_kernel(q_ref, k_ref, v_ref, tri_ref, o_ref, acc_ref, run_ref):
    sub, n_chain = SB_SUB, SB_CHAINS
    blk0 = pl.program_id(2) * n_chain
    lane = lax.broadcasted_iota(jnp.int32, (1, LANES), 1)
    in_head = (lane < HEAD_DIM, lane >= HEAD_DIM)
    q_all = q_ref[...] * jnp.asarray(HEAD_DIM ** -0.5, BF16)
    q_chain = [[jnp.where(in_head[hd], q_all[r * sub:(r + 1) * sub, :], jnp.zeros((), BF16))
                for r in range(n_chain)] for hd in range(2)]
    row = lax.broadcasted_iota(jnp.int32, (sub, sub), 0)
    col = lax.broadcasted_iota(jnp.int32, (sub, sub), 1)
    acc_ref[...] = jnp.zeros_like(acc_ref)
    run_ref[...] = jnp.zeros_like(run_ref)

    def tile(hd, r, d, diag):
        kb = blk0 + r - d
        start = pl.multiple_of(jnp.maximum(kb, 0) * sub, sub)
        k = k_ref[pl.ds(start, sub), :]
        v = jnp.where(in_head[hd], v_ref[pl.ds(start, sub), :], jnp.zeros((), BF16))
        z = lax.dot_general(q_chain[hd][r], k, _NT, preferred_element_type=F32)
        sp = jnp.maximum(z, 0.0) + jnp.log(1.0 + jnp.exp(-jnp.abs(z)))
        if diag:
            sp = jnp.where(col < row, sp, 0.0)
        suffix = _dot(sp.astype(BF16), tri_ref[...])
        run = run_ref[hd, r]
        w = jnp.exp(z - suffix - jnp.where(kb >= 0, run, 1e30))
        if diag:
            w = jnp.where(col < row, w, 0.0)
        acc_ref[hd, r] += _dot(w.astype(BF16), v)
        run_ref[hd, r] = run + suffix[:, 0:1]

    def more(d):
        go = jnp.bool_(False)
        for hd in range(2):
            for r in range(n_chain):
                unfinished = jnp.min(run_ref[hd, r]) < EXP_CUTOFF
                go = jnp.logical_or(go, jnp.logical_and(blk0 + r - d >= 0, unfinished))
        return go.astype(jnp.int32)

    def step(d, diag):
        for r in range(n_chain):
            for hd in range(2):
                tile(hd, r, d, diag)

    def body(carry):
        d, _ = carry
        step(d, False)
        return d + 1, more(d + 1)

    step(0, True)
    lax.while_loop(lambda carry: carry[1] > 0, body, (jnp.int32(1), more(1)))
    for r in range(n_chain):
        o_ref[r * sub:(r + 1) * sub, :] = (acc_ref[0, r] + acc_ref[1, r]).astype(o_ref.dtype)


def _sb_attention(qkv, tri):
    bsz, s, _ = qkv.shape
    tq = SB_SUB * SB_CHAINS
    return pl.pallas_call(
        _sb_kernel,
        grid=(bsz, N_PAIRS, s // tq),
        in_specs=[pl.BlockSpec((None, tq, LANES), lambda b, p, i: (b, i, p)),
                  pl.BlockSpec((None, s, LANES), lambda b, p, i: (b, 0, N_PAIRS + p)),
                  pl.BlockSpec((None, s, LANES), lambda b, p, i: (b, 0, 2 * N_PAIRS + p)),
                  pl.BlockSpec((SB_SUB, SB_SUB), lambda b, p, i: (0, 0))],
        out_specs=pl.BlockSpec((None, tq, LANES), lambda b, p, i: (b, i, p)),
        out_shape=jax.ShapeDtypeStruct((bsz, s, GROUP_DIM), BF16),
        scratch_shapes=[pltpu.VMEM((2, SB_CHAINS, SB_SUB, LANES), F32),
                        pltpu.VMEM((2, SB_CHAINS, SB_SUB, 1), F32)],
        compiler_params=_params(3),
        name="sb_attention",
    )(qkv, qkv, qkv, tri)


def _fox_kernel(q_ref, k_ref, v_ref, a_ref, crow_ref, cs_ref, zbmax_ref, o_ref, acc_ref, z_ref):
    sub, n_chain = FOX_SUB, FOX_CHAINS
    pair = pl.program_id(1)
    i = pl.program_id(2)
    lane = lax.broadcasted_iota(jnp.int32, (1, LANES), 1)
    in_head = (lane < HEAD_DIM, lane >= HEAD_DIM)
    one_lane = (HEAD_DIM, 0)

    q_all = q_ref[...] * jnp.asarray(HEAD_DIM ** -0.5, BF16)
    row = lax.broadcasted_iota(jnp.int32, (sub, sub), 0)
    col = lax.broadcasted_iota(jnp.int32, (sub, sub), 1)
    blk0 = i * n_chain
    acc_ref[...] = jnp.zeros_like(acc_ref)

    for hd in range(2):
        q_head = jnp.where(in_head[hd], q_all, jnp.zeros((), BF16))
        per_step = (sub * n_chain) // FOX_BOUND_ROWS
        zb_max = zbmax_ref[hd, i * per_step]
        for t in range(1, per_step):
            zb_max = jnp.maximum(zb_max, zbmax_ref[hd, i * per_step + t])
        cutoff = EXP_CUTOFF + 2.0 * zb_max
        c_base = cs_ref[hd, blk0 * sub]
        a_all = jnp.sum(jnp.where(lane == 2 * pair + hd, a_ref[...], 0.0), axis=1, keepdims=True) - c_base
        v_one = jnp.where(lane == one_lane[hd], 1.0, 0.0).astype(BF16)
        q_chain = [q_head[r * sub:(r + 1) * sub, :] for r in range(n_chain)]
        a_chain = [a_all[r * sub:(r + 1) * sub, :] for r in range(n_chain)]

        def tile_inputs(kb, hd=hd, c_base=c_base, v_one=v_one):
            start = pl.multiple_of(jnp.maximum(kb, 0) * sub, sub)
            k = k_ref[pl.ds(start, sub), :]
            v = jnp.where(in_head[hd], v_ref[pl.ds(start, sub), :], v_one)
            c_row = crow_ref[hd:hd + 1, pl.ds(start, sub)] - c_base
            return k, v, jnp.where(kb >= 0, c_row, 1e30)

        def more(d, cutoffs, hd=hd):
            go = jnp.bool_(False)
            for r in range(n_chain):
                kb = blk0 + r - d
                first = cs_ref[hd, (blk0 + r) * sub]
                last = cs_ref[hd, jnp.maximum(kb, 0) * sub + sub - 1]
                go = jnp.logical_or(go, jnp.logical_and(kb >= 0, first - last > -cutoffs[r]))
            return go

        def scores(r, d, q_chain=q_chain):
            start = pl.multiple_of(jnp.maximum(blk0 + r - d, 0) * sub, sub)
            return lax.dot_general(q_chain[r], k_ref[pl.ds(start, sub), :], _NT,
                                   preferred_element_type=F32)

        def fast_tile(r, d, z, diag, hd=hd, a_chain=a_chain, tile_inputs=tile_inputs):
            _, v, c_row = tile_inputs(blk0 + r - d)
            e = z + a_chain[r] - c_row
            if diag:
                e = jnp.where(col <= row, e, -1e30)
            acc_ref[hd, r] += _dot(jnp.exp(e).astype(BF16), v)
            return jnp.min(jnp.max(e, axis=1, keepdims=True)) if diag else None

        def slow_tile(r, d, m_old, diag, hd=hd, q_chain=q_chain, tile_inputs=tile_inputs):
            k, v, c_row = tile_inputs(blk0 + r - d)
            s = lax.dot_general(q_chain[r], k, _NT, preferred_element_type=F32) - c_row
            if diag:
                s = jnp.where(col <= row, s, -1e30)
            m_new = jnp.maximum(m_old, jnp.max(s, axis=1, keepdims=True))
            acc_ref[hd, r] = (acc_ref[hd, r] * jnp.exp(m_old - m_new)
                              + _dot(jnp.exp(s - m_new).astype(BF16), v))
            return m_new

        @pl.when(zb_max <= FOX_SAFE_BOUND)
        def _(fast_tile=fast_tile, scores=scores, more=more):
            cutoffs = []
            for r in range(n_chain):
                cutoffs.append(EXP_CUTOFF - fast_tile(r, 0, scores(r, 0), True))
                z_ref[r] = scores(r, 1)

            def body(d):
                for r in range(n_chain):
                    z = z_ref[r]
                    z_ref[r] = scores(r, d + 1)
                    fast_tile(r, d, z, False)
                return d + 1
            lax.while_loop(lambda d: more(d, cutoffs), body, jnp.int32(1))

        @pl.when(zb_max > FOX_SAFE_BOUND)
        def _(slow_tile=slow_tile, more=more):
            m0 = jnp.full((sub, 1), -1e30, F32)
            ms = tuple(slow_tile(r, 0, m0, True) for r in range(n_chain))

            def body(carry):
                d, ms = carry
                return d + 1, tuple(slow_tile(r, d, ms[r], False) for r in range(n_chain))
            lax.while_loop(lambda carry: more(carry[0], [cutoff] * n_chain), body, (jnp.int32(1), ms))

    for r in range(n_chain):
        acc = (acc_ref[0, r], acc_ref[1, r])
        total = [jnp.sum(jnp.where(lane == one_lane[hd], acc[hd], 0.0), axis=1, keepdims=True)
                 for hd in range(2)]
        out = jnp.where(in_head[0], acc[0] / total[0], acc[1] / total[1])
        o_ref[r * sub:(r + 1) * sub, :] = out.astype(o_ref.dtype)


def _fox_attention(qkv, a_cols, c_rows, zb_max):
    bsz, s, _ = qkv.shape
    tq = FOX_SUB * FOX_CHAINS
    return pl.pallas_call(
        _fox_kernel,
        grid=(bsz, N_PAIRS, s // tq),
        in_specs=[pl.BlockSpec((None, tq, LANES), lambda b, p, i: (b, i, 3 * N_PAIRS + p)),
                  pl.BlockSpec((None, s, LANES), lambda b, p, i: (b, 0, 4 * N_PAIRS + p)),
                  pl.BlockSpec((None, s, LANES), lambda b, p, i: (b, 0, 5 * N_PAIRS + p)),
                  pl.BlockSpec((None, tq, LANES), lambda b, p, i: (b, i, 0)),
                  pl.BlockSpec((None, None, 2, s), lambda b, p, i: (b, p, 0, 0)),
                  pl.BlockSpec((None, None, 2, s), lambda b, p, i: (b, p, 0, 0),
                               memory_space=pltpu.SMEM),
                  pl.BlockSpec((None, None, 2, s // FOX_BOUND_ROWS), lambda b, p, i: (b, p, 0, 0),
                               memory_space=pltpu.SMEM)],
        out_specs=pl.BlockSpec((None, tq, LANES), lambda b, p, i: (b, i, p)),
        out_shape=jax.ShapeDtypeStruct((bsz, s, GROUP_DIM), BF16),
        scratch_shapes=[pltpu.VMEM((2, FOX_CHAINS, FOX_SUB, LANES), F32),
                        pltpu.VMEM((FOX_CHAINS, FOX_SUB, FOX_SUB), F32)],
        compiler_params=_params(3),
        name="fox_attention",
    )(qkv, qkv, qkv, a_cols, c_rows, c_rows, zb_max)


def _pool(u, halo, pw_ref, ps_ref, pos0, tm):
    x = jnp.concatenate([halo, u], axis=0)
    s2 = x + pltpu.roll(x, 1, 0)
    s4 = s2 + pltpu.roll(s2, 2, 0)
    s8 = s4 + pltpu.roll(s4, 4, 0)
    s16 = s8 + pltpu.roll(s8, 8, 0)
    lane = lax.broadcasted_iota(jnp.int32, (1, POOL_DIM), 1)
    grp = POOL_DIM // len(POOL_WINDOWS)
    wsum = jnp.where(lane < grp, s2, jnp.where(lane < 2 * grp, s4, jnp.where(lane < 3 * grp, s8, s16)))
    win = jnp.where(lane < grp, 2, jnp.where(lane < 2 * grp, 4, jnp.where(lane < 3 * grp, 8, 16)))
    pos = pos0 + lax.broadcasted_iota(jnp.int32, (tm, 1), 0)
    count = jnp.minimum(pos + 1, win).astype(F32)
    r = wsum[POOL_HALO:, :] / count - u
    return _dot(r.astype(BF16), pw_ref[...]) * ps_ref[...]


def _mix_xattn_kernel(sb_ref, fox_ref, u_ref, halo_ref, pw_ref, ps_ref, wout_ref, gmix_ref, h_ref,
                      gpre_ref, wq_ref, k_ref, v_ref, wo_ref, gpost_ref, o_ref, *, tm, seq):
    pos0 = (pl.program_id(0) * tm) % seq
    halo = jnp.where(pos0 == 0, 0.0, halo_ref[...])
    pool = _pool(u_ref[...], halo, pw_ref, ps_ref, pos0, tm)
    gd = GROUP_DIM
    a = (_dot(sb_ref[...], wout_ref[0:gd, :]) + _dot(fox_ref[...], wout_ref[gd:2 * gd, :])
         + _dot(pool.astype(BF16), wout_ref[2 * gd:, :]))
    h = h_ref[...] + _rms(a, gmix_ref[...])

    d = h.shape[-1]
    hd = d // XA_HEADS
    hn = _rms(h, gpre_ref[...]).astype(BF16)
    q = (_dot(hn, wq_ref[...]) * (hd ** -0.5)).astype(BF16)
    outs = []
    for head in range(XA_HEADS):
        sl = slice(head * hd, (head + 1) * hd)
        s = lax.dot_general(q[:, sl], k_ref[:, sl], _NT, preferred_element_type=F32)
        p = jnp.exp(s - jnp.max(s, axis=1, keepdims=True))
        p = p * (1.0 / jnp.sum(p, axis=1, keepdims=True))
        outs.append(_dot(p.astype(BF16), v_ref[:, sl]).astype(BF16))
    c = _dot(jnp.concatenate(outs, axis=1), wo_ref[...])
    o_ref[...] = h + _rms(c, gpost_ref[...])


def _mix_xattn(sb, fox, u, pool_w_bd, pool_scale, w_out, g_mix, h, g_pre, wq, k_mem, v_mem, wo, g_post,
               seq, tm=512):
    n, d = h.shape
    m_len = k_mem.shape[1]
    per_seq = seq // tm
    hb = tm // POOL_HALO
    const = lambda shape: pl.BlockSpec(shape, lambda i: (0,) * len(shape))
    rows = lambda width: pl.BlockSpec((tm, width), lambda i: (i, 0))
    mem = pl.BlockSpec((None, m_len, d), lambda i: (i // per_seq, 0, 0))
    return pl.pallas_call(
        functools.partial(_mix_xattn_kernel, tm=tm, seq=seq),
        grid=(n // tm,),
        in_specs=[rows(GROUP_DIM), rows(GROUP_DIM), rows(POOL_DIM),
                  pl.BlockSpec((POOL_HALO, POOL_DIM), lambda i: (jnp.maximum(i * hb - 1, 0), 0)),
                  const((POOL_DIM, POOL_DIM)), const((1, POOL_DIM)), const(w_out.shape), const((1, d)),
                  rows(d), const((1, d)), const((d, d)), mem, mem, const((d, d)), const((1, d))],
        out_specs=rows(d),
        out_shape=jax.ShapeDtypeStruct((n, d), F32),
        compiler_params=_params(1),
        name="mix_xattn",
    )(sb, fox, u, u, pool_w_bd, pool_scale, w_out, g_mix, h, g_pre, wq, k_mem, v_mem, wo, g_post)


def _silu(x):
    return x / (1.0 + jnp.exp(-x))


def _swiglu(x, wg_ref, wu_ref, wd_ref, act_ref):
    ff = wg_ref.shape[1]
    for c in range(0, ff, FF_CHUNK):
        gate = _dot(x, wg_ref[:, c:c + FF_CHUNK])
        up = _dot(x, wu_ref[:, c:c + FF_CHUNK])
        act_ref[:, c:c + FF_CHUNK] = (_silu(gate) * up).astype(BF16)
    return _dot(act_ref[...], wd_ref[...])


def _ffn_kernel(h_ref, gpre_ref, wg_ref, wu_ref, wd_ref, gpost_ref, o_ref, act_ref):
    h = h_ref[...]
    f = _swiglu(_rms(h, gpre_ref[...]).astype(BF16), wg_ref, wu_ref, wd_ref, act_ref)
    o_ref[...] = h + _rms(f, gpost_ref[...])


def _ffn(h, g_pre, wg, wu, wd, g_post, tm=512):
    n, d = h.shape
    ff = wg.shape[1]
    resident = pl.Buffered(1)
    return pl.pallas_call(
        _ffn_kernel,
        grid=(n // tm,),
        in_specs=[pl.BlockSpec((tm, d), lambda i: (i, 0)),
                  pl.BlockSpec((1, d), lambda i: (0, 0)),
                  pl.BlockSpec((d, ff), lambda i: (0, 0), pipeline_mode=resident),
                  pl.BlockSpec((d, ff), lambda i: (0, 0), pipeline_mode=resident),
                  pl.BlockSpec((ff, d), lambda i: (0, 0), pipeline_mode=resident),
                  pl.BlockSpec((1, d), lambda i: (0, 0))],
        out_specs=pl.BlockSpec((tm, d), lambda i: (i, 0)),
        out_shape=jax.ShapeDtypeStruct((n, d), F32),
        scratch_shapes=[pltpu.VMEM((tm, ff), BF16)],
        compiler_params=_params(1),
        name="ffn_dense",
    )(h, g_pre, wg, wu, wd, g_post)


def _router_kernel(h_ref, g_ref, wr_ref, tri_ref, idx_ref, gate_ref, count_ref, carry_ref):
    @pl.when(pl.program_id(0) == 0)
    def _():
        carry_ref[...] = jnp.zeros_like(carry_ref)

    hn = _rms(h_ref[...], g_ref[...])
    x_hi, x_mid, _ = _split3(hn)
    w_hi, w_mid = wr_ref[0], wr_ref[1]
    logits = _dot(x_hi, w_hi) + _dot(x_hi, w_mid) + _dot(x_mid, w_hi)
    tm = logits.shape[0]
    lane = lax.broadcasted_iota(jnp.int32, logits.shape, 1)
    logits = jnp.where(lane < N_EXPERTS, logits, -jnp.inf)
    m1 = jnp.max(logits, axis=1, keepdims=True)
    i1 = jnp.min(jnp.where(logits == m1, lane, LANES), axis=1, keepdims=True)
    rest = jnp.where(lane == i1, -jnp.inf, logits)
    m2 = jnp.max(rest, axis=1, keepdims=True)
    i2 = jnp.min(jnp.where(rest == m2, lane, LANES), axis=1, keepdims=True)
    e = jnp.exp(m2 - m1)
    g1 = 1.0 / (1.0 + e)
    gate_ref[...] = jnp.where(lane == 0, g1, jnp.where(lane == 1, e * g1, 0.0))

    onehot = jnp.where(lane == i1, 1.0, jnp.where(lane == i2, 1.0, 0.0))
    before = _dot(tri_ref[...], onehot.astype(BF16)) + carry_ref[...]
    r1 = jnp.sum(jnp.where(lane == i1, before, 0.0), axis=1, keepdims=True).astype(jnp.int32)
    r2 = jnp.sum(jnp.where(lane == i2, before, 0.0), axis=1, keepdims=True).astype(jnp.int32)
    idx_ref[...] = jnp.where(lane == 0, i1, jnp.where(lane == 1, i2,
                             jnp.where(lane == 2, r1, jnp.where(lane == 3, r2, 0))))
    total = before[tm - 1:tm, :] + onehot[tm - 1:tm, :]
    carry_ref[...] = total
    count_ref[...] = total


def _router(h, g, wr_split, tm=512):
    n, d = h.shape
    t = jnp.arange(tm)
    tri = (t[:, None] > t[None, :]).astype(BF16)
    return pl.pallas_call(
        _router_kernel,
        grid=(n // tm,),
        in_specs=[pl.BlockSpec((tm, d), lambda i: (i, 0)),
                  pl.BlockSpec((1, d), lambda i: (0, 0)),
                  pl.BlockSpec(wr_split.shape, lambda i: (0, 0, 0)),
                  pl.BlockSpec((tm, tm), lambda i: (0, 0))],
        out_specs=[pl.BlockSpec((tm, LANES), lambda i: (i, 0)),
                   pl.BlockSpec((tm, LANES), lambda i: (i, 0)),
                   pl.BlockSpec((1, LANES), lambda i: (0, 0))],
        out_shape=[jax.ShapeDtypeStruct((n, LANES), jnp.int32),
                   jax.ShapeDtypeStruct((n, LANES), F32),
                   jax.ShapeDtypeStruct((1, LANES), F32)],
        scratch_shapes=[pltpu.VMEM((1, LANES), F32)],
        compiler_params=_params(1),
        name="router",
    )(h, g, wr_split, tri)


def _rows_to_tiles(ref, x):
    m = x.shape[0]
    for s in range(SUBLANES):
        ref[pl.ds(s, m, stride=SUBLANES), :] = x[:, s * LANES:(s + 1) * LANES]


def _tiles_to_rows(ref, m):
    return jnp.concatenate([ref[pl.ds(s, m, stride=SUBLANES), :] for s in range(SUBLANES)], axis=1)


def _tile_rows(ref, first_row, rows):
    return ref.at[pl.ds(pl.multiple_of(first_row * SUBLANES, SUBLANES), rows * SUBLANES)]


def _dispatch_kernel(dest_ref, pad_start_ref, pad_len_ref, n_valid_ref, h_ref, g_ref, xs_hbm,
                     buf_ref, zero_ref, sem, zero_sem, *, tm):
    base = pl.program_id(0) * tm * TOP_K

    @pl.when(pl.program_id(0) == 0)
    def _():
        zero_ref[...] = jnp.zeros_like(zero_ref)
        n_blocks = n_valid_ref.shape[0]

        def zero_copy(first_row, rows):
            return pltpu.make_async_copy(_tile_rows(zero_ref, 0, rows), _tile_rows(xs_hbm, first_row, rows),
                                         zero_sem)

        def fill(wait):
            def go(cp):
                cp.wait() if wait else cp.start()

            for e in range(N_EXPERTS):
                pos = pad_start_ref[e]
                left = pad_len_ref[e]
                rows = MOE_ROWS // 2
                while rows >= 1:
                    take = (left & rows) != 0

                    @pl.when(take)
                    def _(pos=pos, rows=rows):
                        go(zero_copy(pos, rows))
                    pos = pos + jnp.where(take, rows, 0)
                    rows //= 2

            def blocks(b, carry):
                @pl.when(n_valid_ref[b] == 0)
                def _():
                    go(zero_copy(b * MOE_ROWS, MOE_ROWS))
                return carry
            lax.fori_loop(0, n_blocks, blocks, 0)

        fill(False)
        fill(True)

    _rows_to_tiles(buf_ref, _rms(h_ref[...], g_ref[...]))

    def start(r, carry):
        for k in range(TOP_K):
            dst = dest_ref[base + r * TOP_K + k]
            pltpu.make_async_copy(_tile_rows(buf_ref, r, 1), _tile_rows(xs_hbm, dst, 1), sem).start()
        return carry

    lax.fori_loop(0, tm, start, 0, unroll=DMA_UNROLL)
    for _ in range(TOP_K):
        pltpu.make_async_copy(buf_ref, _tile_rows(xs_hbm, 0, tm), sem).wait()


def _dispatch(dest, pad_start, pad_len, n_valid, h, g, tm=512):
    n, d = h.shape
    assert d == SUBLANES * LANES
    n_rows = n_valid.shape[0] * MOE_ROWS
    grid_spec = pltpu.PrefetchScalarGridSpec(
        num_scalar_prefetch=4,
        grid=(n // tm,),
        in_specs=[pl.BlockSpec((tm, d), lambda i, *_: (i, 0)),
                  pl.BlockSpec((1, d), lambda i, *_: (0, 0))],
        out_specs=pl.BlockSpec(memory_space=pl.ANY),
        scratch_shapes=[pltpu.VMEM((tm * SUBLANES, LANES), F32), pltpu.VMEM((MOE_ROWS * SUBLANES, LANES), F32),
                        pltpu.SemaphoreType.DMA, pltpu.SemaphoreType.DMA],
    )
    return pl.pallas_call(
        functools.partial(_dispatch_kernel, tm=tm),
        grid_spec=grid_spec,
        out_shape=jax.ShapeDtypeStruct((n_rows * SUBLANES, LANES), F32),
        compiler_params=_params(1),
        name="moe_dispatch",
    )(dest, pad_start, pad_len, n_valid, h, g)


def _expert_kernel(blk_e_ref, n_valid_ref, xs_ref, wg_ref, wu_ref, wd_ref, ys_ref, act_ref):
    n_valid = n_valid_ref[pl.program_id(0)]

    @pl.when(n_valid > 0)
    def _():
        x = _tiles_to_rows(xs_ref, MOE_ROWS).astype(BF16)
        _rows_to_tiles(ys_ref, _swiglu(x, wg_ref, wu_ref, wd_ref, act_ref))

    @pl.when(n_valid == 0)
    def _():
        ys_ref[...] = jnp.zeros_like(ys_ref)


def _experts(blk_e, n_valid, xs, wg, wu, wd):
    d, ff = wg.shape[1], wg.shape[2]
    blk = MOE_ROWS * SUBLANES
    grid_spec = pltpu.PrefetchScalarGridSpec(
        num_scalar_prefetch=2,
        grid=(xs.shape[0] // blk,),
        in_specs=[pl.BlockSpec((blk, LANES), lambda i, be, nv: (i, 0)),
                  pl.BlockSpec((None, d, ff), lambda i, be, nv: (be[i], 0, 0)),
                  pl.BlockSpec((None, d, ff), lambda i, be, nv: (be[i], 0, 0)),
                  pl.BlockSpec((None, ff, d), lambda i, be, nv: (be[i], 0, 0))],
        out_specs=pl.BlockSpec((blk, LANES), lambda i, be, nv: (i, 0)),
        scratch_shapes=[pltpu.VMEM((MOE_ROWS, ff), BF16)],
    )
    return pl.pallas_call(
        _expert_kernel,
        grid_spec=grid_spec,
        out_shape=jax.ShapeDtypeStruct(xs.shape, F32),
        compiler_params=_params(1),
        name="moe_experts",
    )(blk_e, n_valid, xs, wg, wu, wd)


def _combine_kernel(dest_ref, ys_hbm, gate_ref, g_ref, h_ref, o_ref, buf_ref, sem, *, tm):
    i = pl.program_id(0)

    def fetch(tile, slot):
        def body(r, carry):
            for k in range(TOP_K):
                src = dest_ref[(tile * tm + r) * TOP_K + k]
                pltpu.make_async_copy(_tile_rows(ys_hbm, src, 1), _tile_rows(buf_ref.at[slot, k], r, 1),
                                      sem.at[slot]).start()
            return carry
        lax.fori_loop(0, tm, body, 0, unroll=DMA_UNROLL)

    @pl.when(i == 0)
    def _():
        fetch(0, 0)

    @pl.when(i + 1 < pl.num_programs(0))
    def _():
        fetch(i + 1, (i + 1) % 2)

    slot = i % 2

    for k in range(TOP_K):
        pltpu.make_async_copy(_tile_rows(ys_hbm, 0, tm), buf_ref.at[slot, k], sem.at[slot]).wait()
    gates = gate_ref[...]
    f = (_tiles_to_rows(buf_ref.at[slot, 0], tm) * gates[:, 0:1]
         + _tiles_to_rows(buf_ref.at[slot, 1], tm) * gates[:, 1:2])
    o_ref[...] = h_ref[...] + _rms(f, g_ref[...])


def _combine(dest, ys, gates, g, h, tm=512):
    n, d = h.shape
    grid_spec = pltpu.PrefetchScalarGridSpec(
        num_scalar_prefetch=1,
        grid=(n // tm,),
        in_specs=[pl.BlockSpec(memory_space=pl.ANY),
                  pl.BlockSpec((tm, LANES), lambda i, dst: (i, 0)),
                  pl.BlockSpec((1, d), lambda i, dst: (0, 0)),
                  pl.BlockSpec((tm, d), lambda i, dst: (i, 0))],
        out_specs=pl.BlockSpec((tm, d), lambda i, dst: (i, 0)),
        scratch_shapes=[pltpu.VMEM((2, TOP_K, tm * SUBLANES, LANES), F32), pltpu.SemaphoreType.DMA((2,))],
    )
    return pl.pallas_call(
        functools.partial(_combine_kernel, tm=tm),
        grid_spec=grid_spec,
        out_shape=jax.ShapeDtypeStruct((n, d), F32),
        compiler_params=_params(1),
        name="moe_combine",
    )(dest, ys, gates, g, h)


def _route_plan(idx, counts_f, n_tok):
    counts = counts_f[0, :N_EXPERTS].astype(jnp.int32)
    padded = ((counts + MOE_ROWS - 1) // MOE_ROWS) * MOE_ROWS
    pend = jnp.cumsum(padded)
    pstart = pend - padded
    expert = idx[:, 0:TOP_K]
    rank = idx[:, TOP_K:2 * TOP_K]
    offset = jnp.zeros_like(expert)
    for e in range(N_EXPERTS):
        offset = jnp.where(expert == e, pstart[e], offset)
    dest = (offset + rank).reshape(-1)
    n_blocks = (n_tok * TOP_K) // MOE_ROWS + N_EXPERTS
    blk_start = jnp.arange(n_blocks, dtype=jnp.int32) * MOE_ROWS
    blk_e = jnp.minimum(jnp.sum(blk_start[:, None] >= pend[None, :], axis=1), N_EXPERTS - 1).astype(jnp.int32)
    n_valid = jnp.clip(pstart[blk_e] + counts[blk_e] - blk_start, 0, MOE_ROWS).astype(jnp.int32)
    return dest, pstart + counts, padded - counts, blk_e, n_valid


def _moe(h, g_pre, router_w, wg, wu, wd, g_post):
    n, d = h.shape
    wr = jnp.pad(router_w, ((0, 0), (0, LANES - N_EXPERTS)))
    wr_hi = wr.astype(BF16)
    wr_mid = (wr - wr_hi.astype(F32)).astype(BF16)
    idx, gates, counts = _router(h, g_pre, jnp.stack([wr_hi, wr_mid]))
    dest, pad_start, pad_len, blk_e, n_valid = _route_plan(idx, counts, n)
    xs = _dispatch(dest, pad_start, pad_len, n_valid, h, g_pre)
    ys = _experts(blk_e, n_valid, xs, wg, wu, wd)
    return _combine(dest, ys, gates, g_post, h)


def kernel(x, mem, mix_norm_pre, mix_norm_post, w_in, b_forget, pool_w, pool_scale, w_out,
           xa_norm_pre, xa_norm_post, mem_norm, xa_wq, xa_wkv, xa_wo,
           ffn_norm_pre, ffn_norm_post, dense_w_gate, dense_w_up, dense_w_down,
           router_w, moe_w_gate, moe_w_up, moe_w_down):
    bsz, seq, d = x.shape
    m_len = mem.shape[1]
    depth = w_in.shape[0]
    n = bsz * seq
    h = x.reshape(n, d)
    mem2 = mem.reshape(bsz * m_len, d)
    row = lambda v: v.reshape(1, -1)

    idx = jnp.arange(SB_SUB)
    tri = (idx[:, None] >= idx[None, :]).astype(BF16)

    for li in range(depth):
        w = w_in[li]
        flog_w = jnp.pad(w[:, QKV_DIM:QKV_DIM + N_FOX], ((0, 0), (0, LANES - N_FOX)))
        w_cat = jnp.concatenate([w[:, :QKV_DIM], w[:, QKV_DIM + N_FOX:], flog_w], axis=1).astype(BF16)
        qkv, u, flog, qn, kmax_tiles = _in_proj(h, row(mix_norm_pre[li]), w_cat)
        qkv = qkv.reshape(bsz, seq, QKV_DIM)
        kmax = jnp.max(kmax_tiles[:, 0, :].reshape(bsz, -1, LANES), axis=1, keepdims=True)

        b_pad = jnp.pad(b_forget[li], (0, LANES - N_FOX)).reshape(1, LANES)
        by_seq = lambda v: v.reshape(bsz, seq, LANES)
        c, a_cols, zb_tiles = _logf_cumsum(by_seq(flog), b_pad, by_seq(qn), kmax)
        by_pair = lambda v: v[..., :N_FOX].transpose(0, 2, 1).reshape(bsz, N_PAIRS, 2, -1)
        c_rows = by_pair(c)
        zb_max = by_pair(zb_tiles[:, :, 0, :])

        sb = _sb_attention(qkv, tri).reshape(n, GROUP_DIM)
        fox = _fox_attention(qkv, a_cols, c_rows, zb_max).reshape(n, GROUP_DIM)

        pool_bd = jax.scipy.linalg.block_diag(*[pool_w[li, gi] for gi in range(len(POOL_WINDOWS))])
        k_mem, v_mem = _norm_matmul(mem2, row(mem_norm[li]), xa_wkv[li].astype(BF16),
                                    splits=[(0, d), (d, d)], dtypes=[BF16, BF16], tm=m_len)
        h = _mix_xattn(sb, fox, u, pool_bd.astype(BF16), row(pool_scale[li]), w_out[li].astype(BF16),
                       row(mix_norm_post[li]), h, row(xa_norm_pre[li]), xa_wq[li].astype(BF16),
                       k_mem.reshape(bsz, m_len, d), v_mem.reshape(bsz, m_len, d),
                       xa_wo[li].astype(BF16), row(xa_norm_post[li]), seq)

        j = li // 2
        if li % 2 == 0:
            h = _ffn(h, row(ffn_norm_pre[li]), dense_w_gate[j].astype(BF16), dense_w_up[j].astype(BF16),
                     dense_w_down[j].astype(BF16), row(ffn_norm_post[li]))
        else:
            h = _moe(h, row(ffn_norm_pre[li]), router_w[j], moe_w_gate[j].astype(BF16),
                     moe_w_up[j].astype(BF16), moe_w_down[j].astype(BF16), row(ffn_norm_post[li]))
    return h.reshape(bsz, seq, d)
```

```python
import functools

import jax
import jax.numpy as jnp
from jax import lax
from jax.experimental import pallas as pl
from jax.experimental.pallas import tpu as pltpu

F32 = jnp.float32
BF16 = jnp.bfloat16
EPS = 1e-6

HEAD_DIM = 64
LANES = 128
SUBLANES = 8
N_PAIRS = 3
GROUP_DIM = N_PAIRS * LANES
QKV_DIM = 6 * GROUP_DIM
POOL_DIM = 256
POOL_WINDOWS = (2, 4, 8, 16)
POOL_HALO = 16
N_FOX = 6
XA_HEADS = 4
N_EXPERTS = 8
TOP_K = 2
MOE_ROWS = 512
FF_CHUNK = 256
DMA_UNROLL = 8
SB_SUB = 256
SB_CHAINS = 4
FOX_SUB = 256
FOX_CHAINS = 8
FOX_SAFE_BOUND = 40.0
FOX_BOUND_ROWS = 1024
EXP_CUTOFF = 105.0
VMEM_LIMIT = 56 * 1024 * 1024

_NT = (((1,), (1,)), ((), ()))


def _params(n_axes):
    return pltpu.CompilerParams(dimension_semantics=("arbitrary",) * n_axes,
                                vmem_limit_bytes=VMEM_LIMIT)


def _rms(x, g):
    return x * lax.rsqrt(jnp.mean(x * x, axis=-1, keepdims=True) + EPS) * g


def _dot(a, b):
    return jnp.dot(a, b, preferred_element_type=F32)


def _split3(x):
    hi = x.astype(BF16)
    r1 = x - hi.astype(F32)
    mid = r1.astype(BF16)
    lo = (r1 - mid.astype(F32)).astype(BF16)
    return hi, mid, lo


def _norm_matmul_kernel(x_ref, g_ref, w_ref, *out_refs, splits):
    yb = _rms(x_ref[...], g_ref[...]).astype(BF16)
    for o_ref, (c0, width) in zip(out_refs, splits):
        for c in range(0, width, 256):
            cw = min(256, width - c)
            o_ref[:, c:c + cw] = _dot(yb, w_ref[:, c0 + c:c0 + c + cw]).astype(o_ref.dtype)


def _norm_matmul(x, g, w, splits, dtypes, tm):
    n, d = x.shape
    kern = functools.partial(_norm_matmul_kernel, splits=tuple(splits))
    return pl.pallas_call(
        kern,
        grid=(n // tm,),
        in_specs=[pl.BlockSpec((tm, d), lambda i: (i, 0)),
                  pl.BlockSpec((1, d), lambda i: (0, 0)),
                  pl.BlockSpec(w.shape, lambda i: (0, 0))],
        out_specs=[pl.BlockSpec((tm, wd), lambda i: (i, 0)) for (_, wd) in splits],
        out_shape=[jax.ShapeDtypeStruct((n, wd), dt) for (_, wd), dt in zip(splits, dtypes)],
        compiler_params=_params(1),
        name="norm_matmul",
    )(x, g, w)


def _in_proj_kernel(x_ref, g_ref, w_ref, sel_ref, qkv_ref, u_ref, fl_ref, qn_ref, kmax_ref):
    yb = _rms(x_ref[...], g_ref[...]).astype(BF16)
    for c in range(0, QKV_DIM, 256):
        qkv_ref[:, c:c + 256] = _dot(yb, w_ref[:, c:c + 256]).astype(BF16)
    u_ref[...] = _dot(yb, w_ref[:, QKV_DIM:QKV_DIM + POOL_DIM])
    fl_ref[...] = _dot(yb, w_ref[:, QKV_DIM + POOL_DIM:])

    def head_norms(c0):
        x = qkv_ref[:, c0:c0 + GROUP_DIM].astype(F32)
        sq = x * x
        hi = sq.astype(BF16)
        lo = (sq - hi.astype(F32)).astype(BF16)
        return _dot(hi, sel_ref[...]) + _dot(lo, sel_ref[...])

    qn_ref[...] = head_norms(3 * GROUP_DIM)
    kmax = jnp.max(head_norms(4 * GROUP_DIM), axis=0, keepdims=True)
    kmax_ref[...] = jnp.broadcast_to(kmax, kmax_ref.shape)


def _in_proj(x, g, w, tm=512):
    n, d = x.shape
    lanes = jnp.arange(GROUP_DIM)[:, None] // HEAD_DIM == jnp.arange(LANES)[None, :]
    rows = lambda width: pl.BlockSpec((tm, width), lambda i: (i, 0))
    return pl.pallas_call(
        _in_proj_kernel,
        grid=(n // tm,),
        in_specs=[rows(d), pl.BlockSpec((1, d), lambda i: (0, 0)), pl.BlockSpec(w.shape, lambda i: (0, 0)),
                  pl.BlockSpec((GROUP_DIM, LANES), lambda i: (0, 0))],
        out_specs=[rows(QKV_DIM), rows(POOL_DIM), rows(LANES), rows(LANES),
                   pl.BlockSpec((None, SUBLANES, LANES), lambda i: (i, 0, 0))],
        out_shape=[jax.ShapeDtypeStruct((n, QKV_DIM), BF16), jax.ShapeDtypeStruct((n, POOL_DIM), F32),
                   jax.ShapeDtypeStruct((n, LANES), F32), jax.ShapeDtypeStruct((n, LANES), F32),
                   jax.ShapeDtypeStruct((n // tm, SUBLANES, LANES), F32)],
        compiler_params=_params(1),
        name="in_proj",
    )(x, g, w, lanes.astype(BF16))


def _logf_cumsum_kernel(fl_ref, b_ref, qn_ref, kmax_ref, c_ref, a_ref, zbmax_ref, carry_ref, *, tc):
    @pl.when(pl.program_id(1) == 0)
    def _():
        carry_ref[...] = jnp.zeros_like(carry_ref)

    x = fl_ref[...] + b_ref[...]
    ls = jnp.minimum(x, 0.0) - jnp.log(1.0 + jnp.exp(-jnp.abs(x)))
    row = lax.broadcasted_iota(jnp.int32, (tc, tc), 0)
    col = lax.broadcasted_iota(jnp.int32, (tc, tc), 1)
    tri = jnp.where(row >= col, 1.0, 0.0).astype(BF16)
    hi, mid, lo = _split3(ls)
    c = _dot(tri, hi) + _dot(tri, mid) + _dot(tri, lo) + carry_ref[...]
    c_ref[...] = c
    carry_ref[...] = c[tc - 1:tc, :]
    zb = jnp.sqrt(qn_ref[...] * kmax_ref[...] * (1.0 / HEAD_DIM)) * 1.001
    a_ref[...] = c - zb
    zbmax_ref[...] = jnp.broadcast_to(jnp.max(zb, axis=0, keepdims=True), zbmax_ref.shape)


def _logf_cumsum(fl, b_pad, qn, kmax, tc=FOX_BOUND_ROWS):
    bsz, s, _ = fl.shape
    rows = pl.BlockSpec((None, tc, LANES), lambda b, j: (b, j, 0))
    return pl.pallas_call(
        functools.partial(_logf_cumsum_kernel, tc=tc),
        grid=(bsz, s // tc),
        in_specs=[rows, pl.BlockSpec((1, LANES), lambda b, j: (0, 0)), rows,
                  pl.BlockSpec((None, 1, LANES), lambda b, j: (b, 0, 0))],
        out_specs=[rows, rows, pl.BlockSpec((None, None, SUBLANES, LANES), lambda b, j: (b, j, 0, 0))],
        out_shape=[jax.ShapeDtypeStruct(fl.shape, F32), jax.ShapeDtypeStruct(fl.shape, F32),
                   jax.ShapeDtypeStruct((bsz, s // tc, SUBLANES, LANES), F32)],
        scratch_shapes=[pltpu.VMEM((1, LANES), F32)],
        compiler_params=_params(2),
        name="logf_cumsum",
    )(fl, b_pad, qn, kmax)


def _sb_kernel(q_ref, k_ref, v_ref, tri_ref, o_ref, acc_ref, run_ref):
    sub, n_chain = SB_SUB, SB_CHAINS
    blk0 = pl.program_id(2) * n_chain
    lane = lax.broadcasted_iota(jnp.int32, (1, LANES), 1)
    in_head = (lane < HEAD_DIM, lane >= HEAD_DIM)
    q_all = q_ref[...] * jnp.asarray(HEAD_DIM ** -0.5, BF16)
    q_chain = [[jnp.where(in_head[hd], q_all[r * sub:(r + 1) * sub, :], jnp.zeros((), BF16))
                for r in range(n_chain)] for hd in range(2)]
    row = lax.broadcasted_iota(jnp.int32, (sub, sub), 0)
    col = lax.broadcasted_iota(jnp.int32, (sub, sub), 1)
    acc_ref[...] = jnp.zeros_like(acc_ref)
    run_ref[...] = jnp.zeros_like(run_ref)

    def tile(hd, r, d, diag):
        kb = blk0 + r - d
        start = pl.multiple_of(jnp.maximum(kb, 0) * sub, sub)
        k = k_ref[pl.ds(start, sub), :]
        v = jnp.where(in_head[hd], v_ref[pl.ds(start, sub), :], jnp.zeros((), BF16))
        z = lax.dot_general(q_chain[hd][r], k, _NT, preferred_element_type=F32)
        sp = jnp.maximum(z, 0.0) + jnp.log(1.0 + jnp.exp(-jnp.abs(z)))
        if diag:
            sp = jnp.where(col < row, sp, 0.0)
        suffix = _dot(sp.astype(BF16), tri_ref[...])
        run = run_ref[hd, r]
        w = jnp.exp(z - suffix - jnp.where(kb >= 0, run, 1e30))
        if diag:
            w = jnp.where(col < row, w, 0.0)
        acc_ref[hd, r] += _dot(w.astype(BF16), v)
        run_ref[hd, r] = run + suffix[:, 0:1]

    def more(d):
        go = jnp.bool_(False)
        for hd in range(2):
            for r in range(n_chain):
                unfinished = jnp.min(run_ref[hd, r]) < EXP_CUTOFF
                go = jnp.logical_or(go, jnp.logical_and(blk0 + r - d >= 0, unfinished))
        return go.astype(jnp.int32)

    def step(d, diag):
        for r in range(n_chain):
            for hd in range(2):
                tile(hd, r, d, diag)

    def body(carry):
        d, _ = carry
        step(d, False)
        return d + 1, more(d + 1)

    step(0, True)
    lax.while_loop(lambda carry: carry[1] > 0, body, (jnp.int32(1), more(1)))
    for r in range(n_chain):
        o_ref[r * sub:(r + 1) * sub, :] = (acc_ref[0, r] + acc_ref[1, r]).astype(o_ref.dtype)


def _sb_attention(qkv, tri):
    bsz, s, _ = qkv.shape
    tq = SB_SUB * SB_CHAINS
    return pl.pallas_call(
        _sb_kernel,
        grid=(bsz, N_PAIRS, s // tq),
        in_specs=[pl.BlockSpec((None, tq, LANES), lambda b, p, i: (b, i, p)),
                  pl.BlockSpec((None, s, LANES), lambda b, p, i: (b, 0, N_PAIRS + p)),
                  pl.BlockSpec((None, s, LANES), lambda b, p, i: (b, 0, 2 * N_PAIRS + p)),
                  pl.BlockSpec((SB_SUB, SB_SUB), lambda b, p, i: (0, 0))],
        out_specs=pl.BlockSpec((None, tq, LANES), lambda b, p, i: (b, i, p)),
        out_shape=jax.ShapeDtypeStruct((bsz, s, GROUP_DIM), BF16),
        scratch_shapes=[pltpu.VMEM((2, SB_CHAINS, SB_SUB, LANES), F32),
                        pltpu.VMEM((2, SB_CHAINS, SB_SUB, 1), F32)],
        compiler_params=_params(3),
        name="sb_attention",
    )(qkv, qkv, qkv, tri)


def _fox_kernel(q_ref, k_ref, v_ref, a_ref, crow_ref, cs_ref, zbmax_ref, o_ref, acc_ref, z_ref):
    sub, n_chain = FOX_SUB, FOX_CHAINS
    pair = pl.program_id(1)
    i = pl.program_id(2)
    lane = lax.broadcasted_iota(jnp.int32, (1, LANES), 1)
    in_head = (lane < HEAD_DIM, lane >= HEAD_DIM)
    one_lane = (HEAD_DIM, 0)

    q_all = q_ref[...] * jnp.asarray(HEAD_DIM ** -0.5, BF16)
    row = lax.broadcasted_iota(jnp.int32, (sub, sub), 0)
    col = lax.broadcasted_iota(jnp.int32, (sub, sub), 1)
    blk0 = i * n_chain
    acc_ref[...] = jnp.zeros_like(acc_ref)

    for hd in range(2):
        q_head = jnp.where(in_head[hd], q_all, jnp.zeros((), BF16))
        per_step = (sub * n_chain) // FOX_BOUND_ROWS
        zb_max = zbmax_ref[hd, i * per_step]
        for t in range(1, per_step):
            zb_max = jnp.maximum(zb_max, zbmax_ref[hd, i * per_step + t])
        cutoff = EXP_CUTOFF + 2.0 * zb_max
        c_base = cs_ref[hd, blk0 * sub]
        a_all = jnp.sum(jnp.where(lane == 2 * pair + hd, a_ref[...], 0.0), axis=1, keepdims=True) - c_base
        v_one = jnp.where(lane == one_lane[hd], 1.0, 0.0).astype(BF16)
        q_chain = [q_head[r * sub:(r + 1) * sub, :] for r in range(n_chain)]
        a_chain = [a_all[r * sub:(r + 1) * sub, :] for r in range(n_chain)]

        def tile_inputs(kb, hd=hd, c_base=c_base, v_one=v_one):
            start = pl.multiple_of(jnp.maximum(kb, 0) * sub, sub)
            k = k_ref[pl.ds(start, sub), :]
            v = jnp.where(in_head[hd], v_ref[pl.ds(start, sub), :], v_one)
            c_row = crow_ref[hd:hd + 1, pl.ds(start, sub)] - c_base
            return k, v, jnp.where(kb >= 0, c_row, 1e30)

        def more(d, cutoffs, hd=hd):
            go = jnp.bool_(False)
            for r in range(n_chain):
                kb = blk0 + r - d
                first = cs_ref[hd, (blk0 + r) * sub]
                last = cs_ref[hd, jnp.maximum(kb, 0) * sub + sub - 1]
                go = jnp.logical_or(go, jnp.logical_and(kb >= 0, first - last > -cutoffs[r]))
            return go

        def scores(r, d, q_chain=q_chain):
            start = pl.multiple_of(jnp.maximum(blk0 + r - d, 0) * sub, sub)
            return lax.dot_general(q_chain[r], k_ref[pl.ds(start, sub), :], _NT,
                                   preferred_element_type=F32)

        def fast_tile(r, d, z, diag, hd=hd, a_chain=a_chain, tile_inputs=tile_inputs):
            _, v, c_row = tile_inputs(blk0 + r - d)
            e = z + a_chain[r] - c_row
            if diag:
                e = jnp.where(col <= row, e, -1e30)
            acc_ref[hd, r] += _dot(jnp.exp(e).astype(BF16), v)
            return jnp.min(jnp.max(e, axis=1, keepdims=True)) if diag else None

        def slow_tile(r, d, m_old, diag, hd=hd, q_chain=q_chain, tile_inputs=tile_inputs):
            k, v, c_row = tile_inputs(blk0 + r - d)
            s = lax.dot_general(q_chain[r], k, _NT, preferred_element_type=F32) - c_row
            if diag:
                s = jnp.where(col <= row, s, -1e30)
            m_new = jnp.maximum(m_old, jnp.max(s, axis=1, keepdims=True))
            acc_ref[hd, r] = (acc_ref[hd, r] * jnp.exp(m_old - m_new)
                              + _dot(jnp.exp(s - m_new).astype(BF16), v))
            return m_new

        @pl.when(zb_max <= FOX_SAFE_BOUND)
        def _(fast_tile=fast_tile, scores=scores, more=more):
            cutoffs = []
            for r in range(n_chain):
                cutoffs.append(EXP_CUTOFF - fast_tile(r, 0, scores(r, 0), True))
                z_ref[r] = scores(r, 1)

            def body(d):
                for r in range(n_chain):
                    z = z_ref[r]
                    z_ref[r] = scores(r, d + 1)
                    fast_tile(r, d, z, False)
                return d + 1
            lax.while_loop(lambda d: more(d, cutoffs), body, jnp.int32(1))

        @pl.when(zb_max > FOX_SAFE_BOUND)
        def _(slow_tile=slow_tile, more=more):
            m0 = jnp.full((sub, 1), -1e30, F32)
            ms = tuple(slow_tile(r, 0, m0, True) for r in range(n_chain))

            def body(carry):
                d, ms = carry
                return d + 1, tuple(slow_tile(r, d, ms[r], False) for r in range(n_chain))
            lax.while_loop(lambda carry: more(carry[0], [cutoff] * n_chain), body, (jnp.int32(1), ms))

    for r in range(n_chain):
        acc = (acc_ref[0, r], acc_ref[1, r])
        total = [jnp.sum(jnp.where(lane == one_lane[hd], acc[hd], 0.0), axis=1, keepdims=True)
                 for hd in range(2)]
        out = jnp.where(in_head[0], acc[0] / total[0], acc[1] / total[1])
        o_ref[r * sub:(r + 1) * sub, :] = out.astype(o_ref.dtype)


def _fox_attention(qkv, a_cols, c_rows, zb_max):
    bsz, s, _ = qkv.shape
    tq = FOX_SUB * FOX_CHAINS
    return pl.pallas_call(
        _fox_kernel,
        grid=(bsz, N_PAIRS, s // tq),
        in_specs=[pl.BlockSpec((None, tq, LANES), lambda b, p, i: (b, i, 3 * N_PAIRS + p)),
                  pl.BlockSpec((None, s, LANES), lambda b, p, i: (b, 0, 4 * N_PAIRS + p)),
                  pl.BlockSpec((None, s, LANES), lambda b, p, i: (b, 0, 5 * N_PAIRS + p)),
                  pl.BlockSpec((None, tq, LANES), lambda b, p, i: (b, i, 0)),
                  pl.BlockSpec((None, None, 2, s), lambda b, p, i: (b, p, 0, 0)),
                  pl.BlockSpec((None, None, 2, s), lambda b, p, i: (b, p, 0, 0),
                               memory_space=pltpu.SMEM),
                  pl.BlockSpec((None, None, 2, s // FOX_BOUND_ROWS), lambda b, p, i: (b, p, 0, 0),
                               memory_space=pltpu.SMEM)],
        out_specs=pl.BlockSpec((None, tq, LANES), lambda b, p, i: (b, i, p)),
        out_shape=jax.ShapeDtypeStruct((bsz, s, GROUP_DIM), BF16),
        scratch_shapes=[pltpu.VMEM((2, FOX_CHAINS, FOX_SUB, LANES), F32),
                        pltpu.VMEM((FOX_CHAINS, FOX_SUB, FOX_SUB), F32)],
        compiler_params=_params(3),
        name="fox_attention",
    )(qkv, qkv, qkv, a_cols, c_rows, c_rows, zb_max)


def _pool(u, halo, pw_ref, ps_ref, pos0, tm):
    x = jnp.concatenate([halo, u], axis=0)
    s2 = x + pltpu.roll(x, 1, 0)
    s4 = s2 + pltpu.roll(s2, 2, 0)
    s8 = s4 + pltpu.roll(s4, 4, 0)
    s16 = s8 + pltpu.roll(s8, 8, 0)
    lane = lax.broadcasted_iota(jnp.int32, (1, POOL_DIM), 1)
    grp = POOL_DIM // len(POOL_WINDOWS)
    wsum = jnp.where(lane < grp, s2, jnp.where(lane < 2 * grp, s4, jnp.where(lane < 3 * grp, s8, s16)))
    win = jnp.where(lane < grp, 2, jnp.where(lane < 2 * grp, 4, jnp.where(lane < 3 * grp, 8, 16)))
    pos = pos0 + lax.broadcasted_iota(jnp.int32, (tm, 1), 0)
    count = jnp.minimum(pos + 1, win).astype(F32)
    r = wsum[POOL_HALO:, :] / count - u
    return _dot(r.astype(BF16), pw_ref[...]) * ps_ref[...]


def _mix_xattn_kernel(sb_ref, fox_ref, u_ref, halo_ref, pw_ref, ps_ref, wout_ref, gmix_ref, h_ref,
                      gpre_ref, wq_ref, k_ref, v_ref, wo_ref, gpost_ref, o_ref, *, tm, seq):
    pos0 = (pl.program_id(0) * tm) % seq
    halo = jnp.where(pos0 == 0, 0.0, halo_ref[...])
    pool = _pool(u_ref[...], halo, pw_ref, ps_ref, pos0, tm)
    gd = GROUP_DIM
    a = (_dot(sb_ref[...], wout_ref[0:gd, :]) + _dot(fox_ref[...], wout_ref[gd:2 * gd, :])
         + _dot(pool.astype(BF16), wout_ref[2 * gd:, :]))
    h = h_ref[...] + _rms(a, gmix_ref[...])

    d = h.shape[-1]
    hd = d // XA_HEADS
    hn = _rms(h, gpre_ref[...]).astype(BF16)
    q = (_dot(hn, wq_ref[...]) * (hd ** -0.5)).astype(BF16)
    outs = []
    for head in range(XA_HEADS):
        sl = slice(head * hd, (head + 1) * hd)
        s = lax.dot_general(q[:, sl], k_ref[:, sl], _NT, preferred_element_type=F32)
        p = jnp.exp(s - jnp.max(s, axis=1, keepdims=True))
        p = p * (1.0 / jnp.sum(p, axis=1, keepdims=True))
        outs.append(_dot(p.astype(BF16), v_ref[:, sl]).astype(BF16))
    c = _dot(jnp.concatenate(outs, axis=1), wo_ref[...])
    o_ref[...] = h + _rms(c, gpost_ref[...])


def _mix_xattn(sb, fox, u, pool_w_bd, pool_scale, w_out, g_mix, h, g_pre, wq, k_mem, v_mem, wo, g_post,
               seq, tm=512):
    n, d = h.shape
    m_len = k_mem.shape[1]
    per_seq = seq // tm
    hb = tm // POOL_HALO
    const = lambda shape: pl.BlockSpec(shape, lambda i: (0,) * len(shape))
    rows = lambda width: pl.BlockSpec((tm, width), lambda i: (i, 0))
    mem = pl.BlockSpec((None, m_len, d), lambda i: (i // per_seq, 0, 0))
    return pl.pallas_call(
        functools.partial(_mix_xattn_kernel, tm=tm, seq=seq),
        grid=(n // tm,),
        in_specs=[rows(GROUP_DIM), rows(GROUP_DIM), rows(POOL_DIM),
                  pl.BlockSpec((POOL_HALO, POOL_DIM), lambda i: (jnp.maximum(i * hb - 1, 0), 0)),
                  const((POOL_DIM, POOL_DIM)), const((1, POOL_DIM)), const(w_out.shape), const((1, d)),
                  rows(d), const((1, d)), const((d, d)), mem, mem, const((d, d)), const((1, d))],
        out_specs=rows(d),
        out_shape=jax.ShapeDtypeStruct((n, d), F32),
        compiler_params=_params(1),
        name="mix_xattn",
    )(sb, fox, u, u, pool_w_bd, pool_scale, w_out, g_mix, h, g_pre, wq, k_mem, v_mem, wo, g_post)


def _silu(x):
    return x / (1.0 + jnp.exp(-x))


def _swiglu(x, wg_ref, wu_ref, wd_ref, act_ref):
    ff = wg_ref.shape[1]
    for c in range(0, ff, FF_CHUNK):
        gate = _dot(x, wg_ref[:, c:c + FF_CHUNK])
        up = _dot(x, wu_ref[:, c:c + FF_CHUNK])
        act_ref[:, c:c + FF_CHUNK] = (_silu(gate) * up).astype(BF16)
    return _dot(act_ref[...], wd_ref[...])


def _ffn_kernel(h_ref, gpre_ref, wg_ref, wu_ref, wd_ref, gpost_ref, o_ref, act_ref):
    h = h_ref[...]
    f = _swiglu(_rms(h, gpre_ref[...]).astype(BF16), wg_ref, wu_ref, wd_ref, act_ref)
    o_ref[...] = h + _rms(f, gpost_ref[...])


def _ffn(h, g_pre, wg, wu, wd, g_post, tm=512):
    n, d = h.shape
    ff = wg.shape[1]
    resident = pl.Buffered(1)
    return pl.pallas_call(
        _ffn_kernel,
        grid=(n // tm,),
        in_specs=[pl.BlockSpec((tm, d), lambda i: (i, 0)),
                  pl.BlockSpec((1, d), lambda i: (0, 0)),
                  pl.BlockSpec((d, ff), lambda i: (0, 0), pipeline_mode=resident),
                  pl.BlockSpec((d, ff), lambda i: (0, 0), pipeline_mode=resident),
                  pl.BlockSpec((ff, d), lambda i: (0, 0), pipeline_mode=resident),
                  pl.BlockSpec((1, d), lambda i: (0, 0))],
        out_specs=pl.BlockSpec((tm, d), lambda i: (i, 0)),
        out_shape=jax.ShapeDtypeStruct((n, d), F32),
        scratch_shapes=[pltpu.VMEM((tm, ff), BF16)],
        compiler_params=_params(1),
        name="ffn_dense",
    )(h, g_pre, wg, wu, wd, g_post)


def _router_kernel(h_ref, g_ref, wr_ref, tri_ref, idx_ref, gate_ref, count_ref, carry_ref):
    @pl.when(pl.program_id(0) == 0)
    def _():
        carry_ref[...] = jnp.zeros_like(carry_ref)

    hn = _rms(h_ref[...], g_ref[...])
    x_hi, x_mid, _ = _split3(hn)
    w_hi, w_mid = wr_ref[0], wr_ref[1]
    logits = _dot(x_hi, w_hi) + _dot(x_hi, w_mid) + _dot(x_mid, w_hi)
    tm = logits.shape[0]
    lane = lax.broadcasted_iota(jnp.int32, logits.shape, 1)
    logits = jnp.where(lane < N_EXPERTS, logits, -jnp.inf)
    m1 = jnp.max(logits, axis=1, keepdims=True)
    i1 = jnp.min(jnp.where(logits == m1, lane, LANES), axis=1, keepdims=True)
    rest = jnp.where(lane == i1, -jnp.inf, logits)
    m2 = jnp.max(rest, axis=1, keepdims=True)
    i2 = jnp.min(jnp.where(rest == m2, lane, LANES), axis=1, keepdims=True)
    e = jnp.exp(m2 - m1)
    g1 = 1.0 / (1.0 + e)
    gate_ref[...] = jnp.where(lane == 0, g1, jnp.where(lane == 1, e * g1, 0.0))

    onehot = jnp.where(lane == i1, 1.0, jnp.where(lane == i2, 1.0, 0.0))
    before = _dot(tri_ref[...], onehot.astype(BF16)) + carry_ref[...]
    r1 = jnp.sum(jnp.where(lane == i1, before, 0.0), axis=1, keepdims=True).astype(jnp.int32)
    r2 = jnp.sum(jnp.where(lane == i2, before, 0.0), axis=1, keepdims=True).astype(jnp.int32)
    idx_ref[...] = jnp.where(lane == 0, i1, jnp.where(lane == 1, i2,
                             jnp.where(lane == 2, r1, jnp.where(lane == 3, r2, 0))))
    total = before[tm - 1:tm, :] + onehot[tm - 1:tm, :]
    carry_ref[...] = total
    count_ref[...] = total


def _router(h, g, wr_split, tm=512):
    n, d = h.shape
    t = jnp.arange(tm)
    tri = (t[:, None] > t[None, :]).astype(BF16)
    return pl.pallas_call(
        _router_kernel,
        grid=(n // tm,),
        in_specs=[pl.BlockSpec((tm, d), lambda i: (i, 0)),
                  pl.BlockSpec((1, d), lambda i: (0, 0)),
                  pl.BlockSpec(wr_split.shape, lambda i: (0, 0, 0)),
                  pl.BlockSpec((tm, tm), lambda i: (0, 0))],
        out_specs=[pl.BlockSpec((tm, LANES), lambda i: (i, 0)),
                   pl.BlockSpec((tm, LANES), lambda i: (i, 0)),
                   pl.BlockSpec((1, LANES), lambda i: (0, 0))],
        out_shape=[jax.ShapeDtypeStruct((n, LANES), jnp.int32),
                   jax.ShapeDtypeStruct((n, LANES), F32),
                   jax.ShapeDtypeStruct((1, LANES), F32)],
        scratch_shapes=[pltpu.VMEM((1, LANES), F32)],
        compiler_params=_params(1),
        name="router",
    )(h, g, wr_split, tri)


def _rows_to_tiles(ref, x):
    m = x.shape[0]
    for s in range(SUBLANES):
        ref[pl.ds(s, m, stride=SUBLANES), :] = x[:, s * LANES:(s + 1) * LANES]


def _tiles_to_rows(ref, m):
    return jnp.concatenate([ref[pl.ds(s, m, stride=SUBLANES), :] for s in range(SUBLANES)], axis=1)


def _tile_rows(ref, first_row, rows):
    return ref.at[pl.ds(pl.multiple_of(first_row * SUBLANES, SUBLANES), rows * SUBLANES)]


def _dispatch_kernel(dest_ref, pad_start_ref, pad_len_ref, n_valid_ref, h_ref, g_ref, xs_hbm,
                     buf_ref, zero_ref, sem, zero_sem, *, tm):
    base = pl.program_id(0) * tm * TOP_K

    @pl.when(pl.program_id(0) == 0)
    def _():
        zero_ref[...] = jnp.zeros_like(zero_ref)
        n_blocks = n_valid_ref.shape[0]

        def zero_copy(first_row, rows):
            return pltpu.make_async_copy(_tile_rows(zero_ref, 0, rows), _tile_rows(xs_hbm, first_row, rows),
                                         zero_sem)

        def fill(wait):
            def go(cp):
                cp.wait() if wait else cp.start()

            for e in range(N_EXPERTS):
                pos = pad_start_ref[e]
                left = pad_len_ref[e]
                rows = MOE_ROWS // 2
                while rows >= 1:
                    take = (left & rows) != 0

                    @pl.when(take)
                    def _(pos=pos, rows=rows):
                        go(zero_copy(pos, rows))
                    pos = pos + jnp.where(take, rows, 0)
                    rows //= 2

            def blocks(b, carry):
                @pl.when(n_valid_ref[b] == 0)
                def _():
                    go(zero_copy(b * MOE_ROWS, MOE_ROWS))
                return carry
            lax.fori_loop(0, n_blocks, blocks, 0)

        fill(False)
        fill(True)

    _rows_to_tiles(buf_ref, _rms(h_ref[...], g_ref[...]))

    def start(r, carry):
        for k in range(TOP_K):
            dst = dest_ref[base + r * TOP_K + k]
            pltpu.make_async_copy(_tile_rows(buf_ref, r, 1), _tile_rows(xs_hbm, dst, 1), sem).start(priority=k)
        return carry

    lax.fori_loop(0, tm, start, 0, unroll=DMA_UNROLL)
    for _ in range(TOP_K):
        pltpu.make_async_copy(buf_ref, _tile_rows(xs_hbm, 0, tm), sem).wait()


def _dispatch(dest, pad_start, pad_len, n_valid, h, g, tm=512):
    n, d = h.shape
    assert d == SUBLANES * LANES
    n_rows = n_valid.shape[0] * MOE_ROWS
    grid_spec = pltpu.PrefetchScalarGridSpec(
        num_scalar_prefetch=4,
        grid=(n // tm,),
        in_specs=[pl.BlockSpec((tm, d), lambda i, *_: (i, 0)),
                  pl.BlockSpec((1, d), lambda i, *_: (0, 0))],
        out_specs=pl.BlockSpec(memory_space=pl.ANY),
        scratch_shapes=[pltpu.VMEM((tm * SUBLANES, LANES), F32), pltpu.VMEM((MOE_ROWS * SUBLANES, LANES), F32),
                        pltpu.SemaphoreType.DMA, pltpu.SemaphoreType.DMA],
    )
    return pl.pallas_call(
        functools.partial(_dispatch_kernel, tm=tm),
        grid_spec=grid_spec,
        out_shape=jax.ShapeDtypeStruct((n_rows * SUBLANES, LANES), F32),
        compiler_params=_params(1),
        name="moe_dispatch",
    )(dest, pad_start, pad_len, n_valid, h, g)


def _expert_kernel(blk_e_ref, n_valid_ref, xs_ref, wg_ref, wu_ref, wd_ref, ys_ref, act_ref):
    n_valid = n_valid_ref[pl.program_id(0)]

    @pl.when(n_valid > 0)
    def _():
        x = _tiles_to_rows(xs_ref, MOE_ROWS).astype(BF16)
        _rows_to_tiles(ys_ref, _swiglu(x, wg_ref, wu_ref, wd_ref, act_ref))

    @pl.when(n_valid == 0)
    def _():
        ys_ref[...] = jnp.zeros_like(ys_ref)


def _experts(blk_e, n_valid, xs, wg, wu, wd):
    d, ff = wg.shape[1], wg.shape[2]
    blk = MOE_ROWS * SUBLANES
    grid_spec = pltpu.PrefetchScalarGridSpec(
        num_scalar_prefetch=2,
        grid=(xs.shape[0] // blk,),
        in_specs=[pl.BlockSpec((blk, LANES), lambda i, be, nv: (i, 0)),
                  pl.BlockSpec((None, d, ff), lambda i, be, nv: (be[i], 0, 0)),
                  pl.BlockSpec((None, d, ff), lambda i, be, nv: (be[i], 0, 0)),
                  pl.BlockSpec((None, ff, d), lambda i, be, nv: (be[i], 0, 0))],
        out_specs=pl.BlockSpec((blk, LANES), lambda i, be, nv: (i, 0)),
        scratch_shapes=[pltpu.VMEM((MOE_ROWS, ff), BF16)],
    )
    return pl.pallas_call(
        _expert_kernel,
        grid_spec=grid_spec,
        out_shape=jax.ShapeDtypeStruct(xs.shape, F32),
        compiler_params=_params(1),
        name="moe_experts",
    )(blk_e, n_valid, xs, wg, wu, wd)


def _combine_kernel(dest_ref, ys_hbm, gate_ref, g_ref, h_ref, o_ref, buf_ref, sem, *, tm):
    i = pl.program_id(0)

    def fetch(tile, slot):
        def body(r, carry):
            for k in range(TOP_K):
                src = dest_ref[(tile * tm + r) * TOP_K + k]
                pltpu.make_async_copy(_tile_rows(ys_hbm, src, 1), _tile_rows(buf_ref.at[slot, k], r, 1),
                                      sem.at[slot]).start(priority=k)
            return carry
        lax.fori_loop(0, tm, body, 0, unroll=DMA_UNROLL)

    @pl.when(i == 0)
    def _():
        fetch(0, 0)

    @pl.when(i + 1 < pl.num_programs(0))
    def _():
        fetch(i + 1, (i + 1) % 2)

    slot = i % 2

    for k in range(TOP_K):
        pltpu.make_async_copy(_tile_rows(ys_hbm, 0, tm), buf_ref.at[slot, k], sem.at[slot]).wait()
    gates = gate_ref[...]
    f = (_tiles_to_rows(buf_ref.at[slot, 0], tm) * gates[:, 0:1]
         + _tiles_to_rows(buf_ref.at[slot, 1], tm) * gates[:, 1:2])
    o_ref[...] = h_ref[...] + _rms(f, g_ref[...])


def _combine(dest, ys, gates, g, h, tm=512):
    n, d = h.shape
    grid_spec = pltpu.PrefetchScalarGridSpec(
        num_scalar_prefetch=1,
        grid=(n // tm,),
        in_specs=[pl.BlockSpec(memory_space=pl.ANY),
                  pl.BlockSpec((tm, LANES), lambda i, dst: (i, 0)),
                  pl.BlockSpec((1, d), lambda i, dst: (0, 0)),
                  pl.BlockSpec((tm, d), lambda i, dst: (i, 0))],
        out_specs=pl.BlockSpec((tm, d), lambda i, dst: (i, 0)),
        scratch_shapes=[pltpu.VMEM((2, TOP_K, tm * SUBLANES, LANES), F32), pltpu.SemaphoreType.DMA((2,))],
    )
    return pl.pallas_call(
        functools.partial(_combine_kernel, tm=tm),
        grid_spec=grid_spec,
        out_shape=jax.ShapeDtypeStruct((n, d), F32),
        compiler_params=_params(1),
        name="moe_combine",
    )(dest, ys, gates, g, h)


def _route_plan(idx, counts_f, n_tok):
    counts = counts_f[0, :N_EXPERTS].astype(jnp.int32)
    padded = ((counts + MOE_ROWS - 1) // MOE_ROWS) * MOE_ROWS
    pend = jnp.cumsum(padded)
    pstart = pend - padded
    expert = idx[:, 0:TOP_K]
    rank = idx[:, TOP_K:2 * TOP_K]
    offset = jnp.zeros_like(expert)
    for e in range(N_EXPERTS):
        offset = jnp.where(expert == e, pstart[e], offset)
    dest = (offset + rank).reshape(-1)
    n_blocks = (n_tok * TOP_K) // MOE_ROWS + N_EXPERTS
    blk_start = jnp.arange(n_blocks, dtype=jnp.int32) * MOE_ROWS
    blk_e = jnp.minimum(jnp.sum(blk_start[:, None] >= pend[None, :], axis=1), N_EXPERTS - 1).astype(jnp.int32)
    n_valid = jnp.clip(pstart[blk_e] + counts[blk_e] - blk_start, 0, MOE_ROWS).astype(jnp.int32)
    return dest, pstart + counts, padded - counts, blk_e, n_valid


def _moe(h, g_pre, router_w, wg, wu, wd, g_post):
    n, d = h.shape
    wr = jnp.pad(router_w, ((0, 0), (0, LANES - N_EXPERTS)))
    wr_hi = wr.astype(BF16)
    wr_mid = (wr - wr_hi.astype(F32)).astype(BF16)
    idx, gates, counts = _router(h, g_pre, jnp.stack([wr_hi, wr_mid]))
    dest, pad_start, pad_len, blk_e, n_valid = _route_plan(idx, counts, n)
    xs = _dispatch(dest, pad_start, pad_len, n_valid, h, g_pre)
    ys = _experts(blk_e, n_valid, xs, wg, wu, wd)
    return _combine(dest, ys, gates, g_post, h)


def kernel(x, mem, mix_norm_pre, mix_norm_post, w_in, b_forget, pool_w, pool_scale, w_out,
           xa_norm_pre, xa_norm_post, mem_norm, xa_wq, xa_wkv, xa_wo,
           ffn_norm_pre, ffn_norm_post, dense_w_gate, dense_w_up, dense_w_down,
           router_w, moe_w_gate, moe_w_up, moe_w_down):
    bsz, seq, d = x.shape
    m_len = mem.shape[1]
    depth = w_in.shape[0]
    n = bsz * seq
    h = x.reshape(n, d)
    mem2 = mem.reshape(bsz * m_len, d)
    row = lambda v: v.reshape(1, -1)

    idx = jnp.arange(SB_SUB)
    tri = (idx[:, None] >= idx[None, :]).astype(BF16)

    for li in range(depth):
        w = w_in[li]
        flog_w = jnp.pad(w[:, QKV_DIM:QKV_DIM + N_FOX], ((0, 0), (0, LANES - N_FOX)))
        w_cat = jnp.concatenate([w[:, :QKV_DIM], w[:, QKV_DIM + N_FOX:], flog_w], axis=1).astype(BF16)
        qkv, u, flog, qn, kmax_tiles = _in_proj(h, row(mix_norm_pre[li]), w_cat)
        qkv = qkv.reshape(bsz, seq, QKV_DIM)
        kmax = jnp.max(kmax_tiles[:, 0, :].reshape(bsz, -1, LANES), axis=1, keepdims=True)

        b_pad = jnp.pad(b_forget[li], (0, LANES - N_FOX)).reshape(1, LANES)
        by_seq = lambda v: v.reshape(bsz, seq, LANES)
        c, a_cols, zb_tiles = _logf_cumsum(by_seq(flog), b_pad, by_seq(qn), kmax)
        by_pair = lambda v: v[..., :N_FOX].transpose(0, 2, 1).reshape(bsz, N_PAIRS, 2, -1)
        c_rows = by_pair(c)
        zb_max = by_pair(zb_tiles[:, :, 0, :])

        sb = _sb_attention(qkv, tri).reshape(n, GROUP_DIM)
        fox = _fox_attention(qkv, a_cols, c_rows, zb_max).reshape(n, GROUP_DIM)

        pool_bd = jax.scipy.linalg.block_diag(*[pool_w[li, gi] for gi in range(len(POOL_WINDOWS))])
        k_mem, v_mem = _norm_matmul(mem2, row(mem_norm[li]), xa_wkv[li].astype(BF16),
                                    splits=[(0, d), (d, d)], dtypes=[BF16, BF16], tm=m_len)
        h = _mix_xattn(sb, fox, u, pool_bd.astype(BF16), row(pool_scale[li]), w_out[li].astype(BF16),
                       row(mix_norm_post[li]), h, row(xa_norm_pre[li]), xa_wq[li].astype(BF16),
                       k_mem.reshape(bsz, m_len, d), v_mem.reshape(bsz, m_len, d),
                       xa_wo[li].astype(BF16), row(xa_norm_post[li]), seq)

        j = li // 2
        if li % 2 == 0:
            h = _ffn(h, row(ffn_norm_pre[li]), dense_w_gate[j].astype(BF16), dense_w_up[j].astype(BF16),
                     dense_w_down[j].astype(BF16), row(ffn_norm_post[li]))
        else:
            h = _moe(h, row(ffn_norm_pre[li]), router_w[j], moe_w_gate[j].astype(BF16),
                     moe_w_up[j].astype(BF16), moe_w_down[j].astype(BF16), row(ffn_norm_post[li]))
    return h.reshape(bsz, seq, d)
```

```python
import functools

import jax
import jax.numpy as jnp
from jax import lax
from jax.experimental import pallas as pl
from jax.experimental.pallas import tpu as pltpu

F32 = jnp.float32
BF16 = jnp.bfloat16
EPS = 1e-6

HEAD_DIM = 64
LANES = 128
SUBLANES = 8
N_PAIRS = 3
GROUP_DIM = N_PAIRS * LANES
QKV_DIM = 6 * GROUP_DIM
POOL_DIM = 256
POOL_WINDOWS = (2, 4, 8, 16)
POOL_HALO = 16
N_FOX = 6
XA_HEADS = 4
N_EXPERTS = 8
TOP_K = 2
MOE_ROWS = 512
FF_CHUNK = 256
DMA_UNROLL = 8
SB_SUB = 256
SB_CHAINS = 8
FOX_SUB = 256
FOX_CHAINS = 8
FOX_SAFE_BOUND = 40.0
FOX_BOUND_ROWS = 1024
EXP_CUTOFF = 105.0
VMEM_LIMIT = 56 * 1024 * 1024

_NT = (((1,), (1,)), ((), ()))


def _params(n_axes):
    return pltpu.CompilerParams(dimension_semantics=("arbitrary",) * n_axes,
                                vmem_limit_bytes=VMEM_LIMIT)


def _rms(x, g):
    return x * lax.rsqrt(jnp.mean(x * x, axis=-1, keepdims=True) + EPS) * g


def _dot(a, b):
    return jnp.dot(a, b, preferred_element_type=F32)


def _split3(x):
    hi = x.astype(BF16)
    r1 = x - hi.astype(F32)
    mid = r1.astype(BF16)
    lo = (r1 - mid.astype(F32)).astype(BF16)
    return hi, mid, lo


def _norm_matmul_kernel(x_ref, g_ref, w_ref, *out_refs, splits):
    yb = _rms(x_ref[...], g_ref[...]).astype(BF16)
    for o_ref, (c0, width) in zip(out_refs, splits):
        for c in range(0, width, 256):
            cw = min(256, width - c)
            o_ref[:, c:c + cw] = _dot(yb, w_ref[:, c0 + c:c0 + c + cw]).astype(o_ref.dtype)


def _norm_matmul(x, g, w, splits, dtypes, tm):
    n, d = x.shape
    kern = functools.partial(_norm_matmul_kernel, splits=tuple(splits))
    return pl.pallas_call(
        kern,
        grid=(n // tm,),
        in_specs=[pl.BlockSpec((tm, d), lambda i: (i, 0)),
                  pl.BlockSpec((1, d), lambda i: (0, 0)),
                  pl.BlockSpec(w.shape, lambda i: (0, 0))],
        out_specs=[pl.BlockSpec((tm, wd), lambda i: (i, 0)) for (_, wd) in splits],
        out_shape=[jax.ShapeDtypeStruct((n, wd), dt) for (_, wd), dt in zip(splits, dtypes)],
        compiler_params=_params(1),
        name="norm_matmul",
    )(x, g, w)


def _in_proj_kernel(x_ref, g_ref, w_ref, sel_ref, qkv_ref, u_ref, fl_ref, qn_ref, kmax_ref):
    yb = _rms(x_ref[...], g_ref[...]).astype(BF16)
    for c in range(0, QKV_DIM, 256):
        qkv_ref[:, c:c + 256] = _dot(yb, w_ref[:, c:c + 256]).astype(BF16)
    u_ref[...] = _dot(yb, w_ref[:, QKV_DIM:QKV_DIM + POOL_DIM])
    fl_ref[...] = _dot(yb, w_ref[:, QKV_DIM + POOL_DIM:])

    def head_norms(c0):
        x = qkv_ref[:, c0:c0 + GROUP_DIM].astype(F32)
        sq = x * x
        hi = sq.astype(BF16)
        lo = (sq - hi.astype(F32)).astype(BF16)
        return _dot(hi, sel_ref[...]) + _dot(lo, sel_ref[...])

    qn_ref[...] = head_norms(3 * GROUP_DIM)
    kmax = jnp.max(head_norms(4 * GROUP_DIM), axis=0, keepdims=True)
    kmax_ref[...] = jnp.broadcast_to(kmax, kmax_ref.shape)


def _in_proj(x, g, w, tm=1024):
    n, d = x.shape
    lanes = jnp.arange(GROUP_DIM)[:, None] // HEAD_DIM == jnp.arange(LANES)[None, :]
    rows = lambda width: pl.BlockSpec((tm, width), lambda i: (i, 0))
    return pl.pallas_call(
        _in_proj_kernel,
        grid=(n // tm,),
        in_specs=[rows(d), pl.BlockSpec((1, d), lambda i: (0, 0)), pl.BlockSpec(w.shape, lambda i: (0, 0)),
                  pl.BlockSpec((GROUP_DIM, LANES), lambda i: (0, 0))],
        out_specs=[rows(QKV_DIM), rows(POOL_DIM), rows(LANES), rows(LANES),
                   pl.BlockSpec((None, SUBLANES, LANES), lambda i: (i, 0, 0))],
        out_shape=[jax.ShapeDtypeStruct((n, QKV_DIM), BF16), jax.ShapeDtypeStruct((n, POOL_DIM), F32),
                   jax.ShapeDtypeStruct((n, LANES), F32), jax.ShapeDtypeStruct((n, LANES), F32),
                   jax.ShapeDtypeStruct((n // tm, SUBLANES, LANES), F32)],
        compiler_params=_params(1),
        name="in_proj",
    )(x, g, w, lanes.astype(BF16))


def _logf_cumsum_kernel(fl_ref, b_ref, qn_ref, kmax_ref, c_ref, a_ref, zbmax_ref, carry_ref, *, tc):
    @pl.when(pl.program_id(1) == 0)
    def _():
        carry_ref[...] = jnp.zeros_like(carry_ref)

    x = fl_ref[...] + b_ref[...]
    ls = jnp.minimum(x, 0.0) - jnp.log(1.0 + jnp.exp(-jnp.abs(x)))
    row = lax.broadcasted_iota(jnp.int32, (tc, tc), 0)
    col = lax.broadcasted_iota(jnp.int32, (tc, tc), 1)
    tri = jnp.where(row >= col, 1.0, 0.0).astype(BF16)
    hi, mid, lo = _split3(ls)
    c = _dot(tri, hi) + _dot(tri, mid) + _dot(tri, lo) + carry_ref[...]
    c_ref[...] = c
    carry_ref[...] = c[tc - 1:tc, :]
    zb = jnp.sqrt(qn_ref[...] * kmax_ref[...] * (1.0 / HEAD_DIM)) * 1.001
    a_ref[...] = c - zb
    zbmax_ref[...] = jnp.broadcast_to(jnp.max(zb, axis=0, keepdims=True), zbmax_ref.shape)


def _logf_cumsum(fl, b_pad, qn, kmax, tc=FOX_BOUND_ROWS):
    bsz, s, _ = fl.shape
    rows = pl.BlockSpec((None, tc, LANES), lambda b, j: (b, j, 0))
    return pl.pallas_call(
        functools.partial(_logf_cumsum_kernel, tc=tc),
        grid=(bsz, s // tc),
        in_specs=[rows, pl.BlockSpec((1, LANES), lambda b, j: (0, 0)), rows,
                  pl.BlockSpec((None, 1, LANES), lambda b, j: (b, 0, 0))],
        out_specs=[rows, rows, pl.BlockSpec((None, None, SUBLANES, LANES), lambda b, j: (b, j, 0, 0))],
        out_shape=[jax.ShapeDtypeStruct(fl.shape, F32), jax.ShapeDtypeStruct(fl.shape, F32),
                   jax.ShapeDtypeStruct((bsz, s // tc, SUBLANES, LANES), F32)],
        scratch_shapes=[pltpu.VMEM((1, LANES), F32)],
        compiler_params=_params(2),
        name="logf_cumsum",
    )(fl, b_pad, qn, kmax)


def _sb_kernel(q_ref, k_ref, v_ref, tri_ref, o_ref, acc_ref, run_ref):
    sub, n_chain = SB_SUB, SB_CHAINS
    blk0 = pl.program_id(2) * n_chain
    lane = lax.broadcasted_iota(jnp.int32, (1, LANES), 1)
    in_head = (lane < HEAD_DIM, lane >= HEAD_DIM)
    q_all = q_ref[...] * jnp.asarray(HEAD_DIM ** -0.5, BF16)
    q_chain = [[jnp.where(in_head[hd], q_all[r * sub:(r + 1) * sub, :], jnp.zeros((), BF16))
                for r in range(n_chain)] for hd in range(2)]
    row = lax.broadcasted_iota(jnp.int32, (sub, sub), 0)
    col = lax.broadcasted_iota(jnp.int32, (sub, sub), 1)
    acc_ref[...] = jnp.zeros_like(acc_ref)
    run_ref[...] = jnp.zeros_like(run_ref)

    def tile(hd, r, d, diag):
        kb = blk0 + r - d
        start = pl.multiple_of(jnp.maximum(kb, 0) * sub, sub)
        k = k_ref[pl.ds(start, sub), :]
        v = jnp.where(in_head[hd], v_ref[pl.ds(start, sub), :], jnp.zeros((), BF16))
        z = lax.dot_general(q_chain[hd][r], k, _NT, preferred_element_type=F32)
        sp = jnp.maximum(z, 0.0) + jnp.log(1.0 + jnp.exp(-jnp.abs(z)))
        if diag:
            sp = jnp.where(col < row, sp, 0.0)
        suffix = _dot(sp.astype(BF16), tri_ref[...])
        run = run_ref[hd, r]
        w = jnp.exp(z - suffix - jnp.where(kb >= 0, run, 1e30))
        if diag:
            w = jnp.where(col < row, w, 0.0)
        acc_ref[hd, r] += _dot(w.astype(BF16), v)
        run_ref[hd, r] = run + suffix[:, 0:1]

    def more(d):
        go = jnp.bool_(False)
        for hd in range(2):
            for r in range(n_chain):
                unfinished = jnp.min(run_ref[hd, r]) < EXP_CUTOFF
                go = jnp.logical_or(go, jnp.logical_and(blk0 + r - d >= 0, unfinished))
        return go.astype(jnp.int32)

    def step(d, diag):
        for r in range(n_chain):
            for hd in range(2):
                tile(hd, r, d, diag)

    def body(carry):
        d, _ = carry
        step(d, False)
        return d + 1, more(d + 1)

    step(0, True)
    lax.while_loop(lambda carry: carry[1] > 0, body, (jnp.int32(1), more(1)))
    for r in range(n_chain):
        o_ref[r * sub:(r + 1) * sub, :] = (acc_ref[0, r] + acc_ref[1, r]).astype(o_ref.dtype)


def _sb_attention(qkv, tri):
    bsz, s, _ = qkv.shape
    tq = SB_SUB * SB_CHAINS
    return pl.pallas_call(
        _sb_kernel,
        grid=(bsz, N_PAIRS, s // tq),
        in_specs=[pl.BlockSpec((None, tq, LANES), lambda b, p, i: (b, i, p)),
                  pl.BlockSpec((None, s, LANES), lambda b, p, i: (b, 0, N_PAIRS + p)),
                  pl.BlockSpec((None, s, LANES), lambda b, p, i: (b, 0, 2 * N_PAIRS + p)),
                  pl.BlockSpec((SB_SUB, SB_SUB), lambda b, p, i: (0, 0))],
        out_specs=pl.BlockSpec((None, tq, LANES), lambda b, p, i: (b, i, p)),
        out_shape=jax.ShapeDtypeStruct((bsz, s, GROUP_DIM), BF16),
        scratch_shapes=[pltpu.VMEM((2, SB_CHAINS, SB_SUB, LANES), F32),
                        pltpu.VMEM((2, SB_CHAINS, SB_SUB, 1), F32)],
        compiler_params=_params(3),
        name="sb_attention",
    )(qkv, qkv, qkv, tri)


def _fox_kernel(q_ref, k_ref, v_ref, a_ref, crow_ref, cs_ref, zbmax_ref, o_ref, acc_ref, z_ref):
    sub, n_chain = FOX_SUB, FOX_CHAINS
    pair = pl.program_id(1)
    i = pl.program_id(2)
    lane = lax.broadcasted_iota(jnp.int32, (1, LANES), 1)
    in_head = (lane < HEAD_DIM, lane >= HEAD_DIM)
    one_lane = (HEAD_DIM, 0)

    q_all = q_ref[...] * jnp.asarray(HEAD_DIM ** -0.5, BF16)
    row = lax.broadcasted_iota(jnp.int32, (sub, sub), 0)
    col = lax.broadcasted_iota(jnp.int32, (sub, sub), 1)
    blk0 = i * n_chain
    acc_ref[...] = jnp.zeros_like(acc_ref)

    for hd in range(2):
        q_head = jnp.where(in_head[hd], q_all, jnp.zeros((), BF16))
        per_step = (sub * n_chain) // FOX_BOUND_ROWS
        zb_max = zbmax_ref[hd, i * per_step]
        for t in range(1, per_step):
            zb_max = jnp.maximum(zb_max, zbmax_ref[hd, i * per_step + t])
        cutoff = EXP_CUTOFF + 2.0 * zb_max
        c_base = cs_ref[hd, blk0 * sub]
        a_all = jnp.sum(jnp.where(lane == 2 * pair + hd, a_ref[...], 0.0), axis=1, keepdims=True) - c_base
        v_one = jnp.where(lane == one_lane[hd], 1.0, 0.0).astype(BF16)
        q_chain = [q_head[r * sub:(r + 1) * sub, :] for r in range(n_chain)]
        a_chain = [a_all[r * sub:(r + 1) * sub, :] for r in range(n_chain)]

        def tile_inputs(kb, hd=hd, c_base=c_base, v_one=v_one):
            start = pl.multiple_of(jnp.maximum(kb, 0) * sub, sub)
            k = k_ref[pl.ds(start, sub), :]
            v = jnp.where(in_head[hd], v_ref[pl.ds(start, sub), :], v_one)
            c_row = crow_ref[hd:hd + 1, pl.ds(start, sub)] - c_base
            return k, v, jnp.where(kb >= 0, c_row, 1e30)

        def more(d, cutoffs, hd=hd):
            go = jnp.bool_(False)
            for r in range(n_chain):
                kb = blk0 + r - d
                first = cs_ref[hd, (blk0 + r) * sub]
                last = cs_ref[hd, jnp.maximum(kb, 0) * sub + sub - 1]
                go = jnp.logical_or(go, jnp.logical_and(kb >= 0, first - last > -cutoffs[r]))
            return go

        def scores(r, d, q_chain=q_chain):
            start = pl.multiple_of(jnp.maximum(blk0 + r - d, 0) * sub, sub)
            return lax.dot_general(q_chain[r], k_ref[pl.ds(start, sub), :], _NT,
                                   preferred_element_type=F32)

        def fast_tile(r, d, z, diag, hd=hd, a_chain=a_chain, tile_inputs=tile_inputs):
            _, v, c_row = tile_inputs(blk0 + r - d)
            e = z + a_chain[r] - c_row
            if diag:
                e = jnp.where(col <= row, e, -1e30)
            acc_ref[hd, r] += _dot(jnp.exp(e).astype(BF16), v)
            return jnp.min(jnp.max(e, axis=1, keepdims=True)) if diag else None

        def slow_tile(r, d, m_old, diag, hd=hd, q_chain=q_chain, tile_inputs=tile_inputs):
            k, v, c_row = tile_inputs(blk0 + r - d)
            s = lax.dot_general(q_chain[r], k, _NT, preferred_element_type=F32) - c_row
            if diag:
                s = jnp.where(col <= row, s, -1e30)
            m_new = jnp.maximum(m_old, jnp.max(s, axis=1, keepdims=True))
            acc_ref[hd, r] = (acc_ref[hd, r] * jnp.exp(m_old - m_new)
                              + _dot(jnp.exp(s - m_new).astype(BF16), v))
            return m_new

        @pl.when(zb_max <= FOX_SAFE_BOUND)
        def _(fast_tile=fast_tile, scores=scores, more=more):
            cutoffs = []
            for r in range(n_chain):
                cutoffs.append(EXP_CUTOFF - fast_tile(r, 0, scores(r, 0), True))
                z_ref[r] = scores(r, 1)

            def body(d):
                for r in range(n_chain):
                    z = z_ref[r]
                    z_ref[r] = scores(r, d + 1)
                    fast_tile(r, d, z, False)
                return d + 1
            lax.while_loop(lambda d: more(d, cutoffs), body, jnp.int32(1))

        @pl.when(zb_max > FOX_SAFE_BOUND)
        def _(slow_tile=slow_tile, more=more):
            m0 = jnp.full((sub, 1), -1e30, F32)
            ms = tuple(slow_tile(r, 0, m0, True) for r in range(n_chain))

            def body(carry):
                d, ms = carry
                return d + 1, tuple(slow_tile(r, d, ms[r], False) for r in range(n_chain))
            lax.while_loop(lambda carry: more(carry[0], [cutoff] * n_chain), body, (jnp.int32(1), ms))

    for r in range(n_chain):
        acc = (acc_ref[0, r], acc_ref[1, r])
        total = [jnp.sum(jnp.where(lane == one_lane[hd], acc[hd], 0.0), axis=1, keepdims=True)
                 for hd in range(2)]
        out = jnp.where(in_head[0], acc[0] / total[0], acc[1] / total[1])
        o_ref[r * sub:(r + 1) * sub, :] = out.astype(o_ref.dtype)


def _fox_attention(qkv, a_cols, c_rows, zb_max):
    bsz, s, _ = qkv.shape
    tq = FOX_SUB * FOX_CHAINS
    return pl.pallas_call(
        _fox_kernel,
        grid=(bsz, N_PAIRS, s // tq),
        in_specs=[pl.BlockSpec((None, tq, LANES), lambda b, p, i: (b, i, 3 * N_PAIRS + p)),
                  pl.BlockSpec((None, s, LANES), lambda b, p, i: (b, 0, 4 * N_PAIRS + p)),
                  pl.BlockSpec((None, s, LANES), lambda b, p, i: (b, 0, 5 * N_PAIRS + p)),
                  pl.BlockSpec((None, tq, LANES), lambda b, p, i: (b, i, 0)),
                  pl.BlockSpec((None, None, 2, s), lambda b, p, i: (b, p, 0, 0)),
                  pl.BlockSpec((None, None, 2, s), lambda b, p, i: (b, p, 0, 0),
                               memory_space=pltpu.SMEM),
                  pl.BlockSpec((None, None, 2, s // FOX_BOUND_ROWS), lambda b, p, i: (b, p, 0, 0),
                               memory_space=pltpu.SMEM)],
        out_specs=pl.BlockSpec((None, tq, LANES), lambda b, p, i: (b, i, p)),
        out_shape=jax.ShapeDtypeStruct((bsz, s, GROUP_DIM), BF16),
        scratch_shapes=[pltpu.VMEM((2, FOX_CHAINS, FOX_SUB, LANES), F32),
                        pltpu.VMEM((FOX_CHAINS, FOX_SUB, FOX_SUB), F32)],
        compiler_params=_params(3),
        name="fox_attention",
    )(qkv, qkv, qkv, a_cols, c_rows, c_rows, zb_max)


def _pool(u, halo, pw_ref, ps_ref, pos0, tm):
    x = jnp.concatenate([halo, u], axis=0)
    s2 = x + pltpu.roll(x, 1, 0)
    s4 = s2 + pltpu.roll(s2, 2, 0)
    s8 = s4 + pltpu.roll(s4, 4, 0)
    s16 = s8 + pltpu.roll(s8, 8, 0)
    lane = lax.broadcasted_iota(jnp.int32, (1, POOL_DIM), 1)
    grp = POOL_DIM // len(POOL_WINDOWS)
    wsum = jnp.where(lane < grp, s2, jnp.where(lane < 2 * grp, s4, jnp.where(lane < 3 * grp, s8, s16)))
    win = jnp.where(lane < grp, 2, jnp.where(lane < 2 * grp, 4, jnp.where(lane < 3 * grp, 8, 16)))
    pos = pos0 + lax.broadcasted_iota(jnp.int32, (tm, 1), 0)
    count = jnp.minimum(pos + 1, win).astype(F32)
    r = wsum[POOL_HALO:, :] / count - u
    return _dot(r.astype(BF16), pw_ref[...]) * ps_ref[...]


def _mix_xattn_kernel(sb_ref, fox_ref, u_ref, halo_ref, pw_ref, ps_ref, wout_ref, gmix_ref, h_ref,
                      gpre_ref, wq_ref, k_ref, v_ref, wo_ref, gpost_ref, o_ref, *, tm, seq):
    pos0 = (pl.program_id(0) * tm) % seq
    halo = jnp.where(pos0 == 0, 0.0, halo_ref[...])
    pool = _pool(u_ref[...], halo, pw_ref, ps_ref, pos0, tm)
    gd = GROUP_DIM
    a = (_dot(sb_ref[...], wout_ref[0:gd, :]) + _dot(fox_ref[...], wout_ref[gd:2 * gd, :])
         + _dot(pool.astype(BF16), wout_ref[2 * gd:, :]))
    h = h_ref[...] + _rms(a, gmix_ref[...])

    d = h.shape[-1]
    hd = d // XA_HEADS
    hn = _rms(h, gpre_ref[...]).astype(BF16)
    q = (_dot(hn, wq_ref[...]) * (hd ** -0.5)).astype(BF16)
    outs = []
    for head in range(XA_HEADS):
        sl = slice(head * hd, (head + 1) * hd)
        s = lax.dot_general(q[:, sl], k_ref[:, sl], _NT, preferred_element_type=F32)
        p = jnp.exp(s - jnp.max(s, axis=1, keepdims=True))
        p = p * (1.0 / jnp.sum(p, axis=1, keepdims=True))
        outs.append(_dot(p.astype(BF16), v_ref[:, sl]).astype(BF16))
    c = _dot(jnp.concatenate(outs, axis=1), wo_ref[...])
    o_ref[...] = h + _rms(c, gpost_ref[...])


def _mix_xattn(sb, fox, u, pool_w_bd, pool_scale, w_out, g_mix, h, g_pre, wq, k_mem, v_mem, wo, g_post,
               seq, tm=1024):
    n, d = h.shape
    m_len = k_mem.shape[1]
    per_seq = seq // tm
    hb = tm // POOL_HALO
    const = lambda shape: pl.BlockSpec(shape, lambda i: (0,) * len(shape))
    rows = lambda width: pl.BlockSpec((tm, width), lambda i: (i, 0))
    mem = pl.BlockSpec((None, m_len, d), lambda i: (i // per_seq, 0, 0))
    return pl.pallas_call(
        functools.partial(_mix_xattn_kernel, tm=tm, seq=seq),
        grid=(n // tm,),
        in_specs=[rows(GROUP_DIM), rows(GROUP_DIM), rows(POOL_DIM),
                  pl.BlockSpec((POOL_HALO, POOL_DIM), lambda i: (jnp.maximum(i * hb - 1, 0), 0)),
                  const((POOL_DIM, POOL_DIM)), const((1, POOL_DIM)), const(w_out.shape), const((1, d)),
                  rows(d), const((1, d)), const((d, d)), mem, mem, const((d, d)), const((1, d))],
        out_specs=rows(d),
        out_shape=jax.ShapeDtypeStruct((n, d), F32),
        compiler_params=_params(1),
        name="mix_xattn",
    )(sb, fox, u, u, pool_w_bd, pool_scale, w_out, g_mix, h, g_pre, wq, k_mem, v_mem, wo, g_post)


def _silu(x):
    return x / (1.0 + jnp.exp(-x))


def _swiglu(x, wg_ref, wu_ref, wd_ref, act_ref):
    ff = wg_ref.shape[1]
    for c in range(0, ff, FF_CHUNK):
        gate = _dot(x, wg_ref[:, c:c + FF_CHUNK])
        up = _dot(x, wu_ref[:, c:c + FF_CHUNK])
        act_ref[:, c:c + FF_CHUNK] = (_silu(gate) * up).astype(BF16)
    return _dot(act_ref[...], wd_ref[...])


def _ffn_kernel(h_ref, gpre_ref, wg_ref, wu_ref, wd_ref, gpost_ref, o_ref, act_ref):
    h = h_ref[...]
    f = _swiglu(_rms(h, gpre_ref[...]).astype(BF16), wg_ref, wu_ref, wd_ref, act_ref)
    o_ref[...] = h + _rms(f, gpost_ref[...])


def _ffn(h, g_pre, wg, wu, wd, g_post, tm=512):
    n, d = h.shape
    ff = wg.shape[1]
    resident = pl.Buffered(1)
    return pl.pallas_call(
        _ffn_kernel,
        grid=(n // tm,),
        in_specs=[pl.BlockSpec((tm, d), lambda i: (i, 0)),
                  pl.BlockSpec((1, d), lambda i: (0, 0)),
                  pl.BlockSpec((d, ff), lambda i: (0, 0), pipeline_mode=resident),
                  pl.BlockSpec((d, ff), lambda i: (0, 0), pipeline_mode=resident),
                  pl.BlockSpec((ff, d), lambda i: (0, 0), pipeline_mode=resident),
                  pl.BlockSpec((1, d), lambda i: (0, 0))],
        out_specs=pl.BlockSpec((tm, d), lambda i: (i, 0)),
        out_shape=jax.ShapeDtypeStruct((n, d), F32),
        scratch_shapes=[pltpu.VMEM((tm, ff), BF16)],
        compiler_params=_params(1),
        name="ffn_dense",
    )(h, g_pre, wg, wu, wd, g_post)


def _router_kernel(h_ref, g_ref, wr_ref, tri_ref, idx_ref, gate_ref, count_ref, carry_ref):
    @pl.when(pl.program_id(0) == 0)
    def _():
        carry_ref[...] = jnp.zeros_like(carry_ref)

    hn = _rms(h_ref[...], g_ref[...])
    x_hi, x_mid, _ = _split3(hn)
    w_hi, w_mid = wr_ref[0], wr_ref[1]
    logits = _dot(x_hi, w_hi) + _dot(x_hi, w_mid) + _dot(x_mid, w_hi)
    tm = logits.shape[0]
    lane = lax.broadcasted_iota(jnp.int32, logits.shape, 1)
    logits = jnp.where(lane < N_EXPERTS, logits, -jnp.inf)
    m1 = jnp.max(logits, axis=1, keepdims=True)
    i1 = jnp.min(jnp.where(logits == m1, lane, LANES), axis=1, keepdims=True)
    rest = jnp.where(lane == i1, -jnp.inf, logits)
    m2 = jnp.max(rest, axis=1, keepdims=True)
    i2 = jnp.min(jnp.where(rest == m2, lane, LANES), axis=1, keepdims=True)
    e = jnp.exp(m2 - m1)
    g1 = 1.0 / (1.0 + e)
    gate_ref[...] = jnp.where(lane == 0, g1, jnp.where(lane == 1, e * g1, 0.0))

    onehot = jnp.where(lane == i1, 1.0, jnp.where(lane == i2, 1.0, 0.0))
    before = _dot(tri_ref[...], onehot.astype(BF16)) + carry_ref[...]
    r1 = jnp.sum(jnp.where(lane == i1, before, 0.0), axis=1, keepdims=True).astype(jnp.int32)
    r2 = jnp.sum(jnp.where(lane == i2, before, 0.0), axis=1, keepdims=True).astype(jnp.int32)
    idx_ref[...] = jnp.where(lane == 0, i1, jnp.where(lane == 1, i2,
                             jnp.where(lane == 2, r1, jnp.where(lane == 3, r2, 0))))
    total = before[tm - 1:tm, :] + onehot[tm - 1:tm, :]
    carry_ref[...] = total
    count_ref[...] = total


def _router(h, g, wr_split, tm=512):
    n, d = h.shape
    t = jnp.arange(tm)
    tri = (t[:, None] > t[None, :]).astype(BF16)
    return pl.pallas_call(
        _router_kernel,
        grid=(n // tm,),
        in_specs=[pl.BlockSpec((tm, d), lambda i: (i, 0)),
                  pl.BlockSpec((1, d), lambda i: (0, 0)),
                  pl.BlockSpec(wr_split.shape, lambda i: (0, 0, 0)),
                  pl.BlockSpec((tm, tm), lambda i: (0, 0))],
        out_specs=[pl.BlockSpec((tm, LANES), lambda i: (i, 0)),
                   pl.BlockSpec((tm, LANES), lambda i: (i, 0)),
                   pl.BlockSpec((1, LANES), lambda i: (0, 0))],
        out_shape=[jax.ShapeDtypeStruct((n, LANES), jnp.int32),
                   jax.ShapeDtypeStruct((n, LANES), F32),
                   jax.ShapeDtypeStruct((1, LANES), F32)],
        scratch_shapes=[pltpu.VMEM((1, LANES), F32)],
        compiler_params=_params(1),
        name="router",
    )(h, g, wr_split, tri)


def _rows_to_tiles(ref, x):
    m = x.shape[0]
    for s in range(SUBLANES):
        ref[pl.ds(s, m, stride=SUBLANES), :] = x[:, s * LANES:(s + 1) * LANES]


def _tiles_to_rows(ref, m):
    return jnp.concatenate([ref[pl.ds(s, m, stride=SUBLANES), :] for s in range(SUBLANES)], axis=1)


def _tile_rows(ref, first_row, rows):
    return ref.at[pl.ds(pl.multiple_of(first_row * SUBLANES, SUBLANES), rows * SUBLANES)]


def _dispatch_kernel(dest_ref, pad_start_ref, pad_len_ref, n_valid_ref, h_ref, g_ref, xs_hbm,
                     buf_ref, zero_ref, sem, zero_sem, *, tm):
    base = pl.program_id(0) * tm * TOP_K

    @pl.when(pl.program_id(0) == 0)
    def _():
        zero_ref[...] = jnp.zeros_like(zero_ref)
        n_blocks = n_valid_ref.shape[0]

        def zero_copy(first_row, rows):
            return pltpu.make_async_copy(_tile_rows(zero_ref, 0, rows), _tile_rows(xs_hbm, first_row, rows),
                                         zero_sem)

        def fill(wait):
            def go(cp):
                cp.wait() if wait else cp.start()

            for e in range(N_EXPERTS):
                pos = pad_start_ref[e]
                left = pad_len_ref[e]
                rows = MOE_ROWS // 2
                while rows >= 1:
                    take = (left & rows) != 0

                    @pl.when(take)
                    def _(pos=pos, rows=rows):
                        go(zero_copy(pos, rows))
                    pos = pos + jnp.where(take, rows, 0)
                    rows //= 2

            def blocks(b, carry):
                @pl.when(n_valid_ref[b] == 0)
                def _():
                    go(zero_copy(b * MOE_ROWS, MOE_ROWS))
                return carry
            lax.fori_loop(0, n_blocks, blocks, 0)

        fill(False)
        fill(True)

    _rows_to_tiles(buf_ref, _rms(h_ref[...], g_ref[...]))

    def start(r, carry):
        for k in range(TOP_K):
            dst = dest_ref[base + r * TOP_K + k]
            pltpu.make_async_copy(_tile_rows(buf_ref, r, 1), _tile_rows(xs_hbm, dst, 1), sem).start(priority=k)
        return carry

    lax.fori_loop(0, tm, start, 0, unroll=DMA_UNROLL)
    for _ in range(TOP_K):
        pltpu.make_async_copy(buf_ref, _tile_rows(xs_hbm, 0, tm), sem).wait()


def _dispatch(dest, pad_start, pad_len, n_valid, h, g, tm=512):
    n, d = h.shape
    assert d == SUBLANES * LANES
    n_rows = n_valid.shape[0] * MOE_ROWS
    grid_spec = pltpu.PrefetchScalarGridSpec(
        num_scalar_prefetch=4,
        grid=(n // tm,),
        in_specs=[pl.BlockSpec((tm, d), lambda i, *_: (i, 0)),
                  pl.BlockSpec((1, d), lambda i, *_: (0, 0))],
        out_specs=pl.BlockSpec(memory_space=pl.ANY),
        scratch_shapes=[pltpu.VMEM((tm * SUBLANES, LANES), F32), pltpu.VMEM((MOE_ROWS * SUBLANES, LANES), F32),
                        pltpu.SemaphoreType.DMA, pltpu.SemaphoreType.DMA],
    )
    return pl.pallas_call(
        functools.partial(_dispatch_kernel, tm=tm),
        grid_spec=grid_spec,
        out_shape=jax.ShapeDtypeStruct((n_rows * SUBLANES, LANES), F32),
        compiler_params=_params(1),
        name="moe_dispatch",
    )(dest, pad_start, pad_len, n_valid, h, g)


def _expert_kernel(blk_e_ref, n_valid_ref, xs_ref, wg_ref, wu_ref, wd_ref, ys_ref, act_ref):
    n_valid = n_valid_ref[pl.program_id(0)]

    @pl.when(n_valid > 0)
    def _():
        x = _tiles_to_rows(xs_ref, MOE_ROWS).astype(BF16)
        _rows_to_tiles(ys_ref, _swiglu(x, wg_ref, wu_ref, wd_ref, act_ref))

    @pl.when(n_valid == 0)
    def _():
        ys_ref[...] = jnp.zeros_like(ys_ref)


def _experts(blk_e, n_valid, xs, wg, wu, wd):
    d, ff = wg.shape[1], wg.shape[2]
    blk = MOE_ROWS * SUBLANES
    grid_spec = pltpu.PrefetchScalarGridSpec(
        num_scalar_prefetch=2,
        grid=(xs.shape[0] // blk,),
        in_specs=[pl.BlockSpec((blk, LANES), lambda i, be, nv: (i, 0)),
                  pl.BlockSpec((None, d, ff), lambda i, be, nv: (be[i], 0, 0)),
                  pl.BlockSpec((None, d, ff), lambda i, be, nv: (be[i], 0, 0)),
                  pl.BlockSpec((None, ff, d), lambda i, be, nv: (be[i], 0, 0))],
        out_specs=pl.BlockSpec((blk, LANES), lambda i, be, nv: (i, 0)),
        scratch_shapes=[pltpu.VMEM((MOE_ROWS, ff), BF16)],
    )
    return pl.pallas_call(
        _expert_kernel,
        grid_spec=grid_spec,
        out_shape=jax.ShapeDtypeStruct(xs.shape, F32),
        compiler_params=_params(1),
        name="moe_experts",
    )(blk_e, n_valid, xs, wg, wu, wd)


def _combine_kernel(dest_ref, ys_hbm, gate_ref, g_ref, h_ref, o_ref, buf_ref, sem, *, tm):
    i = pl.program_id(0)

    def fetch(tile, slot):
        def body(r, carry):
            for k in range(TOP_K):
                src = dest_ref[(tile * tm + r) * TOP_K + k]
                pltpu.make_async_copy(_tile_rows(ys_hbm, src, 1), _tile_rows(buf_ref.at[slot, k], r, 1),
                                      sem.at[slot]).start(priority=k)
            return carry
        lax.fori_loop(0, tm, body, 0, unroll=DMA_UNROLL)

    @pl.when(i == 0)
    def _():
        fetch(0, 0)

    @pl.when(i + 1 < pl.num_programs(0))
    def _():
        fetch(i + 1, (i + 1) % 2)

    slot = i % 2

    for k in range(TOP_K):
        pltpu.make_async_copy(_tile_rows(ys_hbm, 0, tm), buf_ref.at[slot, k], sem.at[slot]).wait()
    gates = gate_ref[...]
    f = (_tiles_to_rows(buf_ref.at[slot, 0], tm) * gates[:, 0:1]
         + _tiles_to_rows(buf_ref.at[slot, 1], tm) * gates[:, 1:2])
    o_ref[...] = h_ref[...] + _rms(f, g_ref[...])


def _combine(dest, ys, gates, g, h, tm=512):
    n, d = h.shape
    grid_spec = pltpu.PrefetchScalarGridSpec(
        num_scalar_prefetch=1,
        grid=(n // tm,),
        in_specs=[pl.BlockSpec(memory_space=pl.ANY),
                  pl.BlockSpec((tm, LANES), lambda i, dst: (i, 0)),
                  pl.BlockSpec((1, d), lambda i, dst: (0, 0)),
                  pl.BlockSpec((tm, d), lambda i, dst: (i, 0))],
        out_specs=pl.BlockSpec((tm, d), lambda i, dst: (i, 0)),
        scratch_shapes=[pltpu.VMEM((2, TOP_K, tm * SUBLANES, LANES), F32), pltpu.SemaphoreType.DMA((2,))],
    )
    return pl.pallas_call(
        functools.partial(_combine_kernel, tm=tm),
        grid_spec=grid_spec,
        out_shape=jax.ShapeDtypeStruct((n, d), F32),
        compiler_params=_params(1),
        name="moe_combine",
    )(dest, ys, gates, g, h)


def _route_plan(idx, counts_f, n_tok):
    counts = counts_f[0, :N_EXPERTS].astype(jnp.int32)
    padded = ((counts + MOE_ROWS - 1) // MOE_ROWS) * MOE_ROWS
    pend = jnp.cumsum(padded)
    pstart = pend - padded
    expert = idx[:, 0:TOP_K]
    rank = idx[:, TOP_K:2 * TOP_K]
    offset = jnp.zeros_like(expert)
    for e in range(N_EXPERTS):
        offset = jnp.where(expert == e, pstart[e], offset)
    dest = (offset + rank).reshape(-1)
    n_blocks = (n_tok * TOP_K) // MOE_ROWS + N_EXPERTS
    blk_start = jnp.arange(n_blocks, dtype=jnp.int32) * MOE_ROWS
    blk_e = jnp.minimum(jnp.sum(blk_start[:, None] >= pend[None, :], axis=1), N_EXPERTS - 1).astype(jnp.int32)
    n_valid = jnp.clip(pstart[blk_e] + counts[blk_e] - blk_start, 0, MOE_ROWS).astype(jnp.int32)
    return dest, pstart + counts, padded - counts, blk_e, n_valid


def _moe(h, g_pre, router_w, wg, wu, wd, g_post):
    n, d = h.shape
    wr = jnp.pad(router_w, ((0, 0), (0, LANES - N_EXPERTS)))
    wr_hi = wr.astype(BF16)
    wr_mid = (wr - wr_hi.astype(F32)).astype(BF16)
    idx, gates, counts = _router(h, g_pre, jnp.stack([wr_hi, wr_mid]))
    dest, pad_start, pad_len, blk_e, n_valid = _route_plan(idx, counts, n)
    xs = _dispatch(dest, pad_start, pad_len, n_valid, h, g_pre)
    ys = _experts(blk_e, n_valid, xs, wg, wu, wd)
    return _combine(dest, ys, gates, g_post, h)


def kernel(x, mem, mix_norm_pre, mix_norm_post, w_in, b_forget, pool_w, pool_scale, w_out,
           xa_norm_pre, xa_norm_post, mem_norm, xa_wq, xa_wkv, xa_wo,
           ffn_norm_pre, ffn_norm_post, dense_w_gate, dense_w_up, dense_w_down,
           router_w, moe_w_gate, moe_w_up, moe_w_down):
    bsz, seq, d = x.shape
    m_len = mem.shape[1]
    depth = w_in.shape[0]
    n = bsz * seq
    h = x.reshape(n, d)
    mem2 = mem.reshape(bsz * m_len, d)
    row = lambda v: v.reshape(1, -1)

    idx = jnp.arange(SB_SUB)
    tri = (idx[:, None] >= idx[None, :]).astype(BF16)

    for li in range(depth):
        w = w_in[li]
        flog_w = jnp.pad(w[:, QKV_DIM:QKV_DIM + N_FOX], ((0, 0), (0, LANES - N_FOX)))
        w_cat = jnp.concatenate([w[:, :QKV_DIM], w[:, QKV_DIM + N_FOX:], flog_w], axis=1).astype(BF16)
        qkv, u, flog, qn, kmax_tiles = _in_proj(h, row(mix_norm_pre[li]), w_cat)
        qkv = qkv.reshape(bsz, seq, QKV_DIM)
        kmax = jnp.max(kmax_tiles[:, 0, :].reshape(bsz, -1, LANES), axis=1, keepdims=True)

        b_pad = jnp.pad(b_forget[li], (0, LANES - N_FOX)).reshape(1, LANES)
        by_seq = lambda v: v.reshape(bsz, seq, LANES)
        c, a_cols, zb_tiles = _logf_cumsum(by_seq(flog), b_pad, by_seq(qn), kmax)
        by_pair = lambda v: v[..., :N_FOX].transpose(0, 2, 1).reshape(bsz, N_PAIRS, 2, -1)
        c_rows = by_pair(c)
        zb_max = by_pair(zb_tiles[:, :, 0, :])

        sb = _sb_attention(qkv, tri).reshape(n, GROUP_DIM)
        fox = _fox_attention(qkv, a_cols, c_rows, zb_max).reshape(n, GROUP_DIM)

        pool_bd = jax.scipy.linalg.block_diag(*[pool_w[li, gi] for gi in range(len(POOL_WINDOWS))])
        k_mem, v_mem = _norm_matmul(mem2, row(mem_norm[li]), xa_wkv[li].astype(BF16),
                                    splits=[(0, d), (d, d)], dtypes=[BF16, BF16], tm=m_len)
        h = _mix_xattn(sb, fox, u, pool_bd.astype(BF16), row(pool_scale[li]), w_out[li].astype(BF16),
                       row(mix_norm_post[li]), h, row(xa_norm_pre[li]), xa_wq[li].astype(BF16),
                       k_mem.reshape(bsz, m_len, d), v_mem.reshape(bsz, m_len, d),
                       xa_wo[li].astype(BF16), row(xa_norm_post[li]), seq)

        j = li // 2
        if li % 2 == 0:
            h = _ffn(h, row(ffn_norm_pre[li]), dense_w_gate[j].astype(BF16), dense_w_up[j].astype(BF16),
                     dense_w_down[j].astype(BF16), row(ffn_norm_post[li]))
        else:
            h = _moe(h, row(ffn_norm_pre[li]), router_w[j], moe_w_gate[j].astype(BF16),
                     moe_w_up[j].astype(BF16), moe_w_down[j].astype(BF16), row(ffn_norm_post[li]))
    return h.reshape(bsz, seq, d)
```

```python
import functools

import jax
import jax.numpy as jnp
from jax import lax
from jax.experimental import pallas as pl
from jax.experimental.pallas import tpu as pltpu

F32 = jnp.float32
BF16 = jnp.bfloat16
EPS = 1e-6

HEAD_DIM = 64
LANES = 128
SUBLANES = 8
N_PAIRS = 3
GROUP_DIM = N_PAIRS * LANES
QKV_DIM = 6 * GROUP_DIM
POOL_DIM = 256
POOL_WINDOWS = (2, 4, 8, 16)
POOL_HALO = 16
N_FOX = 6
XA_HEADS = 4
N_EXPERTS = 8
TOP_K = 2
MOE_ROWS = 512
FF_CHUNK = 256
DMA_UNROLL = 8
SB_SUB = 256
SB_CHAINS = 8
FOX_SUB = 256
FOX_CHAINS = 8
FOX_SAFE_BOUND = 40.0
FOX_BOUND_ROWS = 1024
EXP_CUTOFF = 105.0
VMEM_LIMIT = 56 * 1024 * 1024

_NT = (((1,), (1,)), ((), ()))


def _params(n_axes):
    return pltpu.CompilerParams(dimension_semantics=("arbitrary",) * n_axes,
                                vmem_limit_bytes=VMEM_LIMIT)


def _rms(x, g):
    return x * lax.rsqrt(jnp.mean(x * x, axis=-1, keepdims=True) + EPS) * g


def _dot(a, b):
    return jnp.dot(a, b, preferred_element_type=F32)


def _split3(x):
    hi = x.astype(BF16)
    r1 = x - hi.astype(F32)
    mid = r1.astype(BF16)
    lo = (r1 - mid.astype(F32)).astype(BF16)
    return hi, mid, lo


def _norm_matmul_kernel(x_ref, g_ref, w_ref, *out_refs, splits):
    yb = _rms(x_ref[...], g_ref[...]).astype(BF16)
    for o_ref, (c0, width) in zip(out_refs, splits):
        for c in range(0, width, 256):
            cw = min(256, width - c)
            o_ref[:, c:c + cw] = _dot(yb, w_ref[:, c0 + c:c0 + c + cw]).astype(o_ref.dtype)


def _norm_matmul(x, g, w, splits, dtypes, tm):
    n, d = x.shape
    kern = functools.partial(_norm_matmul_kernel, splits=tuple(splits))
    return pl.pallas_call(
        kern,
        grid=(n // tm,),
        in_specs=[pl.BlockSpec((tm, d), lambda i: (i, 0)),
                  pl.BlockSpec((1, d), lambda i: (0, 0)),
                  pl.BlockSpec(w.shape, lambda i: (0, 0))],
        out_specs=[pl.BlockSpec((tm, wd), lambda i: (i, 0)) for (_, wd) in splits],
        out_shape=[jax.ShapeDtypeStruct((n, wd), dt) for (_, wd), dt in zip(splits, dtypes)],
        compiler_params=_params(1),
        name="norm_matmul",
    )(x, g, w)


def _in_proj_kernel(x_ref, g_ref, w_ref, sel_ref, qkv_ref, u_ref, fl_ref, qn_ref, kmax_ref):
    yb = _rms(x_ref[...], g_ref[...]).astype(BF16)
    for c in range(0, QKV_DIM, 256):
        qkv_ref[:, c:c + 256] = _dot(yb, w_ref[:, c:c + 256]).astype(BF16)
    u_ref[...] = _dot(yb, w_ref[:, QKV_DIM:QKV_DIM + POOL_DIM])
    fl_ref[...] = _dot(yb, w_ref[:, QKV_DIM + POOL_DIM:])

    def head_norms(c0):
        x = qkv_ref[:, c0:c0 + GROUP_DIM].astype(F32)
        sq = x * x
        hi = sq.astype(BF16)
        lo = (sq - hi.astype(F32)).astype(BF16)
        return _dot(hi, sel_ref[...]) + _dot(lo, sel_ref[...])

    qn_ref[...] = head_norms(3 * GROUP_DIM)
    kmax = jnp.max(head_norms(4 * GROUP_DIM), axis=0, keepdims=True)
    kmax_ref[...] = jnp.broadcast_to(kmax, kmax_ref.shape)


def _in_proj(x, g, w, tm=1024):
    n, d = x.shape
    lanes = jnp.arange(GROUP_DIM)[:, None] // HEAD_DIM == jnp.arange(LANES)[None, :]
    rows = lambda width: pl.BlockSpec((tm, width), lambda i: (i, 0))
    return pl.pallas_call(
        _in_proj_kernel,
        grid=(n // tm,),
        in_specs=[rows(d), pl.BlockSpec((1, d), lambda i: (0, 0)), pl.BlockSpec(w.shape, lambda i: (0, 0)),
                  pl.BlockSpec((GROUP_DIM, LANES), lambda i: (0, 0))],
        out_specs=[rows(QKV_DIM), rows(POOL_DIM), rows(LANES), rows(LANES),
                   pl.BlockSpec((None, SUBLANES, LANES), lambda i: (i, 0, 0))],
        out_shape=[jax.ShapeDtypeStruct((n, QKV_DIM), BF16), jax.ShapeDtypeStruct((n, POOL_DIM), F32),
                   jax.ShapeDtypeStruct((n, LANES), F32), jax.ShapeDtypeStruct((n, LANES), F32),
                   jax.ShapeDtypeStruct((n // tm, SUBLANES, LANES), F32)],
        compiler_params=_params(1),
        name="in_proj",
    )(x, g, w, lanes.astype(BF16))


def _logf_cumsum_kernel(fl_ref, b_ref, qn_ref, kmax_ref, c_ref, a_ref, zbmax_ref, carry_ref, *, tc):
    @pl.when(pl.program_id(1) == 0)
    def _():
        carry_ref[...] = jnp.zeros_like(carry_ref)

    x = fl_ref[...] + b_ref[...]
    ls = jnp.minimum(x, 0.0) - jnp.log(1.0 + jnp.exp(-jnp.abs(x)))
    row = lax.broadcasted_iota(jnp.int32, (tc, tc), 0)
    col = lax.broadcasted_iota(jnp.int32, (tc, tc), 1)
    tri = jnp.where(row >= col, 1.0, 0.0).astype(BF16)
    hi, mid, lo = _split3(ls)
    c = _dot(tri, hi) + _dot(tri, mid) + _dot(tri, lo) + carry_ref[...]
    c_ref[...] = c
    carry_ref[...] = c[tc - 1:tc, :]
    zb = jnp.sqrt(qn_ref[...] * kmax_ref[...] * (1.0 / HEAD_DIM)) * 1.001
    a_ref[...] = c - zb
    zbmax_ref[...] = jnp.broadcast_to(jnp.max(zb, axis=0, keepdims=True), zbmax_ref.shape)


def _logf_cumsum(fl, b_pad, qn, kmax, tc=FOX_BOUND_ROWS):
    bsz, s, _ = fl.shape
    rows = pl.BlockSpec((None, tc, LANES), lambda b, j: (b, j, 0))
    return pl.pallas_call(
        functools.partial(_logf_cumsum_kernel, tc=tc),
        grid=(bsz, s // tc),
        in_specs=[rows, pl.BlockSpec((1, LANES), lambda b, j: (0, 0)), rows,
                  pl.BlockSpec((None, 1, LANES), lambda b, j: (b, 0, 0))],
        out_specs=[rows, rows, pl.BlockSpec((None, None, SUBLANES, LANES), lambda b, j: (b, j, 0, 0))],
        out_shape=[jax.ShapeDtypeStruct(fl.shape, F32), jax.ShapeDtypeStruct(fl.shape, F32),
                   jax.ShapeDtypeStruct((bsz, s // tc, SUBLANES, LANES), F32)],
        scratch_shapes=[pltpu.VMEM((1, LANES), F32)],
        compiler_params=_params(2),
        name="logf_cumsum",
    )(fl, b_pad, qn, kmax)


def _sb_kernel(q_ref, k_ref, v_ref, tri_ref, o_ref, acc_ref, run_ref):
    sub, n_chain = SB_SUB, SB_CHAINS
    blk0 = pl.program_id(2) * n_chain
    lane = lax.broadcasted_iota(jnp.int32, (1, LANES), 1)
    in_head = (lane < HEAD_DIM, lane >= HEAD_DIM)
    q_all = q_ref[...] * jnp.asarray(HEAD_DIM ** -0.5, BF16)
    q_chain = [[jnp.where(in_head[hd], q_all[r * sub:(r + 1) * sub, :], jnp.zeros((), BF16))
                for r in range(n_chain)] for hd in range(2)]
    row = lax.broadcasted_iota(jnp.int32, (sub, sub), 0)
    col = lax.broadcasted_iota(jnp.int32, (sub, sub), 1)
    acc_ref[...] = jnp.zeros_like(acc_ref)
    run_ref[...] = jnp.zeros_like(run_ref)

    def tile(hd, r, d, diag):
        kb = blk0 + r - d
        start = pl.multiple_of(jnp.maximum(kb, 0) * sub, sub)
        k = k_ref[pl.ds(start, sub), :]
        v = jnp.where(in_head[hd], v_ref[pl.ds(start, sub), :], jnp.zeros((), BF16))
        z = lax.dot_general(q_chain[hd][r], k, _NT, preferred_element_type=F32)
        sp = jnp.maximum(z, 0.0) + jnp.log(1.0 + jnp.exp(-jnp.abs(z)))
        if diag:
            sp = jnp.where(col < row, sp, 0.0)
        suffix = _dot(sp.astype(BF16), tri_ref[...])
        run = run_ref[hd, r]
        w = jnp.exp(z - suffix - jnp.where(kb >= 0, run, 1e30))
        if diag:
            w = jnp.where(col < row, w, 0.0)
        acc_ref[hd, r] += _dot(w.astype(BF16), v)
        run_ref[hd, r] = run + suffix[:, 0:1]

    def more(d):
        go = jnp.bool_(False)
        for hd in range(2):
            for r in range(n_chain):
                unfinished = jnp.min(run_ref[hd, r]) < EXP_CUTOFF
                go = jnp.logical_or(go, jnp.logical_and(blk0 + r - d >= 0, unfinished))
        return go.astype(jnp.int32)

    def step(d, diag):
        for r in range(n_chain):
            for hd in range(2):
                tile(hd, r, d, diag)

    def body(carry):
        d, _ = carry
        step(d, False)
        return d + 1, more(d + 1)

    step(0, True)
    lax.while_loop(lambda carry: carry[1] > 0, body, (jnp.int32(1), more(1)))
    for r in range(n_chain):
        o_ref[r * sub:(r + 1) * sub, :] = (acc_ref[0, r] + acc_ref[1, r]).astype(o_ref.dtype)


def _sb_attention(qkv, tri):
    bsz, s, _ = qkv.shape
    tq = SB_SUB * SB_CHAINS
    return pl.pallas_call(
        _sb_kernel,
        grid=(bsz, N_PAIRS, s // tq),
        in_specs=[pl.BlockSpec((None, tq, LANES), lambda b, p, i: (b, i, p)),
                  pl.BlockSpec((None, s, LANES), lambda b, p, i: (b, 0, N_PAIRS + p)),
                  pl.BlockSpec((None, s, LANES), lambda b, p, i: (b, 0, 2 * N_PAIRS + p)),
                  pl.BlockSpec((SB_SUB, SB_SUB), lambda b, p, i: (0, 0))],
        out_specs=pl.BlockSpec((None, tq, LANES), lambda b, p, i: (b, i, p)),
        out_shape=jax.ShapeDtypeStruct((bsz, s, GROUP_DIM), BF16),
        scratch_shapes=[pltpu.VMEM((2, SB_CHAINS, SB_SUB, LANES), F32),
                        pltpu.VMEM((2, SB_CHAINS, SB_SUB, 1), F32)],
        compiler_params=_params(3),
        name="sb_attention",
    )(qkv, qkv, qkv, tri)


def _fox_kernel(q_ref, k_ref, v_ref, a_ref, crow_ref, cs_ref, zbmax_ref, o_ref, acc_ref, z_ref):
    sub, n_chain = FOX_SUB, FOX_CHAINS
    pair = pl.program_id(1)
    i = pl.program_id(2)
    lane = lax.broadcasted_iota(jnp.int32, (1, LANES), 1)
    in_head = (lane < HEAD_DIM, lane >= HEAD_DIM)
    one_lane = (HEAD_DIM, 0)

    q_all = q_ref[...] * jnp.asarray(HEAD_DIM ** -0.5, BF16)
    row = lax.broadcasted_iota(jnp.int32, (sub, sub), 0)
    col = lax.broadcasted_iota(jnp.int32, (sub, sub), 1)
    blk0 = i * n_chain
    acc_ref[...] = jnp.zeros_like(acc_ref)

    for hd in range(2):
        q_head = jnp.where(in_head[hd], q_all, jnp.zeros((), BF16))
        per_step = (sub * n_chain) // FOX_BOUND_ROWS
        zb_max = zbmax_ref[hd, i * per_step]
        for t in range(1, per_step):
            zb_max = jnp.maximum(zb_max, zbmax_ref[hd, i * per_step + t])
        cutoff = EXP_CUTOFF + 2.0 * zb_max
        c_base = cs_ref[hd, blk0 * sub]
        a_all = jnp.sum(jnp.where(lane == 2 * pair + hd, a_ref[...], 0.0), axis=1, keepdims=True) - c_base
        v_one = jnp.where(lane == one_lane[hd], 1.0, 0.0).astype(BF16)
        q_chain = [q_head[r * sub:(r + 1) * sub, :] for r in range(n_chain)]
        a_chain = [a_all[r * sub:(r + 1) * sub, :] for r in range(n_chain)]

        def tile_inputs(kb, hd=hd, c_base=c_base, v_one=v_one):
            start = pl.multiple_of(jnp.maximum(kb, 0) * sub, sub)
            k = k_ref[pl.ds(start, sub), :]
            v = jnp.where(in_head[hd], v_ref[pl.ds(start, sub), :], v_one)
            c_row = crow_ref[hd:hd + 1, pl.ds(start, sub)] - c_base
            return k, v, jnp.where(kb >= 0, c_row, 1e30)

        def more(d, cutoffs, hd=hd):
            go = jnp.bool_(False)
            for r in range(n_chain):
                kb = blk0 + r - d
                first = cs_ref[hd, (blk0 + r) * sub]
                last = cs_ref[hd, jnp.maximum(kb, 0) * sub + sub - 1]
                go = jnp.logical_or(go, jnp.logical_and(kb >= 0, first - last > -cutoffs[r]))
            return go

        def scores(r, d, q_chain=q_chain):
            start = pl.multiple_of(jnp.maximum(blk0 + r - d, 0) * sub, sub)
            return lax.dot_general(q_chain[r], k_ref[pl.ds(start, sub), :], _NT,
                                   preferred_element_type=F32)

        def fast_tile(r, d, z, diag, hd=hd, a_chain=a_chain, tile_inputs=tile_inputs):
            _, v, c_row = tile_inputs(blk0 + r - d)
            e = z + a_chain[r] - c_row
            if diag:
                e = jnp.where(col <= row, e, -1e30)
            acc_ref[hd, r] += _dot(jnp.exp(e).astype(BF16), v)
            return jnp.min(jnp.max(e, axis=1, keepdims=True)) if diag else None

        def slow_tile(r, d, m_old, diag, hd=hd, q_chain=q_chain, tile_inputs=tile_inputs):
            k, v, c_row = tile_inputs(blk0 + r - d)
            s = lax.dot_general(q_chain[r], k, _NT, preferred_element_type=F32) - c_row
            if diag:
                s = jnp.where(col <= row, s, -1e30)
            m_new = jnp.maximum(m_old, jnp.max(s, axis=1, keepdims=True))
            acc_ref[hd, r] = (acc_ref[hd, r] * jnp.exp(m_old - m_new)
                              + _dot(jnp.exp(s - m_new).astype(BF16), v))
            return m_new

        @pl.when(zb_max <= FOX_SAFE_BOUND)
        def _(fast_tile=fast_tile, scores=scores, more=more):
            cutoffs = []
            for r in range(n_chain):
                cutoffs.append(EXP_CUTOFF - fast_tile(r, 0, scores(r, 0), True))
                z_ref[r] = scores(r, 1)

            def body(d):
                for r in range(n_chain):
                    z = z_ref[r]
                    z_ref[r] = scores(r, d + 1)
                    fast_tile(r, d, z, False)
                return d + 1
            lax.while_loop(lambda d: more(d, cutoffs), body, jnp.int32(1))

        @pl.when(zb_max > FOX_SAFE_BOUND)
        def _(slow_tile=slow_tile, more=more):
            m0 = jnp.full((sub, 1), -1e30, F32)
            ms = tuple(slow_tile(r, 0, m0, True) for r in range(n_chain))

            def body(carry):
                d, ms = carry
                return d + 1, tuple(slow_tile(r, d, ms[r], False) for r in range(n_chain))
            lax.while_loop(lambda carry: more(carry[0], [cutoff] * n_chain), body, (jnp.int32(1), ms))

    for r in range(n_chain):
        acc = (acc_ref[0, r], acc_ref[1, r])
        total = [jnp.sum(jnp.where(lane == one_lane[hd], acc[hd], 0.0), axis=1, keepdims=True)
                 for hd in range(2)]
        out = jnp.where(in_head[0], acc[0] / total[0], acc[1] / total[1])
        o_ref[r * sub:(r + 1) * sub, :] = out.astype(o_ref.dtype)


def _fox_attention(qkv, a_cols, c_rows, zb_max):
    bsz, s, _ = qkv.shape
    tq = FOX_SUB * FOX_CHAINS
    return pl.pallas_call(
        _fox_kernel,
        grid=(bsz, N_PAIRS, s // tq),
        in_specs=[pl.BlockSpec((None, tq, LANES), lambda b, p, i: (b, i, 3 * N_PAIRS + p)),
                  pl.BlockSpec((None, s, LANES), lambda b, p, i: (b, 0, 4 * N_PAIRS + p)),
                  pl.BlockSpec((None, s, LANES), lambda b, p, i: (b, 0, 5 * N_PAIRS + p)),
                  pl.BlockSpec((None, tq, LANES), lambda b, p, i: (b, i, 0)),
                  pl.BlockSpec((None, None, 2, s), lambda b, p, i: (b, p, 0, 0)),
                  pl.BlockSpec((None, None, 2, s), lambda b, p, i: (b, p, 0, 0),
                               memory_space=pltpu.SMEM),
                  pl.BlockSpec((None, None, 2, s // FOX_BOUND_ROWS), lambda b, p, i: (b, p, 0, 0),
                               memory_space=pltpu.SMEM)],
        out_specs=pl.BlockSpec((None, tq, LANES), lambda b, p, i: (b, i, p)),
        out_shape=jax.ShapeDtypeStruct((bsz, s, GROUP_DIM), BF16),
        scratch_shapes=[pltpu.VMEM((2, FOX_CHAINS, FOX_SUB, LANES), F32),
                        pltpu.VMEM((FOX_CHAINS, FOX_SUB, FOX_SUB), F32)],
        compiler_params=_params(3),
        name="fox_attention",
    )(qkv, qkv, qkv, a_cols, c_rows, c_rows, zb_max)


def _pool(u, halo, pw_ref, ps_ref, pos0, tm):
    x = jnp.concatenate([halo, u], axis=0)
    s2 = x + pltpu.roll(x, 1, 0)
    s4 = s2 + pltpu.roll(s2, 2, 0)
    s8 = s4 + pltpu.roll(s4, 4, 0)
    s16 = s8 + pltpu.roll(s8, 8, 0)
    lane = lax.broadcasted_iota(jnp.int32, (1, POOL_DIM), 1)
    grp = POOL_DIM // len(POOL_WINDOWS)
    wsum = jnp.where(lane < grp, s2, jnp.where(lane < 2 * grp, s4, jnp.where(lane < 3 * grp, s8, s16)))
    win = jnp.where(lane < grp, 2, jnp.where(lane < 2 * grp, 4, jnp.where(lane < 3 * grp, 8, 16)))
    pos = pos0 + lax.broadcasted_iota(jnp.int32, (tm, 1), 0)
    count = jnp.minimum(pos + 1, win).astype(F32)
    r = wsum[POOL_HALO:, :] / count - u
    return _dot(r.astype(BF16), pw_ref[...]) * ps_ref[...]


def _mix_xattn_kernel(sb_ref, fox_ref, u_ref, halo_ref, pw_ref, ps_ref, wout_ref, gmix_ref, h_ref,
                      gpre_ref, wq_ref, k_ref, v_ref, wo_ref, gpost_ref, o_ref, *, tm, seq):
    pos0 = (pl.program_id(0) * tm) % seq
    halo = jnp.where(pos0 == 0, 0.0, halo_ref[...])
    pool = _pool(u_ref[...], halo, pw_ref, ps_ref, pos0, tm)
    gd = GROUP_DIM
    a = (_dot(sb_ref[...], wout_ref[0:gd, :]) + _dot(fox_ref[...], wout_ref[gd:2 * gd, :])
         + _dot(pool.astype(BF16), wout_ref[2 * gd:, :]))
    h = h_ref[...] + _rms(a, gmix_ref[...])

    d = h.shape[-1]
    hd = d // XA_HEADS
    hn = _rms(h, gpre_ref[...]).astype(BF16)
    q = (_dot(hn, wq_ref[...]) * (hd ** -0.5)).astype(BF16)
    outs = []
    for head in range(XA_HEADS):
        sl = slice(head * hd, (head + 1) * hd)
        s = lax.dot_general(q[:, sl], k_ref[:, sl], _NT, preferred_element_type=F32)
        p = jnp.exp(s - jnp.max(s, axis=1, keepdims=True))
        p = p * (1.0 / jnp.sum(p, axis=1, keepdims=True))
        outs.append(_dot(p.astype(BF16), v_ref[:, sl]).astype(BF16))
    c = _dot(jnp.concatenate(outs, axis=1), wo_ref[...])
    o_ref[...] = h + _rms(c, gpost_ref[...])


def _mix_xattn(sb, fox, u, pool_w_bd, pool_scale, w_out, g_mix, h, g_pre, wq, k_mem, v_mem, wo, g_post,
               seq, tm=1024):
    n, d = h.shape
    m_len = k_mem.shape[1]
    per_seq = seq // tm
    hb = tm // POOL_HALO
    const = lambda shape: pl.BlockSpec(shape, lambda i: (0,) * len(shape))
    rows = lambda width: pl.BlockSpec((tm, width), lambda i: (i, 0))
    mem = pl.BlockSpec((None, m_len, d), lambda i: (i // per_seq, 0, 0))
    return pl.pallas_call(
        functools.partial(_mix_xattn_kernel, tm=tm, seq=seq),
        grid=(n // tm,),
        in_specs=[rows(GROUP_DIM), rows(GROUP_DIM), rows(POOL_DIM),
                  pl.BlockSpec((POOL_HALO, POOL_DIM), lambda i: (jnp.maximum(i * hb - 1, 0), 0)),
                  const((POOL_DIM, POOL_DIM)), const((1, POOL_DIM)), const(w_out.shape), const((1, d)),
                  rows(d), const((1, d)), const((d, d)), mem, mem, const((d, d)), const((1, d))],
        out_specs=rows(d),
        out_shape=jax.ShapeDtypeStruct((n, d), F32),
        compiler_params=_params(1),
        name="mix_xattn",
    )(sb, fox, u, u, pool_w_bd, pool_scale, w_out, g_mix, h, g_pre, wq, k_mem, v_mem, wo, g_post)


def _silu(x):
    return x / (1.0 + jnp.exp(-x))


def _swiglu(x, wg_ref, wu_ref, wd_ref, act_ref):
    ff = wg_ref.shape[1]
    for c in range(0, ff, FF_CHUNK):
        gate = _dot(x, wg_ref[:, c:c + FF_CHUNK])
        up = _dot(x, wu_ref[:, c:c + FF_CHUNK])
        act_ref[:, c:c + FF_CHUNK] = (_silu(gate) * up).astype(BF16)
    return _dot(act_ref[...], wd_ref[...])


def _ffn_kernel(h_ref, gpre_ref, wg_ref, wu_ref, wd_ref, gpost_ref, o_ref, act_ref):
    h = h_ref[...]
    f = _swiglu(_rms(h, gpre_ref[...]).astype(BF16), wg_ref, wu_ref, wd_ref, act_ref)
    o_ref[...] = h + _rms(f, gpost_ref[...])


def _ffn(h, g_pre, wg, wu, wd, g_post, tm=512):
    n, d = h.shape
    ff = wg.shape[1]
    resident = pl.Buffered(1)
    return pl.pallas_call(
        _ffn_kernel,
        grid=(n // tm,),
        in_specs=[pl.BlockSpec((tm, d), lambda i: (i, 0)),
                  pl.BlockSpec((1, d), lambda i: (0, 0)),
                  pl.BlockSpec((d, ff), lambda i: (0, 0), pipeline_mode=resident),
                  pl.BlockSpec((d, ff), lambda i: (0, 0), pipeline_mode=resident),
                  pl.BlockSpec((ff, d), lambda i: (0, 0), pipeline_mode=resident),
                  pl.BlockSpec((1, d), lambda i: (0, 0))],
        out_specs=pl.BlockSpec((tm, d), lambda i: (i, 0)),
        out_shape=jax.ShapeDtypeStruct((n, d), F32),
        scratch_shapes=[pltpu.VMEM((tm, ff), BF16)],
        compiler_params=_params(1),
        name="ffn_dense",
    )(h, g_pre, wg, wu, wd, g_post)


def _router_kernel(h_ref, g_ref, wr_ref, tri_ref, idx_ref, gate_ref, count_ref, carry_ref):
    @pl.when(pl.program_id(0) == 0)
    def _():
        carry_ref[...] = jnp.zeros_like(carry_ref)

    hn = _rms(h_ref[...], g_ref[...])
    x_hi, x_mid, _ = _split3(hn)
    w_hi, w_mid = wr_ref[0], wr_ref[1]
    logits = _dot(x_hi, w_hi) + _dot(x_hi, w_mid) + _dot(x_mid, w_hi)
    tm = logits.shape[0]
    lane = lax.broadcasted_iota(jnp.int32, logits.shape, 1)
    logits = jnp.where(lane < N_EXPERTS, logits, -jnp.inf)
    m1 = jnp.max(logits, axis=1, keepdims=True)
    i1 = jnp.min(jnp.where(logits == m1, lane, LANES), axis=1, keepdims=True)
    rest = jnp.where(lane == i1, -jnp.inf, logits)
    m2 = jnp.max(rest, axis=1, keepdims=True)
    i2 = jnp.min(jnp.where(rest == m2, lane, LANES), axis=1, keepdims=True)
    e = jnp.exp(m2 - m1)
    g1 = 1.0 / (1.0 + e)
    gate_ref[...] = jnp.where(lane == 0, g1, jnp.where(lane == 1, e * g1, 0.0))

    onehot = jnp.where(lane == i1, 1.0, jnp.where(lane == i2, 1.0, 0.0))
    before = _dot(tri_ref[...], onehot.astype(BF16)) + carry_ref[...]
    r1 = jnp.sum(jnp.where(lane == i1, before, 0.0), axis=1, keepdims=True).astype(jnp.int32)
    r2 = jnp.sum(jnp.where(lane == i2, before, 0.0), axis=1, keepdims=True).astype(jnp.int32)
    idx_ref[...] = jnp.where(lane == 0, i1, jnp.where(lane == 1, i2,
                             jnp.where(lane == 2, r1, jnp.where(lane == 3, r2, 0))))
    total = before[tm - 1:tm, :] + onehot[tm - 1:tm, :]
    carry_ref[...] = total
    count_ref[...] = total


def _router(h, g, wr_split, tm=1024):
    n, d = h.shape
    t = jnp.arange(tm)
    tri = (t[:, None] > t[None, :]).astype(BF16)
    return pl.pallas_call(
        _router_kernel,
        grid=(n // tm,),
        in_specs=[pl.BlockSpec((tm, d), lambda i: (i, 0)),
                  pl.BlockSpec((1, d), lambda i: (0, 0)),
                  pl.BlockSpec(wr_split.shape, lambda i: (0, 0, 0)),
                  pl.BlockSpec((tm, tm), lambda i: (0, 0))],
        out_specs=[pl.BlockSpec((tm, LANES), lambda i: (i, 0)),
                   pl.BlockSpec((tm, LANES), lambda i: (i, 0)),
                   pl.BlockSpec((1, LANES), lambda i: (0, 0))],
        out_shape=[jax.ShapeDtypeStruct((n, LANES), jnp.int32),
                   jax.ShapeDtypeStruct((n, LANES), F32),
                   jax.ShapeDtypeStruct((1, LANES), F32)],
        scratch_shapes=[pltpu.VMEM((1, LANES), F32)],
        compiler_params=_params(1),
        name="router",
    )(h, g, wr_split, tri)


def _rows_to_tiles(ref, x):
    m = x.shape[0]
    for s in range(SUBLANES):
        ref[pl.ds(s, m, stride=SUBLANES), :] = x[:, s * LANES:(s + 1) * LANES]


def _tiles_to_rows(ref, m):
    return jnp.concatenate([ref[pl.ds(s, m, stride=SUBLANES), :] for s in range(SUBLANES)], axis=1)


def _tile_rows(ref, first_row, rows):
    return ref.at[pl.ds(pl.multiple_of(first_row * SUBLANES, SUBLANES), rows * SUBLANES)]


def _dispatch_kernel(dest_ref, pad_start_ref, pad_len_ref, n_valid_ref, h_ref, g_ref, xs_hbm,
                     buf_ref, zero_ref, sem, zero_sem, *, tm):
    base = pl.program_id(0) * tm * TOP_K

    @pl.when(pl.program_id(0) == 0)
    def _():
        zero_ref[...] = jnp.zeros_like(zero_ref)
        n_blocks = n_valid_ref.shape[0]

        def zero_copy(first_row, rows):
            return pltpu.make_async_copy(_tile_rows(zero_ref, 0, rows), _tile_rows(xs_hbm, first_row, rows),
                                         zero_sem)

        def fill(wait):
            def go(cp):
                cp.wait() if wait else cp.start()

            for e in range(N_EXPERTS):
                pos = pad_start_ref[e]
                left = pad_len_ref[e]
                rows = MOE_ROWS // 2
                while rows >= 1:
                    take = (left & rows) != 0

                    @pl.when(take)
                    def _(pos=pos, rows=rows):
                        go(zero_copy(pos, rows))
                    pos = pos + jnp.where(take, rows, 0)
                    rows //= 2

            def blocks(b, carry):
                @pl.when(n_valid_ref[b] == 0)
                def _():
                    go(zero_copy(b * MOE_ROWS, MOE_ROWS))
                return carry
            lax.fori_loop(0, n_blocks, blocks, 0)

        fill(False)
        fill(True)

    _rows_to_tiles(buf_ref, _rms(h_ref[...], g_ref[...]))

    def start(r, carry):
        for k in range(TOP_K):
            dst = dest_ref[base + r * TOP_K + k]
            pltpu.make_async_copy(_tile_rows(buf_ref, r, 1), _tile_rows(xs_hbm, dst, 1), sem).start(priority=k)
        return carry

    lax.fori_loop(0, tm, start, 0, unroll=DMA_UNROLL)
    for _ in range(TOP_K):
        pltpu.make_async_copy(buf_ref, _tile_rows(xs_hbm, 0, tm), sem).wait()


def _dispatch(dest, pad_start, pad_len, n_valid, h, g, tm=1024):
    n, d = h.shape
    assert d == SUBLANES * LANES
    n_rows = n_valid.shape[0] * MOE_ROWS
    grid_spec = pltpu.PrefetchScalarGridSpec(
        num_scalar_prefetch=4,
        grid=(n // tm,),
        in_specs=[pl.BlockSpec((tm, d), lambda i, *_: (i, 0)),
                  pl.BlockSpec((1, d), lambda i, *_: (0, 0))],
        out_specs=pl.BlockSpec(memory_space=pl.ANY),
        scratch_shapes=[pltpu.VMEM((tm * SUBLANES, LANES), F32), pltpu.VMEM((MOE_ROWS * SUBLANES, LANES), F32),
                        pltpu.SemaphoreType.DMA, pltpu.SemaphoreType.DMA],
    )
    return pl.pallas_call(
        functools.partial(_dispatch_kernel, tm=tm),
        grid_spec=grid_spec,
        out_shape=jax.ShapeDtypeStruct((n_rows * SUBLANES, LANES), F32),
        compiler_params=_params(1),
        name="moe_dispatch",
    )(dest, pad_start, pad_len, n_valid, h, g)


def _expert_kernel(blk_e_ref, n_valid_ref, xs_ref, wg_ref, wu_ref, wd_ref, ys_ref, act_ref):
    n_valid = n_valid_ref[pl.program_id(0)]

    @pl.when(n_valid > 0)
    def _():
        x = _tiles_to_rows(xs_ref, MOE_ROWS).astype(BF16)
        _rows_to_tiles(ys_ref, _swiglu(x, wg_ref, wu_ref, wd_ref, act_ref))

    @pl.when(n_valid == 0)
    def _():
        ys_ref[...] = jnp.zeros_like(ys_ref)


def _experts(blk_e, n_valid, xs, wg, wu, wd):
    d, ff = wg.shape[1], wg.shape[2]
    blk = MOE_ROWS * SUBLANES
    grid_spec = pltpu.PrefetchScalarGridSpec(
        num_scalar_prefetch=2,
        grid=(xs.shape[0] // blk,),
        in_specs=[pl.BlockSpec((blk, LANES), lambda i, be, nv: (i, 0)),
                  pl.BlockSpec((None, d, ff), lambda i, be, nv: (be[i], 0, 0)),
                  pl.BlockSpec((None, d, ff), lambda i, be, nv: (be[i], 0, 0)),
                  pl.BlockSpec((None, ff, d), lambda i, be, nv: (be[i], 0, 0))],
        out_specs=pl.BlockSpec((blk, LANES), lambda i, be, nv: (i, 0)),
        scratch_shapes=[pltpu.VMEM((MOE_ROWS, ff), BF16)],
    )
    return pl.pallas_call(
        _expert_kernel,
        grid_spec=grid_spec,
        out_shape=jax.ShapeDtypeStruct(xs.shape, F32),
        compiler_params=_params(1),
        name="moe_experts",
    )(blk_e, n_valid, xs, wg, wu, wd)


def _combine_kernel(dest_ref, ys_hbm, gate_ref, g_ref, h_ref, o_ref, buf_ref, sem, *, tm):
    i = pl.program_id(0)

    def fetch(tile, slot):
        def body(r, carry):
            for k in range(TOP_K):
                src = dest_ref[(tile * tm + r) * TOP_K + k]
                pltpu.make_async_copy(_tile_rows(ys_hbm, src, 1), _tile_rows(buf_ref.at[slot, k], r, 1),
                                      sem.at[slot]).start(priority=k)
            return carry
        lax.fori_loop(0, tm, body, 0, unroll=DMA_UNROLL)

    @pl.when(i == 0)
    def _():
        fetch(0, 0)

    @pl.when(i + 1 < pl.num_programs(0))
    def _():
        fetch(i + 1, (i + 1) % 2)

    slot = i % 2

    for k in range(TOP_K):
        pltpu.make_async_copy(_tile_rows(ys_hbm, 0, tm), buf_ref.at[slot, k], sem.at[slot]).wait()
    gates = gate_ref[...]
    f = (_tiles_to_rows(buf_ref.at[slot, 0], tm) * gates[:, 0:1]
         + _tiles_to_rows(buf_ref.at[slot, 1], tm) * gates[:, 1:2])
    o_ref[...] = h_ref[...] + _rms(f, g_ref[...])


def _combine(dest, ys, gates, g, h, tm=1024):
    n, d = h.shape
    grid_spec = pltpu.PrefetchScalarGridSpec(
        num_scalar_prefetch=1,
        grid=(n // tm,),
        in_specs=[pl.BlockSpec(memory_space=pl.ANY),
                  pl.BlockSpec((tm, LANES), lambda i, dst: (i, 0)),
                  pl.BlockSpec((1, d), lambda i, dst: (0, 0)),
                  pl.BlockSpec((tm, d), lambda i, dst: (i, 0))],
        out_specs=pl.BlockSpec((tm, d), lambda i, dst: (i, 0)),
        scratch_shapes=[pltpu.VMEM((2, TOP_K, tm * SUBLANES, LANES), F32), pltpu.SemaphoreType.DMA((2,))],
    )
    return pl.pallas_call(
        functools.partial(_combine_kernel, tm=tm),
        grid_spec=grid_spec,
        out_shape=jax.ShapeDtypeStruct((n, d), F32),
        compiler_params=_params(1),
        name="moe_combine",
    )(dest, ys, gates, g, h)


def _route_plan(idx, counts_f, n_tok):
    counts = counts_f[0, :N_EXPERTS].astype(jnp.int32)
    padded = ((counts + MOE_ROWS - 1) // MOE_ROWS) * MOE_ROWS
    pend = jnp.cumsum(padded)
    pstart = pend - padded
    expert = idx[:, 0:TOP_K]
    rank = idx[:, TOP_K:2 * TOP_K]
    offset = jnp.zeros_like(expert)
    for e in range(N_EXPERTS):
        offset = jnp.where(expert == e, pstart[e], offset)
    dest = (offset + rank).reshape(-1)
    n_blocks = (n_tok * TOP_K) // MOE_ROWS + N_EXPERTS
    blk_start = jnp.arange(n_blocks, dtype=jnp.int32) * MOE_ROWS
    blk_e = jnp.minimum(jnp.sum(blk_start[:, None] >= pend[None, :], axis=1), N_EXPERTS - 1).astype(jnp.int32)
    n_valid = jnp.clip(pstart[blk_e] + counts[blk_e] - blk_start, 0, MOE_ROWS).astype(jnp.int32)
    return dest, pstart + counts, padded - counts, blk_e, n_valid


def _moe(h, g_pre, router_w, wg, wu, wd, g_post):
    n, d = h.shape
    wr = jnp.pad(router_w, ((0, 0), (0, LANES - N_EXPERTS)))
    wr_hi = wr.astype(BF16)
    wr_mid = (wr - wr_hi.astype(F32)).astype(BF16)
    idx, gates, counts = _router(h, g_pre, jnp.stack([wr_hi, wr_mid]))
    dest, pad_start, pad_len, blk_e, n_valid = _route_plan(idx, counts, n)
    xs = _dispatch(dest, pad_start, pad_len, n_valid, h, g_pre)
    ys = _experts(blk_e, n_valid, xs, wg, wu, wd)
    return _combine(dest, ys, gates, g_post, h)


def kernel(x, mem, mix_norm_pre, mix_norm_post, w_in, b_forget, pool_w, pool_scale, w_out,
           xa_norm_pre, xa_norm_post, mem_norm, xa_wq, xa_wkv, xa_wo,
           ffn_norm_pre, ffn_norm_post, dense_w_gate, dense_w_up, dense_w_down,
           router_w, moe_w_gate, moe_w_up, moe_w_down):
    bsz, seq, d = x.shape
    m_len = mem.shape[1]
    depth = w_in.shape[0]
    n = bsz * seq
    h = x.reshape(n, d)
    mem2 = mem.reshape(bsz * m_len, d)
    row = lambda v: v.reshape(1, -1)

    idx = jnp.arange(SB_SUB)
    tri = (idx[:, None] >= idx[None, :]).astype(BF16)

    for li in range(depth):
        w = w_in[li]
        flog_w = jnp.pad(w[:, QKV_DIM:QKV_DIM + N_FOX], ((0, 0), (0, LANES - N_FOX)))
        w_cat = jnp.concatenate([w[:, :QKV_DIM], w[:, QKV_DIM + N_FOX:], flog_w], axis=1).astype(BF16)
        qkv, u, flog, qn, kmax_tiles = _in_proj(h, row(mix_norm_pre[li]), w_cat)
        qkv = qkv.reshape(bsz, seq, QKV_DIM)
        kmax = jnp.max(kmax_tiles[:, 0, :].reshape(bsz, -1, LANES), axis=1, keepdims=True)

        b_pad = jnp.pad(b_forget[li], (0, LANES - N_FOX)).reshape(1, LANES)
        by_seq = lambda v: v.reshape(bsz, seq, LANES)
        c, a_cols, zb_tiles = _logf_cumsum(by_seq(flog), b_pad, by_seq(qn), kmax)
        by_pair = lambda v: v[..., :N_FOX].transpose(0, 2, 1).reshape(bsz, N_PAIRS, 2, -1)
        c_rows = by_pair(c)
        zb_max = by_pair(zb_tiles[:, :, 0, :])

        sb = _sb_attention(qkv, tri).reshape(n, GROUP_DIM)
        fox = _fox_attention(qkv, a_cols, c_rows, zb_max).reshape(n, GROUP_DIM)

        pool_bd = jax.scipy.linalg.block_diag(*[pool_w[li, gi] for gi in range(len(POOL_WINDOWS))])
        k_mem, v_mem = _norm_matmul(mem2, row(mem_norm[li]), xa_wkv[li].astype(BF16),
                                    splits=[(0, d), (d, d)], dtypes=[BF16, BF16], tm=m_len)
        h = _mix_xattn(sb, fox, u, pool_bd.astype(BF16), row(pool_scale[li]), w_out[li].astype(BF16),
                       row(mix_norm_post[li]), h, row(xa_norm_pre[li]), xa_wq[li].astype(BF16),
                       k_mem.reshape(bsz, m_len, d), v_mem.reshape(bsz, m_len, d),
                       xa_wo[li].astype(BF16), row(xa_norm_post[li]), seq)

        j = li // 2
        if li % 2 == 0:
            h = _ffn(h, row(ffn_norm_pre[li]), dense_w_gate[j].astype(BF16), dense_w_up[j].astype(BF16),
                     dense_w_down[j].astype(BF16), row(ffn_norm_post[li]))
        else:
            h = _moe(h, row(ffn_norm_pre[li]), router_w[j], moe_w_gate[j].astype(BF16),
                     moe_w_up[j].astype(BF16), moe_w_down[j].astype(BF16), row(ffn_norm_post[li]))
    return h.reshape(bsz, seq, d)
```

```python
import functools

import jax
import jax.numpy as jnp
from jax import lax
from jax.experimental import pallas as pl
from jax.experimental.pallas import tpu as pltpu

F32 = jnp.float32
BF16 = jnp.bfloat16
EPS = 1e-6

HEAD_DIM = 64
LANES = 128
SUBLANES = 8
N_PAIRS = 3
GROUP_DIM = N_PAIRS * LANES
QKV_DIM = 6 * GROUP_DIM
POOL_DIM = 256
POOL_WINDOWS = (2, 4, 8, 16)
POOL_HALO = 16
N_FOX = 6
XA_HEADS = 4
N_EXPERTS = 8
TOP_K = 2
MOE_ROWS = 512
FF_CHUNK = 256
DMA_UNROLL = 8
SB_SUB = 256
SB_CHAINS = 8
FOX_SUB = 256
FOX_CHAINS = 8
FOX_SAFE_BOUND = 40.0
FOX_BOUND_ROWS = 1024
EXP_CUTOFF = 105.0
VMEM_LIMIT = 56 * 1024 * 1024

_NT = (((1,), (1,)), ((), ()))


def _params(n_axes):
    return pltpu.CompilerParams(dimension_semantics=("arbitrary",) * n_axes,
                                vmem_limit_bytes=VMEM_LIMIT)


def _rms(x, g):
    return x * lax.rsqrt(jnp.mean(x * x, axis=-1, keepdims=True) + EPS) * g


def _dot(a, b):
    return jnp.dot(a, b, preferred_element_type=F32)


def _split3(x):
    hi = x.astype(BF16)
    r1 = x - hi.astype(F32)
    mid = r1.astype(BF16)
    lo = (r1 - mid.astype(F32)).astype(BF16)
    return hi, mid, lo


def _norm_matmul_kernel(x_ref, g_ref, w_ref, *out_refs, splits):
    yb = _rms(x_ref[...], g_ref[...]).astype(BF16)
    for o_ref, (c0, width) in zip(out_refs, splits):
        for c in range(0, width, 256):
            cw = min(256, width - c)
            o_ref[:, c:c + cw] = _dot(yb, w_ref[:, c0 + c:c0 + c + cw]).astype(o_ref.dtype)


def _norm_matmul(x, g, w, splits, dtypes, tm):
    n, d = x.shape
    kern = functools.partial(_norm_matmul_kernel, splits=tuple(splits))
    return pl.pallas_call(
        kern,
        grid=(n // tm,),
        in_specs=[pl.BlockSpec((tm, d), lambda i: (i, 0)),
                  pl.BlockSpec((1, d), lambda i: (0, 0)),
                  pl.BlockSpec(w.shape, lambda i: (0, 0))],
        out_specs=[pl.BlockSpec((tm, wd), lambda i: (i, 0)) for (_, wd) in splits],
        out_shape=[jax.ShapeDtypeStruct((n, wd), dt) for (_, wd), dt in zip(splits, dtypes)],
        compiler_params=_params(1),
        name="norm_matmul",
    )(x, g, w)


def _in_proj_kernel(x_ref, g_ref, w_ref, sel_ref, qkv_ref, u_ref, fl_ref, qn_ref, kmax_ref):
    yb = _rms(x_ref[...], g_ref[...]).astype(BF16)
    for c in range(0, QKV_DIM, 256):
        qkv_ref[:, c:c + 256] = _dot(yb, w_ref[:, c:c + 256]).astype(BF16)
    u_ref[...] = _dot(yb, w_ref[:, QKV_DIM:QKV_DIM + POOL_DIM])
    fl_ref[...] = _dot(yb, w_ref[:, QKV_DIM + POOL_DIM:])

    def head_norms(c0):
        x = qkv_ref[:, c0:c0 + GROUP_DIM].astype(F32)
        sq = x * x
        hi = sq.astype(BF16)
        lo = (sq - hi.astype(F32)).astype(BF16)
        return _dot(hi, sel_ref[...]) + _dot(lo, sel_ref[...])

    qn_ref[...] = head_norms(3 * GROUP_DIM)
    kmax = jnp.max(head_norms(4 * GROUP_DIM), axis=0, keepdims=True)
    kmax_ref[...] = jnp.broadcast_to(kmax, kmax_ref.shape)


def _in_proj(x, g, w, tm=1024):
    n, d = x.shape
    lanes = jnp.arange(GROUP_DIM)[:, None] // HEAD_DIM == jnp.arange(LANES)[None, :]
    rows = lambda width: pl.BlockSpec((tm, width), lambda i: (i, 0))
    return pl.pallas_call(
        _in_proj_kernel,
        grid=(n // tm,),
        in_specs=[rows(d), pl.BlockSpec((1, d), lambda i: (0, 0)), pl.BlockSpec(w.shape, lambda i: (0, 0)),
                  pl.BlockSpec((GROUP_DIM, LANES), lambda i: (0, 0))],
        out_specs=[rows(QKV_DIM), rows(POOL_DIM), rows(LANES), rows(LANES),
                   pl.BlockSpec((None, SUBLANES, LANES), lambda i: (i, 0, 0))],
        out_shape=[jax.ShapeDtypeStruct((n, QKV_DIM), BF16), jax.ShapeDtypeStruct((n, POOL_DIM), F32),
                   jax.ShapeDtypeStruct((n, LANES), F32), jax.ShapeDtypeStruct((n, LANES), F32),
                   jax.ShapeDtypeStruct((n // tm, SUBLANES, LANES), F32)],
        compiler_params=_params(1),
        name="in_proj",
    )(x, g, w, lanes.astype(BF16))


def _logf_cumsum_kernel(fl_ref, b_ref, qn_ref, kmax_ref, c_ref, a_ref, zbmax_ref, carry_ref, *, tc):
    @pl.when(pl.program_id(1) == 0)
    def _():
        carry_ref[...] = jnp.zeros_like(carry_ref)

    x = fl_ref[...] + b_ref[...]
    ls = jnp.minimum(x, 0.0) - jnp.log(1.0 + jnp.exp(-jnp.abs(x)))
    row = lax.broadcasted_iota(jnp.int32, (tc, tc), 0)
    col = lax.broadcasted_iota(jnp.int32, (tc, tc), 1)
    tri = jnp.where(row >= col, 1.0, 0.0).astype(BF16)
    hi, mid, lo = _split3(ls)
    c = _dot(tri, hi) + _dot(tri, mid) + _dot(tri, lo) + carry_ref[...]
    c_ref[...] = c
    carry_ref[...] = c[tc - 1:tc, :]
    zb = jnp.sqrt(qn_ref[...] * kmax_ref[...] * (1.0 / HEAD_DIM)) * 1.001
    a_ref[...] = c - zb
    zbmax_ref[...] = jnp.broadcast_to(jnp.max(zb, axis=0, keepdims=True), zbmax_ref.shape)


def _logf_cumsum(fl, b_pad, qn, kmax, tc=FOX_BOUND_ROWS):
    bsz, s, _ = fl.shape
    rows = pl.BlockSpec((None, tc, LANES), lambda b, j: (b, j, 0))
    return pl.pallas_call(
        functools.partial(_logf_cumsum_kernel, tc=tc),
        grid=(bsz, s // tc),
        in_specs=[rows, pl.BlockSpec((1, LANES), lambda b, j: (0, 0)), rows,
                  pl.BlockSpec((None, 1, LANES), lambda b, j: (b, 0, 0))],
        out_specs=[rows, rows, pl.BlockSpec((None, None, SUBLANES, LANES), lambda b, j: (b, j, 0, 0))],
        out_shape=[jax.ShapeDtypeStruct(fl.shape, F32), jax.ShapeDtypeStruct(fl.shape, F32),
                   jax.ShapeDtypeStruct((bsz, s // tc, SUBLANES, LANES), F32)],
        scratch_shapes=[pltpu.VMEM((1, LANES), F32)],
        compiler_params=_params(2),
        name="logf_cumsum",
    )(fl, b_pad, qn, kmax)


def _sb_kernel(q_ref, k_ref, v_ref, tri_ref, o_ref, acc_ref, run_ref):
    sub, n_chain = SB_SUB, SB_CHAINS
    blk0 = pl.program_id(2) * n_chain
    lane = lax.broadcasted_iota(jnp.int32, (1, LANES), 1)
    in_head = (lane < HEAD_DIM, lane >= HEAD_DIM)
    q_all = q_ref[...] * jnp.asarray(HEAD_DIM ** -0.5, BF16)
    q_chain = [[jnp.where(in_head[hd], q_all[r * sub:(r + 1) * sub, :], jnp.zeros((), BF16))
                for r in range(n_chain)] for hd in range(2)]
    row = lax.broadcasted_iota(jnp.int32, (sub, sub), 0)
    col = lax.broadcasted_iota(jnp.int32, (sub, sub), 1)
    acc_ref[...] = jnp.zeros_like(acc_ref)
    run_ref[...] = jnp.zeros_like(run_ref)

    def tile(hd, r, d, diag):
        kb = blk0 + r - d
        start = pl.multiple_of(jnp.maximum(kb, 0) * sub, sub)
        k = k_ref[pl.ds(start, sub), :]
        v = jnp.where(in_head[hd], v_ref[pl.ds(start, sub), :], jnp.zeros((), BF16))
        z = lax.dot_general(q_chain[hd][r], k, _NT, preferred_element_type=F32)
        sp = jnp.maximum(z, 0.0) + jnp.log(1.0 + jnp.exp(-jnp.abs(z)))
        if diag:
            sp = jnp.where(col < row, sp, 0.0)
        suffix = _dot(sp.astype(BF16), tri_ref[...])
        run = run_ref[hd, r]
        w = jnp.exp(z - suffix - jnp.where(kb >= 0, run, 1e30))
        if diag:
            w = jnp.where(col < row, w, 0.0)
        acc_ref[hd, r] += _dot(w.astype(BF16), v)
        run_ref[hd, r] = run + suffix[:, 0:1]

    def more(d):
        go = jnp.bool_(False)
        for hd in range(2):
            for r in range(n_chain):
                unfinished = jnp.min(run_ref[hd, r]) < EXP_CUTOFF
                go = jnp.logical_or(go, jnp.logical_and(blk0 + r - d >= 0, unfinished))
        return go.astype(jnp.int32)

    def step(d, diag):
        for r in range(n_chain):
            for hd in range(2):
                tile(hd, r, d, diag)

    def body(carry):
        d, _ = carry
        step(d, False)
        return d + 1, more(d + 1)

    step(0, True)
    lax.while_loop(lambda carry: carry[1] > 0, body, (jnp.int32(1), more(1)))
    for r in range(n_chain):
        o_ref[r * sub:(r + 1) * sub, :] = (acc_ref[0, r] + acc_ref[1, r]).astype(o_ref.dtype)


def _sb_attention(qkv, tri):
    bsz, s, _ = qkv.shape
    tq = SB_SUB * SB_CHAINS
    return pl.pallas_call(
        _sb_kernel,
        grid=(bsz, N_PAIRS, s // tq),
        in_specs=[pl.BlockSpec((None, tq, LANES), lambda b, p, i: (b, i, p)),
                  pl.BlockSpec((None, s, LANES), lambda b, p, i: (b, 0, N_PAIRS + p)),
                  pl.BlockSpec((None, s, LANES), lambda b, p, i: (b, 0, 2 * N_PAIRS + p)),
                  pl.BlockSpec((SB_SUB, SB_SUB), lambda b, p, i: (0, 0))],
        out_specs=pl.BlockSpec((None, tq, LANES), lambda b, p, i: (b, i, p)),
        out_shape=jax.ShapeDtypeStruct((bsz, s, GROUP_DIM), BF16),
        scratch_shapes=[pltpu.VMEM((2, SB_CHAINS, SB_SUB, LANES), F32),
                        pltpu.VMEM((2, SB_CHAINS, SB_SUB, 1), F32)],
        compiler_params=_params(3),
        name="sb_attention",
    )(qkv, qkv, qkv, tri)


def _fox_kernel(q_ref, k_ref, v_ref, a_ref, crow_ref, cs_ref, zbmax_ref, o_ref, acc_ref, z_ref):
    sub, n_chain = FOX_SUB, FOX_CHAINS
    pair = pl.program_id(1)
    i = pl.program_id(2)
    lane = lax.broadcasted_iota(jnp.int32, (1, LANES), 1)
    in_head = (lane < HEAD_DIM, lane >= HEAD_DIM)
    one_lane = (HEAD_DIM, 0)

    q_all = q_ref[...] * jnp.asarray(HEAD_DIM ** -0.5, BF16)
    row = lax.broadcasted_iota(jnp.int32, (sub, sub), 0)
    col = lax.broadcasted_iota(jnp.int32, (sub, sub), 1)
    blk0 = i * n_chain
    acc_ref[...] = jnp.zeros_like(acc_ref)

    for hd in range(2):
        q_head = jnp.where(in_head[hd], q_all, jnp.zeros((), BF16))
        per_step = (sub * n_chain) // FOX_BOUND_ROWS
        zb_max = zbmax_ref[hd, i * per_step]
        for t in range(1, per_step):
            zb_max = jnp.maximum(zb_max, zbmax_ref[hd, i * per_step + t])
        cutoff = EXP_CUTOFF + 2.0 * zb_max
        c_base = cs_ref[hd, blk0 * sub]
        a_all = jnp.sum(jnp.where(lane == 2 * pair + hd, a_ref[...], 0.0), axis=1, keepdims=True) - c_base
        v_one = jnp.where(lane == one_lane[hd], 1.0, 0.0).astype(BF16)
        q_chain = [q_head[r * sub:(r + 1) * sub, :] for r in range(n_chain)]
        a_chain = [a_all[r * sub:(r + 1) * sub, :] for r in range(n_chain)]

        def tile_inputs(kb, hd=hd, c_base=c_base, v_one=v_one):
            start = pl.multiple_of(jnp.maximum(kb, 0) * sub, sub)
            k = k_ref[pl.ds(start, sub), :]
            v = jnp.where(in_head[hd], v_ref[pl.ds(start, sub), :], v_one)
            c_row = crow_ref[hd:hd + 1, pl.ds(start, sub)] - c_base
            return k, v, jnp.where(kb >= 0, c_row, 1e30)

        def more(d, cutoffs, hd=hd):
            go = jnp.bool_(False)
            for r in range(n_chain):
                kb = blk0 + r - d
                first = cs_ref[hd, (blk0 + r) * sub]
                last = cs_ref[hd, jnp.maximum(kb, 0) * sub + sub - 1]
                go = jnp.logical_or(go, jnp.logical_and(kb >= 0, first - last > -cutoffs[r]))
            return go

        def scores(r, d, q_chain=q_chain):
            start = pl.multiple_of(jnp.maximum(blk0 + r - d, 0) * sub, sub)
            return lax.dot_general(q_chain[r], k_ref[pl.ds(start, sub), :], _NT,
                                   preferred_element_type=F32)

        def fast_tile(r, d, z, diag, hd=hd, a_chain=a_chain, tile_inputs=tile_inputs):
            _, v, c_row = tile_inputs(blk0 + r - d)
            e = z + a_chain[r] - c_row
            if diag:
                e = jnp.where(col <= row, e, -1e30)
            acc_ref[hd, r] += _dot(jnp.exp(e).astype(BF16), v)
            return jnp.min(jnp.max(e, axis=1, keepdims=True)) if diag else None

        def slow_tile(r, d, m_old, diag, hd=hd, q_chain=q_chain, tile_inputs=tile_inputs):
            k, v, c_row = tile_inputs(blk0 + r - d)
            s = lax.dot_general(q_chain[r], k, _NT, preferred_element_type=F32) - c_row
            if diag:
                s = jnp.where(col <= row, s, -1e30)
            m_new = jnp.maximum(m_old, jnp.max(s, axis=1, keepdims=True))
            acc_ref[hd, r] = (acc_ref[hd, r] * jnp.exp(m_old - m_new)
                              + _dot(jnp.exp(s - m_new).astype(BF16), v))
            return m_new

        @pl.when(zb_max <= FOX_SAFE_BOUND)
        def _(fast_tile=fast_tile, scores=scores, more=more):
            cutoffs = []
            for r in range(n_chain):
                cutoffs.append(EXP_CUTOFF - fast_tile(r, 0, scores(r, 0), True))
                z_ref[r] = scores(r, 1)

            def body(d):
                for r in range(n_chain):
                    z = z_ref[r]
                    z_ref[r] = scores(r, d + 1)
                    fast_tile(r, d, z, False)
                return d + 1
            lax.while_loop(lambda d: more(d, cutoffs), body, jnp.int32(1))

        @pl.when(zb_max > FOX_SAFE_BOUND)
        def _(slow_tile=slow_tile, more=more):
            m0 = jnp.full((sub, 1), -1e30, F32)
            ms = tuple(slow_tile(r, 0, m0, True) for r in range(n_chain))

            def body(carry):
                d, ms = carry
                return d + 1, tuple(slow_tile(r, d, ms[r], False) for r in range(n_chain))
            lax.while_loop(lambda carry: more(carry[0], [cutoff] * n_chain), body, (jnp.int32(1), ms))

    for r in range(n_chain):
        acc = (acc_ref[0, r], acc_ref[1, r])
        total = [jnp.sum(jnp.where(lane == one_lane[hd], acc[hd], 0.0), axis=1, keepdims=True)
                 for hd in range(2)]
        out = jnp.where(in_head[0], acc[0] / total[0], acc[1] / total[1])
        o_ref[r * sub:(r + 1) * sub, :] = out.astype(o_ref.dtype)


def _fox_attention(qkv, a_cols, c_rows, zb_max):
    bsz, s, _ = qkv.shape
    tq = FOX_SUB * FOX_CHAINS
    return pl.pallas_call(
        _fox_kernel,
        grid=(bsz, N_PAIRS, s // tq),
        in_specs=[pl.BlockSpec((None, tq, LANES), lambda b, p, i: (b, i, 3 * N_PAIRS + p)),
                  pl.BlockSpec((None, s, LANES), lambda b, p, i: (b, 0, 4 * N_PAIRS + p)),
                  pl.BlockSpec((None, s, LANES), lambda b, p, i: (b, 0, 5 * N_PAIRS + p)),
                  pl.BlockSpec((None, tq, LANES), lambda b, p, i: (b, i, 0)),
                  pl.BlockSpec((None, None, 2, s), lambda b, p, i: (b, p, 0, 0)),
                  pl.BlockSpec((None, None, 2, s), lambda b, p, i: (b, p, 0, 0),
                               memory_space=pltpu.SMEM),
                  pl.BlockSpec((None, None, 2, s // FOX_BOUND_ROWS), lambda b, p, i: (b, p, 0, 0),
                               memory_space=pltpu.SMEM)],
        out_specs=pl.BlockSpec((None, tq, LANES), lambda b, p, i: (b, i, p)),
        out_shape=jax.ShapeDtypeStruct((bsz, s, GROUP_DIM), BF16),
        scratch_shapes=[pltpu.VMEM((2, FOX_CHAINS, FOX_SUB, LANES), F32),
                        pltpu.VMEM((FOX_CHAINS, FOX_SUB, FOX_SUB), F32)],
        compiler_params=_params(3),
        name="fox_attention",
    )(qkv, qkv, qkv, a_cols, c_rows, c_rows, zb_max)


def _pool(u, halo, pw_ref, ps_ref, pos0, tm):
    x = jnp.concatenate([halo, u], axis=0)
    s2 = x + pltpu.roll(x, 1, 0)
    s4 = s2 + pltpu.roll(s2, 2, 0)
    s8 = s4 + pltpu.roll(s4, 4, 0)
    s16 = s8 + pltpu.roll(s8, 8, 0)
    lane = lax.broadcasted_iota(jnp.int32, (1, POOL_DIM), 1)
    grp = POOL_DIM // len(POOL_WINDOWS)
    wsum = jnp.where(lane < grp, s2, jnp.where(lane < 2 * grp, s4, jnp.where(lane < 3 * grp, s8, s16)))
    win = jnp.where(lane < grp, 2, jnp.where(lane < 2 * grp, 4, jnp.where(lane < 3 * grp, 8, 16)))
    pos = pos0 + lax.broadcasted_iota(jnp.int32, (tm, 1), 0)
    count = jnp.minimum(pos + 1, win).astype(F32)
    r = wsum[POOL_HALO:, :] / count - u
    return _dot(r.astype(BF16), pw_ref[...]) * ps_ref[...]


def _mix_xattn_kernel(sb_ref, fox_ref, u_ref, halo_ref, pw_ref, ps_ref, wout_ref, gmix_ref, h_ref,
                      gpre_ref, wq_ref, k_ref, v_ref, wo_ref, gpost_ref, o_ref, *, tm, seq):
    pos0 = (pl.program_id(0) * tm) % seq
    halo = jnp.where(pos0 == 0, 0.0, halo_ref[...])
    pool = _pool(u_ref[...], halo, pw_ref, ps_ref, pos0, tm)
    gd = GROUP_DIM
    a = (_dot(sb_ref[...], wout_ref[0:gd, :]) + _dot(fox_ref[...], wout_ref[gd:2 * gd, :])
         + _dot(pool.astype(BF16), wout_ref[2 * gd:, :]))
    h = h_ref[...] + _rms(a, gmix_ref[...])

    d = h.shape[-1]
    hd = d // XA_HEADS
    hn = _rms(h, gpre_ref[...]).astype(BF16)
    q = (_dot(hn, wq_ref[...]) * (hd ** -0.5)).astype(BF16)
    outs = []
    for head in range(XA_HEADS):
        sl = slice(head * hd, (head + 1) * hd)
        s = lax.dot_general(q[:, sl], k_ref[:, sl], _NT, preferred_element_type=F32)
        p = jnp.exp(s - jnp.max(s, axis=1, keepdims=True))
        p = p * (1.0 / jnp.sum(p, axis=1, keepdims=True))
        outs.append(_dot(p.astype(BF16), v_ref[:, sl]).astype(BF16))
    c = _dot(jnp.concatenate(outs, axis=1), wo_ref[...])
    o_ref[...] = h + _rms(c, gpost_ref[...])


def _mix_xattn(sb, fox, u, pool_w_bd, pool_scale, w_out, g_mix, h, g_pre, wq, k_mem, v_mem, wo, g_post,
               seq, tm=1024):
    n, d = h.shape
    m_len = k_mem.shape[1]
    per_seq = seq // tm
    hb = tm // POOL_HALO
    const = lambda shape: pl.BlockSpec(shape, lambda i: (0,) * len(shape))
    rows = lambda width: pl.BlockSpec((tm, width), lambda i: (i, 0))
    mem = pl.BlockSpec((None, m_len, d), lambda i: (i // per_seq, 0, 0))
    return pl.pallas_call(
        functools.partial(_mix_xattn_kernel, tm=tm, seq=seq),
        grid=(n // tm,),
        in_specs=[rows(GROUP_DIM), rows(GROUP_DIM), rows(POOL_DIM),
                  pl.BlockSpec((POOL_HALO, POOL_DIM), lambda i: (jnp.maximum(i * hb - 1, 0), 0)),
                  const((POOL_DIM, POOL_DIM)), const((1, POOL_DIM)), const(w_out.shape), const((1, d)),
                  rows(d), const((1, d)), const((d, d)), mem, mem, const((d, d)), const((1, d))],
        out_specs=rows(d),
        out_shape=jax.ShapeDtypeStruct((n, d), F32),
        compiler_params=_params(1),
        name="mix_xattn",
    )(sb, fox, u, u, pool_w_bd, pool_scale, w_out, g_mix, h, g_pre, wq, k_mem, v_mem, wo, g_post)


def _silu(x):
    return x / (1.0 + jnp.exp(-x))


def _swiglu(x, wg_ref, wu_ref, wd_ref, act_ref):
    ff = wg_ref.shape[1]
    for c in range(0, ff, FF_CHUNK):
        gate = _dot(x, wg_ref[:, c:c + FF_CHUNK])
        up = _dot(x, wu_ref[:, c:c + FF_CHUNK])
        act_ref[:, c:c + FF_CHUNK] = (_silu(gate) * up).astype(BF16)
    return _dot(act_ref[...], wd_ref[...])


def _ffn_kernel(h_ref, gpre_ref, wg_ref, wu_ref, wd_ref, gpost_ref, o_ref, act_ref):
    h = h_ref[...]
    f = _swiglu(_rms(h, gpre_ref[...]).astype(BF16), wg_ref, wu_ref, wd_ref, act_ref)
    o_ref[...] = h + _rms(f, gpost_ref[...])


def _ffn(h, g_pre, wg, wu, wd, g_post, tm=512):
    n, d = h.shape
    ff = wg.shape[1]
    resident = pl.Buffered(1)
    return pl.pallas_call(
        _ffn_kernel,
        grid=(n // tm,),
        in_specs=[pl.BlockSpec((tm, d), lambda i: (i, 0)),
                  pl.BlockSpec((1, d), lambda i: (0, 0)),
                  pl.BlockSpec((d, ff), lambda i: (0, 0), pipeline_mode=resident),
                  pl.BlockSpec((d, ff), lambda i: (0, 0), pipeline_mode=resident),
                  pl.BlockSpec((ff, d), lambda i: (0, 0), pipeline_mode=resident),
                  pl.BlockSpec((1, d), lambda i: (0, 0))],
        out_specs=pl.BlockSpec((tm, d), lambda i: (i, 0)),
        out_shape=jax.ShapeDtypeStruct((n, d), F32),
        scratch_shapes=[pltpu.VMEM((tm, ff), BF16)],
        compiler_params=_params(1),
        name="ffn_dense",
    )(h, g_pre, wg, wu, wd, g_post)


def _router_kernel(h_ref, g_ref, wr_ref, tri_ref, idx_ref, gate_ref, count_ref, carry_ref):
    @pl.when(pl.program_id(0) == 0)
    def _():
        carry_ref[...] = jnp.zeros_like(carry_ref)

    hn = _rms(h_ref[...], g_ref[...])
    x_hi, x_mid, _ = _split3(hn)
    w_hi, w_mid = wr_ref[0], wr_ref[1]
    logits = _dot(x_hi, w_hi) + _dot(x_hi, w_mid) + _dot(x_mid, w_hi)
    tm = logits.shape[0]
    lane = lax.broadcasted_iota(jnp.int32, logits.shape, 1)
    logits = jnp.where(lane < N_EXPERTS, logits, -jnp.inf)
    m1 = jnp.max(logits, axis=1, keepdims=True)
    i1 = jnp.min(jnp.where(logits == m1, lane, LANES), axis=1, keepdims=True)
    rest = jnp.where(lane == i1, -jnp.inf, logits)
    m2 = jnp.max(rest, axis=1, keepdims=True)
    i2 = jnp.min(jnp.where(rest == m2, lane, LANES), axis=1, keepdims=True)
    e = jnp.exp(m2 - m1)
    g1 = 1.0 / (1.0 + e)
    gate_ref[...] = jnp.where(lane == 0, g1, jnp.where(lane == 1, e * g1, 0.0))

    onehot = jnp.where(lane == i1, 1.0, jnp.where(lane == i2, 1.0, 0.0))
    before = _dot(tri_ref[...], onehot.astype(BF16)) + carry_ref[...]
    r1 = jnp.sum(jnp.where(lane == i1, before, 0.0), axis=1, keepdims=True).astype(jnp.int32)
    r2 = jnp.sum(jnp.where(lane == i2, before, 0.0), axis=1, keepdims=True).astype(jnp.int32)
    idx_ref[...] = jnp.where(lane == 0, i1, jnp.where(lane == 1, i2,
                             jnp.where(lane == 2, r1, jnp.where(lane == 3, r2, 0))))
    total = before[tm - 1:tm, :] + onehot[tm - 1:tm, :]
    carry_ref[...] = total
    count_ref[...] = total


def _router(h, g, wr_split, tm=512):
    n, d = h.shape
    t = jnp.arange(tm)
    tri = (t[:, None] > t[None, :]).astype(BF16)
    return pl.pallas_call(
        _router_kernel,
        grid=(n // tm,),
        in_specs=[pl.BlockSpec((tm, d), lambda i: (i, 0)),
                  pl.BlockSpec((1, d), lambda i: (0, 0)),
                  pl.BlockSpec(wr_split.shape, lambda i: (0, 0, 0)),
                  pl.BlockSpec((tm, tm), lambda i: (0, 0))],
        out_specs=[pl.BlockSpec((tm, LANES), lambda i: (i, 0)),
                   pl.BlockSpec((tm, LANES), lambda i: (i, 0)),
                   pl.BlockSpec((1, LANES), lambda i: (0, 0))],
        out_shape=[jax.ShapeDtypeStruct((n, LANES), jnp.int32),
                   jax.ShapeDtypeStruct((n, LANES), F32),
                   jax.ShapeDtypeStruct((1, LANES), F32)],
        scratch_shapes=[pltpu.VMEM((1, LANES), F32)],
        compiler_params=_params(1),
        name="router",
    )(h, g, wr_split, tri)


def _rows_to_tiles(ref, x):
    m = x.shape[0]
    for s in range(SUBLANES):
        ref[pl.ds(s, m, stride=SUBLANES), :] = x[:, s * LANES:(s + 1) * LANES]


def _tiles_to_rows(ref, m):
    return jnp.concatenate([ref[pl.ds(s, m, stride=SUBLANES), :] for s in range(SUBLANES)], axis=1)


def _tile_rows(ref, first_row, rows):
    return ref.at[pl.ds(pl.multiple_of(first_row * SUBLANES, SUBLANES), rows * SUBLANES)]


def _dispatch_kernel(dest_ref, pad_start_ref, pad_len_ref, n_valid_ref, h_ref, g_ref, xs_hbm,
                     buf_ref, zero_ref, sem, zero_sem, *, tm):
    base = pl.program_id(0) * tm * TOP_K

    @pl.when(pl.program_id(0) == 0)
    def _():
        zero_ref[...] = jnp.zeros_like(zero_ref)
        n_blocks = n_valid_ref.shape[0]

        def zero_copy(first_row, rows):
            return pltpu.make_async_copy(_tile_rows(zero_ref, 0, rows), _tile_rows(xs_hbm, first_row, rows),
                                         zero_sem)

        def fill(wait):
            def go(cp):
                cp.wait() if wait else cp.start()

            for e in range(N_EXPERTS):
                pos = pad_start_ref[e]
                left = pad_len_ref[e]
                rows = MOE_ROWS // 2
                while rows >= 1:
                    take = (left & rows) != 0

                    @pl.when(take)
                    def _(pos=pos, rows=rows):
                        go(zero_copy(pos, rows))
                    pos = pos + jnp.where(take, rows, 0)
                    rows //= 2

            def blocks(b, carry):
                @pl.when(n_valid_ref[b] == 0)
                def _():
                    go(zero_copy(b * MOE_ROWS, MOE_ROWS))
                return carry
            lax.fori_loop(0, n_blocks, blocks, 0)

        fill(False)
        fill(True)

    _rows_to_tiles(buf_ref, _rms(h_ref[...], g_ref[...]))

    def start(r, carry):
        for k in range(TOP_K):
            dst = dest_ref[base + r * TOP_K + k]
            pltpu.make_async_copy(_tile_rows(buf_ref, r, 1), _tile_rows(xs_hbm, dst, 1), sem).start(priority=k)
        return carry

    lax.fori_loop(0, tm, start, 0, unroll=DMA_UNROLL)
    for _ in range(TOP_K):
        pltpu.make_async_copy(buf_ref, _tile_rows(xs_hbm, 0, tm), sem).wait()


def _dispatch(dest, pad_start, pad_len, n_valid, h, g, tm=1024):
    n, d = h.shape
    assert d == SUBLANES * LANES
    n_rows = n_valid.shape[0] * MOE_ROWS
    grid_spec = pltpu.PrefetchScalarGridSpec(
        num_scalar_prefetch=4,
        grid=(n // tm,),
        in_specs=[pl.BlockSpec((tm, d), lambda i, *_: (i, 0)),
                  pl.BlockSpec((1, d), lambda i, *_: (0, 0))],
        out_specs=pl.BlockSpec(memory_space=pl.ANY),
        scratch_shapes=[pltpu.VMEM((tm * SUBLANES, LANES), F32), pltpu.VMEM((MOE_ROWS * SUBLANES, LANES), F32),
                        pltpu.SemaphoreType.DMA, pltpu.SemaphoreType.DMA],
    )
    return pl.pallas_call(
        functools.partial(_dispatch_kernel, tm=tm),
        grid_spec=grid_spec,
        out_shape=jax.ShapeDtypeStruct((n_rows * SUBLANES, LANES), F32),
        compiler_params=_params(1),
        name="moe_dispatch",
    )(dest, pad_start, pad_len, n_valid, h, g)


def _expert_kernel(blk_e_ref, n_valid_ref, xs_ref, wg_ref, wu_ref, wd_ref, ys_ref, act_ref):
    n_valid = n_valid_ref[pl.program_id(0)]

    @pl.when(n_valid > 0)
    def _():
        x = _tiles_to_rows(xs_ref, MOE_ROWS).astype(BF16)
        _rows_to_tiles(ys_ref, _swiglu(x, wg_ref, wu_ref, wd_ref, act_ref))

    @pl.when(n_valid == 0)
    def _():
        ys_ref[...] = jnp.zeros_like(ys_ref)


def _experts(blk_e, n_valid, xs, wg, wu, wd):
    d, ff = wg.shape[1], wg.shape[2]
    blk = MOE_ROWS * SUBLANES
    grid_spec = pltpu.PrefetchScalarGridSpec(
        num_scalar_prefetch=2,
        grid=(xs.shape[0] // blk,),
        in_specs=[pl.BlockSpec((blk, LANES), lambda i, be, nv: (i, 0)),
                  pl.BlockSpec((None, d, ff), lambda i, be, nv: (be[i], 0, 0)),
                  pl.BlockSpec((None, d, ff), lambda i, be, nv: (be[i], 0, 0)),
                  pl.BlockSpec((None, ff, d), lambda i, be, nv: (be[i], 0, 0))],
        out_specs=pl.BlockSpec((blk, LANES), lambda i, be, nv: (i, 0)),
        scratch_shapes=[pltpu.VMEM((MOE_ROWS, ff), BF16)],
    )
    return pl.pallas_call(
        _expert_kernel,
        grid_spec=grid_spec,
        out_shape=jax.ShapeDtypeStruct(xs.shape, F32),
        compiler_params=_params(1),
        name="moe_experts",
    )(blk_e, n_valid, xs, wg, wu, wd)


def _combine_kernel(dest_ref, ys_hbm, gate_ref, g_ref, h_ref, o_ref, buf_ref, sem, *, tm):
    i = pl.program_id(0)

    def fetch(tile, slot):
        def body(r, carry):
            for k in range(TOP_K):
                src = dest_ref[(tile * tm + r) * TOP_K + k]
                pltpu.make_async_copy(_tile_rows(ys_hbm, src, 1), _tile_rows(buf_ref.at[slot, k], r, 1),
                                      sem.at[slot]).start(priority=k)
            return carry
        lax.fori_loop(0, tm, body, 0, unroll=DMA_UNROLL)

    @pl.when(i == 0)
    def _():
        fetch(0, 0)

    @pl.when(i + 1 < pl.num_programs(0))
    def _():
        fetch(i + 1, (i + 1) % 2)

    slot = i % 2

    for k in range(TOP_K):
        pltpu.make_async_copy(_tile_rows(ys_hbm, 0, tm), buf_ref.at[slot, k], sem.at[slot]).wait()
    gates = gate_ref[...]
    f = (_tiles_to_rows(buf_ref.at[slot, 0], tm) * gates[:, 0:1]
         + _tiles_to_rows(buf_ref.at[slot, 1], tm) * gates[:, 1:2])
    o_ref[...] = h_ref[...] + _rms(f, g_ref[...])


def _combine(dest, ys, gates, g, h, tm=512):
    n, d = h.shape
    grid_spec = pltpu.PrefetchScalarGridSpec(
        num_scalar_prefetch=1,
        grid=(n // tm,),
        in_specs=[pl.BlockSpec(memory_space=pl.ANY),
                  pl.BlockSpec((tm, LANES), lambda i, dst: (i, 0)),
                  pl.BlockSpec((1, d), lambda i, dst: (0, 0)),
                  pl.BlockSpec((tm, d), lambda i, dst: (i, 0))],
        out_specs=pl.BlockSpec((tm, d), lambda i, dst: (i, 0)),
        scratch_shapes=[pltpu.VMEM((2, TOP_K, tm * SUBLANES, LANES), F32), pltpu.SemaphoreType.DMA((2,))],
    )
    return pl.pallas_call(
        functools.partial(_combine_kernel, tm=tm),
        grid_spec=grid_spec,
        out_shape=jax.ShapeDtypeStruct((n, d), F32),
        compiler_params=_params(1),
        name="moe_combine",
    )(dest, ys, gates, g, h)


def _route_plan(idx, counts_f, n_tok):
    counts = counts_f[0, :N_EXPERTS].astype(jnp.int32)
    padded = ((counts + MOE_ROWS - 1) // MOE_ROWS) * MOE_ROWS
    pend = jnp.cumsum(padded)
    pstart = pend - padded
    expert = idx[:, 0:TOP_K]
    rank = idx[:, TOP_K:2 * TOP_K]
    offset = jnp.zeros_like(expert)
    for e in range(N_EXPERTS):
        offset = jnp.where(expert == e, pstart[e], offset)
    dest = (offset + rank).reshape(-1)
    n_blocks = (n_tok * TOP_K) // MOE_ROWS + N_EXPERTS
    blk_start = jnp.arange(n_blocks, dtype=jnp.int32) * MOE_ROWS
    blk_e = jnp.minimum(jnp.sum(blk_start[:, None] >= pend[None, :], axis=1), N_EXPERTS - 1).astype(jnp.int32)
    n_valid = jnp.clip(pstart[blk_e] + counts[blk_e] - blk_start, 0, MOE_ROWS).astype(jnp.int32)
    return dest, pstart + counts, padded - counts, blk_e, n_valid


def _moe(h, g_pre, router_w, wg, wu, wd, g_post):
    n, d = h.shape
    wr = jnp.pad(router_w, ((0, 0), (0, LANES - N_EXPERTS)))
    wr_hi = wr.astype(BF16)
    wr_mid = (wr - wr_hi.astype(F32)).astype(BF16)
    idx, gates, counts = _router(h, g_pre, jnp.stack([wr_hi, wr_mid]))
    dest, pad_start, pad_len, blk_e, n_valid = _route_plan(idx, counts, n)
    xs = _dispatch(dest, pad_start, pad_len, n_valid, h, g_pre)
    ys = _experts(blk_e, n_valid, xs, wg, wu, wd)
    return _combine(dest, ys, gates, g_post, h)


def kernel(x, mem, mix_norm_pre, mix_norm_post, w_in, b_forget, pool_w, pool_scale, w_out,
           xa_norm_pre, xa_norm_post, mem_norm, xa_wq, xa_wkv, xa_wo,
           ffn_norm_pre, ffn_norm_post, dense_w_gate, dense_w_up, dense_w_down,
           router_w, moe_w_gate, moe_w_up, moe_w_down):
    bsz, seq, d = x.shape
    m_len = mem.shape[1]
    depth = w_in.shape[0]
    n = bsz * seq
    h = x.reshape(n, d)
    mem2 = mem.reshape(bsz * m_len, d)
    row = lambda v: v.reshape(1, -1)

    idx = jnp.arange(SB_SUB)
    tri = (idx[:, None] >= idx[None, :]).astype(BF16)

    for li in range(depth):
        w = w_in[li]
        flog_w = jnp.pad(w[:, QKV_DIM:QKV_DIM + N_FOX], ((0, 0), (0, LANES - N_FOX)))
        w_cat = jnp.concatenate([w[:, :QKV_DIM], w[:, QKV_DIM + N_FOX:], flog_w], axis=1).astype(BF16)
        qkv, u, flog, qn, kmax_tiles = _in_proj(h, row(mix_norm_pre[li]), w_cat)
        qkv = qkv.reshape(bsz, seq, QKV_DIM)
        kmax = jnp.max(kmax_tiles[:, 0, :].reshape(bsz, -1, LANES), axis=1, keepdims=True)

        b_pad = jnp.pad(b_forget[li], (0, LANES - N_FOX)).reshape(1, LANES)
        by_seq = lambda v: v.reshape(bsz, seq, LANES)
        c, a_cols, zb_tiles = _logf_cumsum(by_seq(flog), b_pad, by_seq(qn), kmax)
        by_pair = lambda v: v[..., :N_FOX].transpose(0, 2, 1).reshape(bsz, N_PAIRS, 2, -1)
        c_rows = by_pair(c)
        zb_max = by_pair(zb_tiles[:, :, 0, :])

        sb = _sb_attention(qkv, tri).reshape(n, GROUP_DIM)
        fox = _fox_attention(qkv, a_cols, c_rows, zb_max).reshape(n, GROUP_DIM)

        pool_bd = jax.scipy.linalg.block_diag(*[pool_w[li, gi] for gi in range(len(POOL_WINDOWS))])
        k_mem, v_mem = _norm_matmul(mem2, row(mem_norm[li]), xa_wkv[li].astype(BF16),
                                    splits=[(0, d), (d, d)], dtypes=[BF16, BF16], tm=m_len)
        h = _mix_xattn(sb, fox, u, pool_bd.astype(BF16), row(pool_scale[li]), w_out[li].astype(BF16),
                       row(mix_norm_post[li]), h, row(xa_norm_pre[li]), xa_wq[li].astype(BF16),
                       k_mem.reshape(bsz, m_len, d), v_mem.reshape(bsz, m_len, d),
                       xa_wo[li].astype(BF16), row(xa_norm_post[li]), seq)

        j = li // 2
        if li % 2 == 0:
            h = _ffn(h, row(ffn_norm_pre[li]), dense_w_gate[j].astype(BF16), dense_w_up[j].astype(BF16),
                     dense_w_down[j].astype(BF16), row(ffn_norm_post[li]))
        else:
            h = _moe(h, row(ffn_norm_pre[li]), router_w[j], moe_w_gate[j].astype(BF16),
                     moe_w_up[j].astype(BF16), moe_w_down[j].astype(BF16), row(ffn_norm_post[li]))
    return h.reshape(bsz, seq, d)
```

```python
import functools

import jax
import jax.numpy as jnp
from jax import lax
from jax.experimental import pallas as pl
from jax.experimental.pallas import tpu as pltpu

F32 = jnp.float32
BF16 = jnp.bfloat16
EPS = 1e-6

HEAD_DIM = 64
LANES = 128
SUBLANES = 8
N_PAIRS = 3
GROUP_DIM = N_PAIRS * LANES
QKV_DIM = 6 * GROUP_DIM
POOL_DIM = 256
POOL_WINDOWS = (2, 4, 8, 16)
POOL_HALO = 16
N_FOX = 6
XA_HEADS = 4
N_EXPERTS = 8
TOP_K = 2
MOE_ROWS = 512
FF_CHUNK = 256
DMA_UNROLL = 8
SB_SUB = 256
SB_CHAINS = 8
FOX_SUB = 256
FOX_CHAINS = 8
FOX_SAFE_BOUND = 40.0
FOX_BOUND_ROWS = 1024
EXP_CUTOFF = 105.0
VMEM_LIMIT = 56 * 1024 * 1024

_NT = (((1,), (1,)), ((), ()))


def _params(n_axes):
    return pltpu.CompilerParams(dimension_semantics=("arbitrary",) * n_axes,
                                vmem_limit_bytes=VMEM_LIMIT)


def _rms(x, g):
    return x * lax.rsqrt(jnp.mean(x * x, axis=-1, keepdims=True) + EPS) * g


def _dot(a, b):
    return jnp.dot(a, b, preferred_element_type=F32)


def _split3(x):
    hi = x.astype(BF16)
    r1 = x - hi.astype(F32)
    mid = r1.astype(BF16)
    lo = (r1 - mid.astype(F32)).astype(BF16)
    return hi, mid, lo


def _norm_matmul_kernel(x_ref, g_ref, w_ref, *out_refs, splits):
    yb = _rms(x_ref[...], g_ref[...]).astype(BF16)
    for o_ref, (c0, width) in zip(out_refs, splits):
        for c in range(0, width, 256):
            cw = min(256, width - c)
            o_ref[:, c:c + cw] = _dot(yb, w_ref[:, c0 + c:c0 + c + cw]).astype(o_ref.dtype)


def _norm_matmul(x, g, w, splits, dtypes, tm):
    n, d = x.shape
    kern = functools.partial(_norm_matmul_kernel, splits=tuple(splits))
    return pl.pallas_call(
        kern,
        grid=(n // tm,),
        in_specs=[pl.BlockSpec((tm, d), lambda i: (i, 0)),
                  pl.BlockSpec((1, d), lambda i: (0, 0)),
                  pl.BlockSpec(w.shape, lambda i: (0, 0))],
        out_specs=[pl.BlockSpec((tm, wd), lambda i: (i, 0)) for (_, wd) in splits],
        out_shape=[jax.ShapeDtypeStruct((n, wd), dt) for (_, wd), dt in zip(splits, dtypes)],
        compiler_params=_params(1),
        name="norm_matmul",
    )(x, g, w)


def _in_proj_kernel(x_ref, g_ref, w_ref, sel_ref, qkv_ref, u_ref, fl_ref, qn_ref, kmax_ref):
    yb = _rms(x_ref[...], g_ref[...]).astype(BF16)
    for c in range(0, QKV_DIM, 256):
        qkv_ref[:, c:c + 256] = _dot(yb, w_ref[:, c:c + 256]).astype(BF16)
    u_ref[...] = _dot(yb, w_ref[:, QKV_DIM:QKV_DIM + POOL_DIM])
    fl_ref[...] = _dot(yb, w_ref[:, QKV_DIM + POOL_DIM:])

    def head_norms(c0):
        x = qkv_ref[:, c0:c0 + GROUP_DIM].astype(F32)
        sq = x * x
        hi = sq.astype(BF16)
        lo = (sq - hi.astype(F32)).astype(BF16)
        return _dot(hi, sel_ref[...]) + _dot(lo, sel_ref[...])

    qn_ref[...] = head_norms(3 * GROUP_DIM)
    kmax = jnp.max(head_norms(4 * GROUP_DIM), axis=0, keepdims=True)
    kmax_ref[...] = jnp.broadcast_to(kmax, kmax_ref.shape)


def _in_proj(x, g, w, tm=1024):
    n, d = x.shape
    lanes = jnp.arange(GROUP_DIM)[:, None] // HEAD_DIM == jnp.arange(LANES)[None, :]
    rows = lambda width: pl.BlockSpec((tm, width), lambda i: (i, 0))
    return pl.pallas_call(
        _in_proj_kernel,
        grid=(n // tm,),
        in_specs=[rows(d), pl.BlockSpec((1, d), lambda i: (0, 0)), pl.BlockSpec(w.shape, lambda i: (0, 0)),
                  pl.BlockSpec((GROUP_DIM, LANES), lambda i: (0, 0))],
        out_specs=[rows(QKV_DIM), rows(POOL_DIM), rows(LANES), rows(LANES),
                   pl.BlockSpec((None, SUBLANES, LANES), lambda i: (i, 0, 0))],
        out_shape=[jax.ShapeDtypeStruct((n, QKV_DIM), BF16), jax.ShapeDtypeStruct((n, POOL_DIM), F32),
                   jax.ShapeDtypeStruct((n, LANES), F32), jax.ShapeDtypeStruct((n, LANES), F32),
                   jax.ShapeDtypeStruct((n // tm, SUBLANES, LANES), F32)],
        compiler_params=_params(1),
        name="in_proj",
    )(x, g, w, lanes.astype(BF16))


def _logf_cumsum_kernel(fl_ref, b_ref, qn_ref, kmax_ref, c_ref, a_ref, zbmax_ref, carry_ref, *, tc):
    @pl.when(pl.program_id(1) == 0)
    def _():
        carry_ref[...] = jnp.zeros_like(carry_ref)

    x = fl_ref[...] + b_ref[...]
    ls = jnp.minimum(x, 0.0) - jnp.log(1.0 + jnp.exp(-jnp.abs(x)))
    row = lax.broadcasted_iota(jnp.int32, (tc, tc), 0)
    col = lax.broadcasted_iota(jnp.int32, (tc, tc), 1)
    tri = jnp.where(row >= col, 1.0, 0.0).astype(BF16)
    hi, mid, lo = _split3(ls)
    c = _dot(tri, hi) + _dot(tri, mid) + _dot(tri, lo) + carry_ref[...]
    c_ref[...] = c
    carry_ref[...] = c[tc - 1:tc, :]
    zb = jnp.sqrt(qn_ref[...] * kmax_ref[...] * (1.0 / HEAD_DIM)) * 1.001
    a_ref[...] = c - zb
    zbmax_ref[...] = jnp.broadcast_to(jnp.max(zb, axis=0, keepdims=True), zbmax_ref.shape)


def _logf_cumsum(fl, b_pad, qn, kmax, tc=FOX_BOUND_ROWS):
    bsz, s, _ = fl.shape
    rows = pl.BlockSpec((None, tc, LANES), lambda b, j: (b, j, 0))
    return pl.pallas_call(
        functools.partial(_logf_cumsum_kernel, tc=tc),
        grid=(bsz, s // tc),
        in_specs=[rows, pl.BlockSpec((1, LANES), lambda b, j: (0, 0)), rows,
                  pl.BlockSpec((None, 1, LANES), lambda b, j: (b, 0, 0))],
        out_specs=[rows, rows, pl.BlockSpec((None, None, SUBLANES, LANES), lambda b, j: (b, j, 0, 0))],
        out_shape=[jax.ShapeDtypeStruct(fl.shape, F32), jax.ShapeDtypeStruct(fl.shape, F32),
                   jax.ShapeDtypeStruct((bsz, s // tc, SUBLANES, LANES), F32)],
        scratch_shapes=[pltpu.VMEM((1, LANES), F32)],
        compiler_params=_params(2),
        name="logf_cumsum",
    )(fl, b_pad, qn, kmax)


def _sb_kernel(q_ref, k_ref, v_ref, tri_ref, o_ref, acc_ref, run_ref):
    sub, n_chain = SB_SUB, SB_CHAINS
    blk0 = pl.program_id(2) * n_chain
    lane = lax.broadcasted_iota(jnp.int32, (1, LANES), 1)
    in_head = (lane < HEAD_DIM, lane >= HEAD_DIM)
    row = lax.broadcasted_iota(jnp.int32, (sub, sub), 0)
    col = lax.broadcasted_iota(jnp.int32, (sub, sub), 1)
    acc_ref[...] = jnp.zeros_like(acc_ref)
    run_ref[...] = jnp.zeros_like(run_ref)

    def tile(hd, r, d, diag):
        kb = blk0 + r - d
        start = pl.multiple_of(jnp.maximum(kb, 0) * sub, sub)
        k = k_ref[pl.ds(start, sub), :]
        v = jnp.where(in_head[hd], v_ref[pl.ds(start, sub), :], jnp.zeros((), BF16))
        q = q_ref[r * sub:(r + 1) * sub, :] * jnp.asarray(HEAD_DIM ** -0.5, BF16)
        q = jnp.where(in_head[hd], q, jnp.zeros((), BF16))
        z = lax.dot_general(q, k, _NT, preferred_element_type=F32)
        sp = jnp.maximum(z, 0.0) + jnp.log(1.0 + jnp.exp(-jnp.abs(z)))
        if diag:
            sp = jnp.where(col < row, sp, 0.0)
        suffix = _dot(sp.astype(BF16), tri_ref[...])
        run = run_ref[hd, r]
        w = jnp.exp(z - suffix - jnp.where(kb >= 0, run, 1e30))
        if diag:
            w = jnp.where(col < row, w, 0.0)
        acc_ref[hd, r] += _dot(w.astype(BF16), v)
        run_ref[hd, r] = run + suffix[:, 0:1]

    def more(d):
        go = jnp.bool_(False)
        for hd in range(2):
            for r in range(n_chain):
                unfinished = jnp.min(run_ref[hd, r]) < EXP_CUTOFF
                go = jnp.logical_or(go, jnp.logical_and(blk0 + r - d >= 0, unfinished))
        return go.astype(jnp.int32)

    def step(d, diag):
        for r in range(n_chain):
            for hd in range(2):
                tile(hd, r, d, diag)

    def body(carry):
        d, _ = carry
        step(d, False)
        return d + 1, more(d + 1)

    step(0, True)
    lax.while_loop(lambda carry: carry[1] > 0, body, (jnp.int32(1), more(1)))
    for r in range(n_chain):
        o_ref[r * sub:(r + 1) * sub, :] = (acc_ref[0, r] + acc_ref[1, r]).astype(o_ref.dtype)


def _sb_attention(qkv, tri):
    bsz, s, _ = qkv.shape
    tq = SB_SUB * SB_CHAINS
    return pl.pallas_call(
        _sb_kernel,
        grid=(bsz, N_PAIRS, s // tq),
        in_specs=[pl.BlockSpec((None, tq, LANES), lambda b, p, i: (b, i, p)),
                  pl.BlockSpec((None, s, LANES), lambda b, p, i: (b, 0, N_PAIRS + p)),
                  pl.BlockSpec((None, s, LANES), lambda b, p, i: (b, 0, 2 * N_PAIRS + p)),
                  pl.BlockSpec((SB_SUB, SB_SUB), lambda b, p, i: (0, 0))],
        out_specs=pl.BlockSpec((None, tq, LANES), lambda b, p, i: (b, i, p)),
        out_shape=jax.ShapeDtypeStruct((bsz, s, GROUP_DIM), BF16),
        scratch_shapes=[pltpu.VMEM((2, SB_CHAINS, SB_SUB, LANES), F32),
                        pltpu.VMEM((2, SB_CHAINS, SB_SUB, 1), F32)],
        compiler_params=_params(3),
        name="sb_attention",
    )(qkv, qkv, qkv, tri)


def _fox_kernel(q_ref, k_ref, v_ref, a_ref, crow_ref, cs_ref, zbmax_ref, o_ref, acc_ref, z_ref):
    sub, n_chain = FOX_SUB, FOX_CHAINS
    pair = pl.program_id(1)
    i = pl.program_id(2)
    lane = lax.broadcasted_iota(jnp.int32, (1, LANES), 1)
    in_head = (lane < HEAD_DIM, lane >= HEAD_DIM)
    one_lane = (HEAD_DIM, 0)

    q_all = q_ref[...] * jnp.asarray(HEAD_DIM ** -0.5, BF16)
    row = lax.broadcasted_iota(jnp.int32, (sub, sub), 0)
    col = lax.broadcasted_iota(jnp.int32, (sub, sub), 1)
    blk0 = i * n_chain
    acc_ref[...] = jnp.zeros_like(acc_ref)

    for hd in range(2):
        q_head = jnp.where(in_head[hd], q_all, jnp.zeros((), BF16))
        per_step = (sub * n_chain) // FOX_BOUND_ROWS
        zb_max = zbmax_ref[hd, i * per_step]
        for t in range(1, per_step):
            zb_max = jnp.maximum(zb_max, zbmax_ref[hd, i * per_step + t])
        cutoff = EXP_CUTOFF + 2.0 * zb_max
        c_base = cs_ref[hd, blk0 * sub]
        a_all = jnp.sum(jnp.where(lane == 2 * pair + hd, a_ref[...], 0.0), axis=1, keepdims=True) - c_base
        v_one = jnp.where(lane == one_lane[hd], 1.0, 0.0).astype(BF16)
        q_chain = [q_head[r * sub:(r + 1) * sub, :] for r in range(n_chain)]
        a_chain = [a_all[r * sub:(r + 1) * sub, :] for r in range(n_chain)]

        def tile_inputs(kb, hd=hd, c_base=c_base, v_one=v_one):
            start = pl.multiple_of(jnp.maximum(kb, 0) * sub, sub)
            k = k_ref[pl.ds(start, sub), :]
            v = jnp.where(in_head[hd], v_ref[pl.ds(start, sub), :], v_one)
            c_row = crow_ref[hd:hd + 1, pl.ds(start, sub)] - c_base
            return k, v, jnp.where(kb >= 0, c_row, 1e30)

        def more(d, cutoffs, hd=hd):
            go = jnp.bool_(False)
            for r in range(n_chain):
                kb = blk0 + r - d
                first = cs_ref[hd, (blk0 + r) * sub]
                last = cs_ref[hd, jnp.maximum(kb, 0) * sub + sub - 1]
                go = jnp.logical_or(go, jnp.logical_and(kb >= 0, first - last > -cutoffs[r]))
            return go

        def scores(r, d, q_chain=q_chain):
            start = pl.multiple_of(jnp.maximum(blk0 + r - d, 0) * sub, sub)
            return lax.dot_general(q_chain[r], k_ref[pl.ds(start, sub), :], _NT,
                                   preferred_element_type=F32)

        def fast_tile(r, d, z, diag, hd=hd, a_chain=a_chain, tile_inputs=tile_inputs):
            _, v, c_row = tile_inputs(blk0 + r - d)
            e = z + a_chain[r] - c_row
            if diag:
                e = jnp.where(col <= row, e, -1e30)
            acc_ref[hd, r] += _dot(jnp.exp(e).astype(BF16), v)
            return jnp.min(jnp.max(e, axis=1, keepdims=True)) if diag else None

        def slow_tile(r, d, m_old, diag, hd=hd, q_chain=q_chain, tile_inputs=tile_inputs):
            k, v, c_row = tile_inputs(blk0 + r - d)
            s = lax.dot_general(q_chain[r], k, _NT, preferred_element_type=F32) - c_row
            if diag:
                s = jnp.where(col <= row, s, -1e30)
            m_new = jnp.maximum(m_old, jnp.max(s, axis=1, keepdims=True))
            acc_ref[hd, r] = (acc_ref[hd, r] * jnp.exp(m_old - m_new)
                              + _dot(jnp.exp(s - m_new).astype(BF16), v))
            return m_new

        @pl.when(zb_max <= FOX_SAFE_BOUND)
        def _(fast_tile=fast_tile, scores=scores, more=more):
            cutoffs = []
            for r in range(n_chain):
                cutoffs.append(EXP_CUTOFF - fast_tile(r, 0, scores(r, 0), True))
                z_ref[r] = scores(r, 1)

            def body(d):
                for r in range(n_chain):
                    z = z_ref[r]
                    z_ref[r] = scores(r, d + 1)
                    fast_tile(r, d, z, False)
                return d + 1
            lax.while_loop(lambda d: more(d, cutoffs), body, jnp.int32(1))

        @pl.when(zb_max > FOX_SAFE_BOUND)
        def _(slow_tile=slow_tile, more=more):
            m0 = jnp.full((sub, 1), -1e30, F32)
            ms = tuple(slow_tile(r, 0, m0, True) for r in range(n_chain))

            def body(carry):
                d, ms = carry
                return d + 1, tuple(slow_tile(r, d, ms[r], False) for r in range(n_chain))
            lax.while_loop(lambda carry: more(carry[0], [cutoff] * n_chain), body, (jnp.int32(1), ms))

    for r in range(n_chain):
        acc = (acc_ref[0, r], acc_ref[1, r])
        total = [jnp.sum(jnp.where(lane == one_lane[hd], acc[hd], 0.0), axis=1, keepdims=True)
                 for hd in range(2)]
        out = jnp.where(in_head[0], acc[0] / total[0], acc[1] / total[1])
        o_ref[r * sub:(r + 1) * sub, :] = out.astype(o_ref.dtype)


def _fox_attention(qkv, a_cols, c_rows, zb_max):
    bsz, s, _ = qkv.shape
    tq = FOX_SUB * FOX_CHAINS
    return pl.pallas_call(
        _fox_kernel,
        grid=(bsz, N_PAIRS, s // tq),
        in_specs=[pl.BlockSpec((None, tq, LANES), lambda b, p, i: (b, i, 3 * N_PAIRS + p)),
                  pl.BlockSpec((None, s, LANES), lambda b, p, i: (b, 0, 4 * N_PAIRS + p)),
                  pl.BlockSpec((None, s, LANES), lambda b, p, i: (b, 0, 5 * N_PAIRS + p)),
                  pl.BlockSpec((None, tq, LANES), lambda b, p, i: (b, i, 0)),
                  pl.BlockSpec((None, None, 2, s), lambda b, p, i: (b, p, 0, 0)),
                  pl.BlockSpec((None, None, 2, s), lambda b, p, i: (b, p, 0, 0),
                               memory_space=pltpu.SMEM),
                  pl.BlockSpec((None, None, 2, s // FOX_BOUND_ROWS), lambda b, p, i: (b, p, 0, 0),
                               memory_space=pltpu.SMEM)],
        out_specs=pl.BlockSpec((None, tq, LANES), lambda b, p, i: (b, i, p)),
        out_shape=jax.ShapeDtypeStruct((bsz, s, GROUP_DIM), BF16),
        scratch_shapes=[pltpu.VMEM((2, FOX_CHAINS, FOX_SUB, LANES), F32),
                        pltpu.VMEM((FOX_CHAINS, FOX_SUB, FOX_SUB), F32)],
        compiler_params=_params(3),
        name="fox_attention",
    )(qkv, qkv, qkv, a_cols, c_rows, c_rows, zb_max)


def _pool(u, halo, pw_ref, ps_ref, pos0, tm):
    x = jnp.concatenate([halo, u], axis=0)
    s2 = x + pltpu.roll(x, 1, 0)
    s4 = s2 + pltpu.roll(s2, 2, 0)
    s8 = s4 + pltpu.roll(s4, 4, 0)
    s16 = s8 + pltpu.roll(s8, 8, 0)
    lane = lax.broadcasted_iota(jnp.int32, (1, POOL_DIM), 1)
    grp = POOL_DIM // len(POOL_WINDOWS)
    wsum = jnp.where(lane < grp, s2, jnp.where(lane < 2 * grp, s4, jnp.where(lane < 3 * grp, s8, s16)))
    win = jnp.where(lane < grp, 2, jnp.where(lane < 2 * grp, 4, jnp.where(lane < 3 * grp, 8, 16)))
    pos = pos0 + lax.broadcasted_iota(jnp.int32, (tm, 1), 0)
    count = jnp.minimum(pos + 1, win).astype(F32)
    r = wsum[POOL_HALO:, :] / count - u
    return _dot(r.astype(BF16), pw_ref[...]) * ps_ref[...]


def _mix_xattn_kernel(sb_ref, fox_ref, u_ref, halo_ref, pw_ref, ps_ref, wout_ref, gmix_ref, h_ref,
                      gpre_ref, wq_ref, k_ref, v_ref, wo_ref, gpost_ref, o_ref, *, tm, seq):
    pos0 = (pl.program_id(0) * tm) % seq
    halo = jnp.where(pos0 == 0, 0.0, halo_ref[...])
    pool = _pool(u_ref[...], halo, pw_ref, ps_ref, pos0, tm)
    gd = GROUP_DIM
    a = (_dot(sb_ref[...], wout_ref[0:gd, :]) + _dot(fox_ref[...], wout_ref[gd:2 * gd, :])
         + _dot(pool.astype(BF16), wout_ref[2 * gd:, :]))
    h = h_ref[...] + _rms(a, gmix_ref[...])

    d = h.shape[-1]
    hd = d // XA_HEADS
    hn = _rms(h, gpre_ref[...]).astype(BF16)
    q = (_dot(hn, wq_ref[...]) * (hd ** -0.5)).astype(BF16)
    outs = []
    for head in range(XA_HEADS):
        sl = slice(head * hd, (head + 1) * hd)
        s = lax.dot_general(q[:, sl], k_ref[:, sl], _NT, preferred_element_type=F32)
        p = jnp.exp(s - jnp.max(s, axis=1, keepdims=True))
        p = p * (1.0 / jnp.sum(p, axis=1, keepdims=True))
        outs.append(_dot(p.astype(BF16), v_ref[:, sl]).astype(BF16))
    c = _dot(jnp.concatenate(outs, axis=1), wo_ref[...])
    o_ref[...] = h + _rms(c, gpost_ref[...])


def _mix_xattn(sb, fox, u, pool_w_bd, pool_scale, w_out, g_mix, h, g_pre, wq, k_mem, v_mem, wo, g_post,
               seq, tm=1024):
    n, d = h.shape
    m_len = k_mem.shape[1]
    per_seq = seq // tm
    hb = tm // POOL_HALO
    const = lambda shape: pl.BlockSpec(shape, lambda i: (0,) * len(shape))
    rows = lambda width: pl.BlockSpec((tm, width), lambda i: (i, 0))
    mem = pl.BlockSpec((None, m_len, d), lambda i: (i // per_seq, 0, 0))
    return pl.pallas_call(
        functools.partial(_mix_xattn_kernel, tm=tm, seq=seq),
        grid=(n // tm,),
        in_specs=[rows(GROUP_DIM), rows(GROUP_DIM), rows(POOL_DIM),
                  pl.BlockSpec((POOL_HALO, POOL_DIM), lambda i: (jnp.maximum(i * hb - 1, 0), 0)),
                  const((POOL_DIM, POOL_DIM)), const((1, POOL_DIM)), const(w_out.shape), const((1, d)),
                  rows(d), const((1, d)), const((d, d)), mem, mem, const((d, d)), const((1, d))],
        out_specs=rows(d),
        out_shape=jax.ShapeDtypeStruct((n, d), F32),
        compiler_params=_params(1),
        name="mix_xattn",
    )(sb, fox, u, u, pool_w_bd, pool_scale, w_out, g_mix, h, g_pre, wq, k_mem, v_mem, wo, g_post)


def _silu(x):
    return x / (1.0 + jnp.exp(-x))


def _swiglu(x, wg_ref, wu_ref, wd_ref, act_ref):
    ff = wg_ref.shape[1]
    for c in range(0, ff, FF_CHUNK):
        gate = _dot(x, wg_ref[:, c:c + FF_CHUNK])
        up = _dot(x, wu_ref[:, c:c + FF_CHUNK])
        act_ref[:, c:c + FF_CHUNK] = (_silu(gate) * up).astype(BF16)
    return _dot(act_ref[...], wd_ref[...])


def _ffn_kernel(h_ref, gpre_ref, wg_ref, wu_ref, wd_ref, gpost_ref, o_ref, act_ref):
    h = h_ref[...]
    f = _swiglu(_rms(h, gpre_ref[...]).astype(BF16), wg_ref, wu_ref, wd_ref, act_ref)
    o_ref[...] = h + _rms(f, gpost_ref[...])


def _ffn(h, g_pre, wg, wu, wd, g_post, tm=512):
    n, d = h.shape
    ff = wg.shape[1]
    resident = pl.Buffered(1)
    return pl.pallas_call(
        _ffn_kernel,
        grid=(n // tm,),
        in_specs=[pl.BlockSpec((tm, d), lambda i: (i, 0)),
                  pl.BlockSpec((1, d), lambda i: (0, 0)),
                  pl.BlockSpec((d, ff), lambda i: (0, 0), pipeline_mode=resident),
                  pl.BlockSpec((d, ff), lambda i: (0, 0), pipeline_mode=resident),
                  pl.BlockSpec((ff, d), lambda i: (0, 0), pipeline_mode=resident),
                  pl.BlockSpec((1, d), lambda i: (0, 0))],
        out_specs=pl.BlockSpec((tm, d), lambda i: (i, 0)),
        out_shape=jax.ShapeDtypeStruct((n, d), F32),
        scratch_shapes=[pltpu.VMEM((tm, ff), BF16)],
        compiler_params=_params(1),
        name="ffn_dense",
    )(h, g_pre, wg, wu, wd, g_post)


def _router_kernel(h_ref, g_ref, wr_ref, tri_ref, idx_ref, gate_ref, count_ref, carry_ref):
    @pl.when(pl.program_id(0) == 0)
    def _():
        carry_ref[...] = jnp.zeros_like(carry_ref)

    hn = _rms(h_ref[...], g_ref[...])
    x_hi, x_mid, _ = _split3(hn)
    w_hi, w_mid = wr_ref[0], wr_ref[1]
    logits = _dot(x_hi, w_hi) + _dot(x_hi, w_mid) + _dot(x_mid, w_hi)
    tm = logits.shape[0]
    lane = lax.broadcasted_iota(jnp.int32, logits.shape, 1)
    logits = jnp.where(lane < N_EXPERTS, logits, -jnp.inf)
    m1 = jnp.max(logits, axis=1, keepdims=True)
    i1 = jnp.min(jnp.where(logits == m1, lane, LANES), axis=1, keepdims=True)
    rest = jnp.where(lane == i1, -jnp.inf, logits)
    m2 = jnp.max(rest, axis=1, keepdims=True)
    i2 = jnp.min(jnp.where(rest == m2, lane, LANES), axis=1, keepdims=True)
    e = jnp.exp(m2 - m1)
    g1 = 1.0 / (1.0 + e)
    gate_ref[...] = jnp.where(lane == 0, g1, jnp.where(lane == 1, e * g1, 0.0))

    onehot = jnp.where(lane == i1, 1.0, jnp.where(lane == i2, 1.0, 0.0))
    before = _dot(tri_ref[...], onehot.astype(BF16)) + carry_ref[...]
    r1 = jnp.sum(jnp.where(lane == i1, before, 0.0), axis=1, keepdims=True).astype(jnp.int32)
    r2 = jnp.sum(jnp.where(lane == i2, before, 0.0), axis=1, keepdims=True).astype(jnp.int32)
    idx_ref[...] = jnp.where(lane == 0, i1, jnp.where(lane == 1, i2,
                             jnp.where(lane == 2, r1, jnp.where(lane == 3, r2, 0))))
    total = before[tm - 1:tm, :] + onehot[tm - 1:tm, :]
    carry_ref[...] = total
    count_ref[...] = total


def _router(h, g, wr_split, tm=512):
    n, d = h.shape
    t = jnp.arange(tm)
    tri = (t[:, None] > t[None, :]).astype(BF16)
    return pl.pallas_call(
        _router_kernel,
        grid=(n // tm,),
        in_specs=[pl.BlockSpec((tm, d), lambda i: (i, 0)),
                  pl.BlockSpec((1, d), lambda i: (0, 0)),
                  pl.BlockSpec(wr_split.shape, lambda i: (0, 0, 0)),
                  pl.BlockSpec((tm, tm), lambda i: (0, 0))],
        out_specs=[pl.BlockSpec((tm, LANES), lambda i: (i, 0)),
                   pl.BlockSpec((tm, LANES), lambda i: (i, 0)),
                   pl.BlockSpec((1, LANES), lambda i: (0, 0))],
        out_shape=[jax.ShapeDtypeStruct((n, LANES), jnp.int32),
                   jax.ShapeDtypeStruct((n, LANES), F32),
                   jax.ShapeDtypeStruct((1, LANES), F32)],
        scratch_shapes=[pltpu.VMEM((1, LANES), F32)],
        compiler_params=_params(1),
        name="router",
    )(h, g, wr_split, tri)


def _rows_to_tiles(ref, x):
    m = x.shape[0]
    for s in range(SUBLANES):
        ref[pl.ds(s, m, stride=SUBLANES), :] = x[:, s * LANES:(s + 1) * LANES]


def _tiles_to_rows(ref, m):
    return jnp.concatenate([ref[pl.ds(s, m, stride=SUBLANES), :] for s in range(SUBLANES)], axis=1)


def _tile_rows(ref, first_row, rows):
    return ref.at[pl.ds(pl.multiple_of(first_row * SUBLANES, SUBLANES), rows * SUBLANES)]


def _dispatch_kernel(dest_ref, pad_start_ref, pad_len_ref, n_valid_ref, h_ref, g_ref, xs_hbm,
                     buf_ref, zero_ref, sem, zero_sem, *, tm):
    base = pl.program_id(0) * tm * TOP_K

    @pl.when(pl.program_id(0) == 0)
    def _():
        zero_ref[...] = jnp.zeros_like(zero_ref)
        n_blocks = n_valid_ref.shape[0]

        def zero_copy(first_row, rows):
            return pltpu.make_async_copy(_tile_rows(zero_ref, 0, rows), _tile_rows(xs_hbm, first_row, rows),
                                         zero_sem)

        def fill(wait):
            def go(cp):
                cp.wait() if wait else cp.start()

            for e in range(N_EXPERTS):
                pos = pad_start_ref[e]
                left = pad_len_ref[e]
                rows = MOE_ROWS // 2
                while rows >= 1:
                    take = (left & rows) != 0

                    @pl.when(take)
                    def _(pos=pos, rows=rows):
                        go(zero_copy(pos, rows))
                    pos = pos + jnp.where(take, rows, 0)
                    rows //= 2

            def blocks(b, carry):
                @pl.when(n_valid_ref[b] == 0)
                def _():
                    go(zero_copy(b * MOE_ROWS, MOE_ROWS))
                return carry
            lax.fori_loop(0, n_blocks, blocks, 0)

        fill(False)
        fill(True)

    _rows_to_tiles(buf_ref, _rms(h_ref[...], g_ref[...]))

    def start(r, carry):
        for k in range(TOP_K):
            dst = dest_ref[base + r * TOP_K + k]
            pltpu.make_async_copy(_tile_rows(buf_ref, r, 1), _tile_rows(xs_hbm, dst, 1), sem).start(priority=k)
        return carry

    lax.fori_loop(0, tm, start, 0, unroll=DMA_UNROLL)
    for _ in range(TOP_K):
        pltpu.make_async_copy(buf_ref, _tile_rows(xs_hbm, 0, tm), sem).wait()


def _dispatch(dest, pad_start, pad_len, n_valid, h, g, tm=1024):
    n, d = h.shape
    assert d == SUBLANES * LANES
    n_rows = n_valid.shape[0] * MOE_ROWS
    grid_spec = pltpu.PrefetchScalarGridSpec(
        num_scalar_prefetch=4,
        grid=(n // tm,),
        in_specs=[pl.BlockSpec((tm, d), lambda i, *_: (i, 0)),
                  pl.BlockSpec((1, d), lambda i, *_: (0, 0))],
        out_specs=pl.BlockSpec(memory_space=pl.ANY),
        scratch_shapes=[pltpu.VMEM((tm * SUBLANES, LANES), F32), pltpu.VMEM((MOE_ROWS * SUBLANES, LANES), F32),
                        pltpu.SemaphoreType.DMA, pltpu.SemaphoreType.DMA],
    )
    return pl.pallas_call(
        functools.partial(_dispatch_kernel, tm=tm),
        grid_spec=grid_spec,
        out_shape=jax.ShapeDtypeStruct((n_rows * SUBLANES, LANES), F32),
        compiler_params=_params(1),
        name="moe_dispatch",
    )(dest, pad_start, pad_len, n_valid, h, g)


def _expert_kernel(blk_e_ref, n_valid_ref, xs_ref, wg_ref, wu_ref, wd_ref, ys_ref, act_ref):
    n_valid = n_valid_ref[pl.program_id(0)]

    @pl.when(n_valid > 0)
    def _():
        x = _tiles_to_rows(xs_ref, MOE_ROWS).astype(BF16)
        _rows_to_tiles(ys_ref, _swiglu(x, wg_ref, wu_ref, wd_ref, act_ref))

    @pl.when(n_valid == 0)
    def _():
        ys_ref[...] = jnp.zeros_like(ys_ref)


def _experts(blk_e, n_valid, xs, wg, wu, wd):
    d, ff = wg.shape[1], wg.shape[2]
    blk = MOE_ROWS * SUBLANES
    grid_spec = pltpu.PrefetchScalarGridSpec(
        num_scalar_prefetch=2,
        grid=(xs.shape[0] // blk,),
        in_specs=[pl.BlockSpec((blk, LANES), lambda i, be, nv: (i, 0)),
                  pl.BlockSpec((None, d, ff), lambda i, be, nv: (be[i], 0, 0)),
                  pl.BlockSpec((None, d, ff), lambda i, be, nv: (be[i], 0, 0)),
                  pl.BlockSpec((None, ff, d), lambda i, be, nv: (be[i], 0, 0))],
        out_specs=pl.BlockSpec((blk, LANES), lambda i, be, nv: (i, 0)),
        scratch_shapes=[pltpu.VMEM((MOE_ROWS, ff), BF16)],
    )
    return pl.pallas_call(
        _expert_kernel,
        grid_spec=grid_spec,
        out_shape=jax.ShapeDtypeStruct(xs.shape, F32),
        compiler_params=_params(1),
        name="moe_experts",
    )(blk_e, n_valid, xs, wg, wu, wd)


def _combine_kernel(dest_ref, ys_hbm, gate_ref, g_ref, h_ref, o_ref, buf_ref, sem, *, tm):
    i = pl.program_id(0)

    def fetch(tile, slot):
        def body(r, carry):
            for k in range(TOP_K):
                src = dest_ref[(tile * tm + r) * TOP_K + k]
                pltpu.make_async_copy(_tile_rows(ys_hbm, src, 1), _tile_rows(buf_ref.at[slot, k], r, 1),
                                      sem.at[slot]).start(priority=k)
            return carry
        lax.fori_loop(0, tm, body, 0, unroll=DMA_UNROLL)

    @pl.when(i == 0)
    def _():
        fetch(0, 0)

    @pl.when(i + 1 < pl.num_programs(0))
    def _():
        fetch(i + 1, (i + 1) % 2)

    slot = i % 2

    for k in range(TOP_K):
        pltpu.make_async_copy(_tile_rows(ys_hbm, 0, tm), buf_ref.at[slot, k], sem.at[slot]).wait()
    gates = gate_ref[...]
    f = (_tiles_to_rows(buf_ref.at[slot, 0], tm) * gates[:, 0:1]
         + _tiles_to_rows(buf_ref.at[slot, 1], tm) * gates[:, 1:2])
    o_ref[...] = h_ref[...] + _rms(f, g_ref[...])


def _combine(dest, ys, gates, g, h, tm=512):
    n, d = h.shape
    grid_spec = pltpu.PrefetchScalarGridSpec(
        num_scalar_prefetch=1,
        grid=(n // tm,),
        in_specs=[pl.BlockSpec(memory_space=pl.ANY),
                  pl.BlockSpec((tm, LANES), lambda i, dst: (i, 0)),
                  pl.BlockSpec((1, d), lambda i, dst: (0, 0)),
                  pl.BlockSpec((tm, d), lambda i, dst: (i, 0))],
        out_specs=pl.BlockSpec((tm, d), lambda i, dst: (i, 0)),
        scratch_shapes=[pltpu.VMEM((2, TOP_K, tm * SUBLANES, LANES), F32), pltpu.SemaphoreType.DMA((2,))],
    )
    return pl.pallas_call(
        functools.partial(_combine_kernel, tm=tm),
        grid_spec=grid_spec,
        out_shape=jax.ShapeDtypeStruct((n, d), F32),
        compiler_params=_params(1),
        name="moe_combine",
    )(dest, ys, gates, g, h)


def _route_plan(idx, counts_f, n_tok):
    counts = counts_f[0, :N_EXPERTS].astype(jnp.int32)
    padded = ((counts + MOE_ROWS - 1) // MOE_ROWS) * MOE_ROWS
    pend = jnp.cumsum(padded)
    pstart = pend - padded
    expert = idx[:, 0:TOP_K]
    rank = idx[:, TOP_K:2 * TOP_K]
    offset = jnp.zeros_like(expert)
    for e in range(N_EXPERTS):
        offset = jnp.where(expert == e, pstart[e], offset)
    dest = (offset + rank).reshape(-1)
    n_blocks = (n_tok * TOP_K) // MOE_ROWS + N_EXPERTS
    blk_start = jnp.arange(n_blocks, dtype=jnp.int32) * MOE_ROWS
    blk_e = jnp.minimum(jnp.sum(blk_start[:, None] >= pend[None, :], axis=1), N_EXPERTS - 1).astype(jnp.int32)
    n_valid = jnp.clip(pstart[blk_e] + counts[blk_e] - blk_start, 0, MOE_ROWS).astype(jnp.int32)
    return dest, pstart + counts, padded - counts, blk_e, n_valid


def _moe(h, g_pre, router_w, wg, wu, wd, g_post):
    n, d = h.shape
    wr = jnp.pad(router_w, ((0, 0), (0, LANES - N_EXPERTS)))
    wr_hi = wr.astype(BF16)
    wr_mid = (wr - wr_hi.astype(F32)).astype(BF16)
    idx, gates, counts = _router(h, g_pre, jnp.stack([wr_hi, wr_mid]))
    dest, pad_start, pad_len, blk_e, n_valid = _route_plan(idx, counts, n)
    xs = _dispatch(dest, pad_start, pad_len, n_valid, h, g_pre)
    ys = _experts(blk_e, n_valid, xs, wg, wu, wd)
    return _combine(dest, ys, gates, g_post, h)


def kernel(x, mem, mix_norm_pre, mix_norm_post, w_in, b_forget, pool_w, pool_scale, w_out,
           xa_norm_pre, xa_norm_post, mem_norm, xa_wq, xa_wkv, xa_wo,
           ffn_norm_pre, ffn_norm_post, dense_w_gate, dense_w_up, dense_w_down,
           router_w, moe_w_gate, moe_w_up, moe_w_down):
    bsz, seq, d = x.shape
    m_len = mem.shape[1]
    depth = w_in.shape[0]
    n = bsz * seq
    h = x.reshape(n, d)
    mem2 = mem.reshape(bsz * m_len, d)
    row = lambda v: v.reshape(1, -1)

    idx = jnp.arange(SB_SUB)
    tri = (idx[:, None] >= idx[None, :]).astype(BF16)

    for li in range(depth):
        w = w_in[li]
        flog_w = jnp.pad(w[:, QKV_DIM:QKV_DIM + N_FOX], ((0, 0), (0, LANES - N_FOX)))
        w_cat = jnp.concatenate([w[:, :QKV_DIM], w[:, QKV_DIM + N_FOX:], flog_w], axis=1).astype(BF16)
        qkv, u, flog, qn, kmax_tiles = _in_proj(h, row(mix_norm_pre[li]), w_cat)
        qkv = qkv.reshape(bsz, seq, QKV_DIM)
        kmax = jnp.max(kmax_tiles[:, 0, :].reshape(bsz, -1, LANES), axis=1, keepdims=True)

        b_pad = jnp.pad(b_forget[li], (0, LANES - N_FOX)).reshape(1, LANES)
        by_seq = lambda v: v.reshape(bsz, seq, LANES)
        c, a_cols, zb_tiles = _logf_cumsum(by_seq(flog), b_pad, by_seq(qn), kmax)
        by_pair = lambda v: v[..., :N_FOX].transpose(0, 2, 1).reshape(bsz, N_PAIRS, 2, -1)
        c_rows = by_pair(c)
        zb_max = by_pair(zb_tiles[:, :, 0, :])

        sb = _sb_attention(qkv, tri).reshape(n, GROUP_DIM)
        fox = _fox_attention(qkv, a_cols, c_rows, zb_max).reshape(n, GROUP_DIM)

        pool_bd = jax.scipy.linalg.block_diag(*[pool_w[li, gi] for gi in range(len(POOL_WINDOWS))])
        k_mem, v_mem = _norm_matmul(mem2, row(mem_norm[li]), xa_wkv[li].astype(BF16),
                                    splits=[(0, d), (d, d)], dtypes=[BF16, BF16], tm=m_len)
        h = _mix_xattn(sb, fox, u, pool_bd.astype(BF16), row(pool_scale[li]), w_out[li].astype(BF16),
                       row(mix_norm_post[li]), h, row(xa_norm_pre[li]), xa_wq[li].astype(BF16),
                       k_mem.reshape(bsz, m_len, d), v_mem.reshape(bsz, m_len, d),
                       xa_wo[li].astype(BF16), row(xa_norm_post[li]), seq)

        j = li // 2
        if li % 2 == 0:
            h = _ffn(h, row(ffn_norm_pre[li]), dense_w_gate[j].astype(BF16), dense_w_up[j].astype(BF16),
                     dense_w_down[j].astype(BF16), row(ffn_norm_post[li]))
        else:
            h = _moe(h, row(ffn_norm_pre[li]), router_w[j], moe_w_gate[j].astype(BF16),
                     moe_w_up[j].astype(BF16), moe_w_down[j].astype(BF16), row(ffn_norm_post[li]))
    return h.reshape(bsz, seq, d)
```
